```python
import math
import jax, jax.numpy as jnp
from jax import lax
import numpy as np

D_MODEL = 1024
BATCH = 8
SEQ = 4096
DEPTH = 1

HEAD_DIM = 64
NSA_HEADS = 8
NSA_GROUPS = 2
NSA_HPG = NSA_HEADS // NSA_GROUPS
CMP_BLOCK = 32
CMP_STRIDE = 16
CMP_HIDDEN = 256
SLC_BLOCK = 64
SLC_TOPN = 16
SLC_FORCE = 1e4
WINDOW = 512
DSA_HEADS = 8
KV_RANK = 128
IDX_HEADS = 4
IDX_DIM = 64
IDX_TOPK_MAX = 256
REL_BUCKETS = 32
REL_EXACT = 16
REL_MAX_DIST = 1024
N_REL_HEADS = NSA_HEADS + DSA_HEADS
N_EXPERT_GROUPS = 4
EXPERTS_PER_GROUP = 8
N_EXPERTS = N_EXPERT_GROUPS * EXPERTS_PER_GROUP
EXPERT_TOPK = 2
D_EXPERT = 256
EXPERT_BLOCK = 256
QBLK = 128
NSA_WIDTH = NSA_HEADS * HEAD_DIM
DSA_WIDTH = DSA_HEADS * HEAD_DIM
ALPHA = (2.0 * DEPTH) ** 0.25
BETA = (8.0 * DEPTH) ** -0.25
NEG = -1e30
SPLIT_SIZES = (NSA_WIDTH,) + (NSA_GROUPS * HEAD_DIM,) * 6 + (NSA_HEADS * 3, DSA_WIDTH, KV_RANK, IDX_HEADS * IDX_DIM, IDX_DIM, IDX_HEADS, D_MODEL, D_MODEL)
D_IN = int(sum(SPLIT_SIZES))
SPLIT_POINTS = tuple(int(v) for v in np.cumsum(SPLIT_SIZES)[:-1])

kernel_name = 'hybrid_nsa_dsa_hmoe_deepnorm'


def layer_norm(x, g, b, eps=1e-5):
    xf = x.astype(jnp.float32)
    xc = xf - jnp.mean(xf, axis=-1, keepdims=True)
    var = jnp.mean(xc * xc, axis=-1, keepdims=True)
    return (xc * lax.rsqrt(var + eps) * g + b).astype(x.dtype)


def rms_norm(x, g, eps=1e-6):
    xf = x.astype(jnp.float32)
    return (xf * lax.rsqrt(jnp.mean(xf * xf, axis=-1, keepdims=True) + eps) * g).astype(x.dtype)


def masked_softmax(logits, mask):
    l = jnp.where(mask, logits.astype(jnp.float32), NEG)
    m = jnp.max(l, axis=-1, keepdims=True)
    e = jnp.where(mask, jnp.exp(l - m), 0.0)
    p = e / jnp.maximum(jnp.sum(e, axis=-1, keepdims=True), 1e-30)
    return p.astype(logits.dtype)


def rel_bucket(dist):
    n = jnp.maximum(dist, 0)
    nf = jnp.maximum(n, 1).astype(jnp.float32)
    large = REL_EXACT + (jnp.log(nf / REL_EXACT) / math.log(REL_MAX_DIST / REL_EXACT) * (REL_BUCKETS - REL_EXACT)).astype(jnp.int32)
    return jnp.where(n < REL_EXACT, n, jnp.minimum(large, REL_BUCKETS - 1))


def compress_blocks(tok, pe, w1, w2):
    b, t, g, d = tok.shape
    nc = (t - CMP_BLOCK) // CMP_STRIDE + 1
    idx = CMP_STRIDE * np.arange(nc)[:, None] + np.arange(CMP_BLOCK)[None, :]
    blk = tok[:, idx] + pe[None, None, :, None, :]
    blk = jnp.moveaxis(blk, 3, 2).reshape(b, nc, g, CMP_BLOCK * d)
    return jax.nn.gelu(blk @ w1) @ w2


def cmp_to_slc_map(t):
    nc = (t - CMP_BLOCK) // CMP_STRIDE + 1
    ns = t // SLC_BLOCK
    cs = CMP_STRIDE * np.arange(nc)[:, None]
    ss = SLC_BLOCK * np.arange(ns)[None, :]
    ov = np.minimum(cs + CMP_BLOCK, ss + SLC_BLOCK) - np.maximum(cs, ss)
    return np.clip(ov, 0, None).astype(np.float32) / CMP_STRIDE


def nsa_block(q, k_cmp, v_cmp, k_slc, v_slc, k_win, v_win, gates, rel_a, cmp_map, tq, q0):
    scale = HEAD_DIM ** -0.5
    nc = k_cmp.shape[0]
    t = k_slc.shape[0]
    ns = t // SLC_BLOCK
    n_top = min(SLC_TOPN, ns)
    g_ar = jnp.arange(NSA_GROUPS)[:, None, None]
    cmp_end = CMP_STRIDE * jnp.arange(nc) + CMP_BLOCK - 1
    dist = tq[:, None] - cmp_end[None, :]
    logits = jnp.einsum('tghd,cgd->ghtc', q, k_cmp) * scale + rel_a[rel_bucket(dist)].transpose(2, 3, 0, 1)
    p_cmp = masked_softmax(logits, dist >= 0)
    o_cmp = jnp.einsum('ghtc,cgd->tghd', p_cmp, v_cmp)
    imp = jnp.einsum('ghtc,cn->gtn', p_cmp, cmp_map)
    blk = jnp.arange(ns)[None, :]
    cur = (tq // SLC_BLOCK)[:, None]
    forced = (blk == 0) | (blk == cur) | (blk == cur - 1)
    avail = blk * SLC_BLOCK <= tq[:, None]
    score = jnp.where(avail, imp.astype(jnp.float32) + SLC_FORCE * forced, NEG)
    sel_score, sel_idx = lax.top_k(score, n_top)
    sel_ok = sel_score > 0.5 * NEG
    k_sb = k_slc.reshape(ns, SLC_BLOCK, NSA_GROUPS, HEAD_DIM).transpose(2, 0, 1, 3)
    v_sb = v_slc.reshape(ns, SLC_BLOCK, NSA_GROUPS, HEAD_DIM).transpose(2, 0, 1, 3)
    k_g = k_sb[g_ar, sel_idx].reshape(NSA_GROUPS, QBLK, n_top * SLC_BLOCK, HEAD_DIM)
    v_g = v_sb[g_ar, sel_idx].reshape(NSA_GROUPS, QBLK, n_top * SLC_BLOCK, HEAD_DIM)
    kpos = (sel_idx[..., None] * SLC_BLOCK + jnp.arange(SLC_BLOCK)).reshape(NSA_GROUPS, QBLK, n_top * SLC_BLOCK)
    dist = tq[None, :, None] - kpos
    mask = jnp.repeat(sel_ok, SLC_BLOCK, axis=-1) & (dist >= 0)
    bias = jnp.moveaxis(rel_a[rel_bucket(dist), g_ar], -1, 1)
    logits = jnp.einsum('tghd,gtkd->ghtk', q, k_g) * scale + bias
    p = masked_softmax(logits, mask[:, None])
    o_slc = jnp.einsum('ghtk,gtkd->tghd', p, v_g)
    kw = lax.dynamic_slice_in_dim(k_win, q0, WINDOW + QBLK, axis=0)
    vw = lax.dynamic_slice_in_dim(v_win, q0, WINDOW + QBLK, axis=0)
    kpos = q0 - WINDOW + jnp.arange(WINDOW + QBLK)
    dist = tq[:, None] - kpos[None, :]
    mask = (kpos[None, :] >= 0) & (dist >= 0) & (dist < WINDOW)
    logits = jnp.einsum('tghd,kgd->ghtk', q, kw) * scale + rel_a[rel_bucket(dist)].transpose(2, 3, 0, 1)
    p = masked_softmax(logits, mask)
    o_win = jnp.einsum('ghtk,kgd->tghd', p, vw)
    o = gates[..., 0:1] * o_cmp + gates[..., 1:2] * o_slc + gates[..., 2:3] * o_win
    return o.reshape(QBLK, NSA_WIDTH)


def dsa_block(q_lat, ckv, q_idx, k_idx, w_idx, rel_b, tq):
    t = ckv.shape[0]
    k_sel = min(IDX_TOPK_MAX, t // 4)
    rs = jax.nn.relu(jnp.einsum('tid,sd->tis', q_idx, k_idx) * IDX_DIM ** -0.5)
    index = jnp.einsum('tis,ti->ts', rs, w_idx).astype(jnp.float32)
    index = jnp.where(jnp.arange(t)[None, :] <= tq[:, None], index, NEG)
    _, sel = lax.top_k(index, k_sel)
    ok = sel <= tq[:, None]
    c_g = ckv[sel]
    bias = rel_b[rel_bucket(tq[:, None] - sel)].transpose(2, 0, 1)
    logits = jnp.einsum('thr,tkr->htk', q_lat, c_g) * HEAD_DIM ** -0.5 + bias
    p = masked_softmax(logits, ok[None])
    return jnp.einsum('htk,tkr->thr', p, c_g)


def moe(h, w_grp, b_grp, w_rtr, b_rtr, w_gate, w_up, w_down):
    b, t, d = h.shape
    n = b * t
    hf = h.reshape(n, d)
    g_logits = (hf @ w_grp + b_grp).astype(jnp.float32)
    g_sel = jnp.argmax(g_logits, axis=-1)
    p_grp = jnp.take_along_axis(jax.nn.softmax(g_logits, axis=-1), g_sel[:, None], axis=1)[:, 0]
    e_logits = (hf @ w_rtr + b_rtr).astype(jnp.float32).reshape(n, N_EXPERT_GROUPS, EXPERTS_PER_GROUP)
    e_logits = jnp.take_along_axis(e_logits, g_sel[:, None, None], axis=1)[:, 0]
    top_l, top_i = lax.top_k(e_logits, EXPERT_TOPK)
    p_exp = jax.nn.softmax(top_l, axis=-1)
    w_a = (p_grp[:, None] * p_exp).reshape(-1)
    e_a = (g_sel[:, None] * EXPERTS_PER_GROUP + top_i).reshape(-1).astype(jnp.int32)
    tok_a = jnp.repeat(jnp.arange(n, dtype=jnp.int32), EXPERT_TOPK)
    n_a = n * EXPERT_TOPK
    order = jnp.argsort(e_a)
    e_s, tok_s, w_s = e_a[order], tok_a[order], w_a[order]
    counts = jax.ops.segment_sum(jnp.ones_like(e_a), e_a, num_segments=N_EXPERTS)
    padded = (counts + EXPERT_BLOCK - 1) // EXPERT_BLOCK * EXPERT_BLOCK
    off = jnp.cumsum(counts) - counts
    pend = jnp.cumsum(padded)
    poff = pend - padded
    dest = poff[e_s] + jnp.arange(n_a, dtype=jnp.int32) - off[e_s]
    n_blk = -(-n_a // EXPERT_BLOCK) + N_EXPERTS
    tok_pad = jnp.zeros((n_blk * EXPERT_BLOCK,), jnp.int32).at[dest].set(tok_s)
    w_pad = jnp.zeros((n_blk * EXPERT_BLOCK,), jnp.float32).at[dest].set(w_s)
    blk_exp = jnp.minimum(jnp.searchsorted(pend, jnp.arange(n_blk, dtype=jnp.int32) * EXPERT_BLOCK, side='right'), N_EXPERTS - 1)
    x_pad = hf[tok_pad].reshape(n_blk, EXPERT_BLOCK, d)

    def expert_block(args):
        xb, e = args
        return (jax.nn.silu(xb @ w_gate[e]) * (xb @ w_up[e])) @ w_down[e]

    y = lax.map(expert_block, (x_pad, blk_exp)).reshape(-1, d)
    out = jnp.zeros_like(hf).at[tok_pad].add(y * w_pad[:, None].astype(y.dtype))
    return out.reshape(b, t, d)


def setup_inputs(seed: int = 0) -> dict:
    key = jax.random.key(seed)
    ks = jax.random.split(key, 26)
    L = DEPTH

    def nrm(k, shape, scale):
        return jax.random.normal(k, shape, jnp.float32) * scale

    return {
        'x': nrm(ks[0], (BATCH, SEQ, D_MODEL), 1.0),
        'w_in': nrm(ks[1], (L, D_MODEL, D_IN), D_MODEL ** -0.5),
        'cmp_pe_k': nrm(ks[2], (L, CMP_BLOCK, HEAD_DIM), 0.1),
        'cmp_pe_v': nrm(ks[3], (L, CMP_BLOCK, HEAD_DIM), 0.1),
        'cmp_w1_k': nrm(ks[4], (L, CMP_BLOCK * HEAD_DIM, CMP_HIDDEN), (CMP_BLOCK * HEAD_DIM) ** -0.5),
        'cmp_w2_k': nrm(ks[5], (L, CMP_HIDDEN, HEAD_DIM), CMP_HIDDEN ** -0.5),
        'cmp_w1_v': nrm(ks[6], (L, CMP_BLOCK * HEAD_DIM, CMP_HIDDEN), (CMP_BLOCK * HEAD_DIM) ** -0.5),
        'cmp_w2_v': nrm(ks[7], (L, CMP_HIDDEN, HEAD_DIM), CMP_HIDDEN ** -0.5),
        'ckv_norm_g': 1.0 + nrm(ks[8], (L, KV_RANK), 0.02),
        'w_uk': nrm(ks[9], (L, DSA_HEADS, HEAD_DIM, KV_RANK), KV_RANK ** -0.5),
        'w_uv': nrm(ks[10], (L, DSA_HEADS, KV_RANK, HEAD_DIM), KV_RANK ** -0.5),
        'rel_bias': nrm(ks[11], (REL_BUCKETS, N_REL_HEADS), 0.1),
        'w_branch_a': nrm(ks[12], (L, NSA_WIDTH, D_MODEL), NSA_WIDTH ** -0.5 * BETA),
        'w_branch_b': nrm(ks[13], (L, DSA_WIDTH, D_MODEL), DSA_WIDTH ** -0.5 * BETA),
        'w_out': nrm(ks[14], (L, D_MODEL, D_MODEL), D_MODEL ** -0.5 * BETA),
        'ln1_g': 1.0 + nrm(ks[15], (L, D_MODEL), 0.02),
        'ln1_b': nrm(ks[16], (L, D_MODEL), 0.02),
        'w_grp': nrm(ks[17], (L, D_MODEL, N_EXPERT_GROUPS), D_MODEL ** -0.5),
        'b_grp': nrm(ks[18], (L, N_EXPERT_GROUPS), 0.01),
        'w_rtr': nrm(ks[19], (L, D_MODEL, N_EXPERTS), D_MODEL ** -0.5),
        'b_rtr': nrm(ks[20], (L, N_EXPERTS), 0.01),
        'w_gate': nrm(ks[21], (L, N_EXPERTS, D_MODEL, D_EXPERT), D_MODEL ** -0.5),
        'w_up': nrm(ks[22], (L, N_EXPERTS, D_MODEL, D_EXPERT), D_MODEL ** -0.5),
        'w_down': nrm(ks[23], (L, N_EXPERTS, D_EXPERT, D_MODEL), D_EXPERT ** -0.5 * BETA),
        'ln2_g': 1.0 + nrm(ks[24], (L, D_MODEL), 0.02),
        'ln2_b': nrm(ks[25], (L, D_MODEL), 0.02),
    }


def reference(x, w_in, cmp_pe_k, cmp_pe_v, cmp_w1_k, cmp_w2_k, cmp_w1_v, cmp_w2_v, ckv_norm_g, w_uk, w_uv, rel_bias, w_branch_a, w_branch_b, w_out, ln1_g, ln1_b, w_grp, b_grp, w_rtr, b_rtr, w_gate, w_up, w_down, ln2_g, ln2_b):
    b, t, _ = x.shape
    n_qblk = t // QBLK
    cmp_map = jnp.asarray(cmp_to_slc_map(t))
    rel_a = rel_bias[:, :NSA_HEADS].reshape(REL_BUCKETS, NSA_GROUPS, NSA_HPG)
    rel_b = rel_bias[:, NSA_HEADS:]
    kv_shape = (b, t, NSA_GROUPS, HEAD_DIM)
    pad = ((0, 0), (WINDOW, 0), (0, 0), (0, 0))
    h = x
    for l in range(DEPTH):
        z = jnp.einsum('btd,de->bte', h, w_in[l])
        (q_a, kc, vc, ks, vs, kw, vw, g_nsa, q_b, ckv, q_idx, k_idx, w_idx, gate_a, gate_b) = jnp.split(z, SPLIT_POINTS, axis=-1)
        q_a = q_a.reshape(b, t, NSA_GROUPS, NSA_HPG, HEAD_DIM)
        k_cmp = compress_blocks(kc.reshape(kv_shape), cmp_pe_k[l], cmp_w1_k[l], cmp_w2_k[l])
        v_cmp = compress_blocks(vc.reshape(kv_shape), cmp_pe_v[l], cmp_w1_v[l], cmp_w2_v[l])
        ks = ks.reshape(kv_shape)
        vs = vs.reshape(kv_shape)
        kw = jnp.pad(kw.reshape(kv_shape), pad)
        vw = jnp.pad(vw.reshape(kv_shape), pad)
        g_nsa = jax.nn.sigmoid(g_nsa.reshape(b, t, NSA_GROUPS, NSA_HPG, 3))
        q_lat = jnp.einsum('bthd,hdr->bthr', q_b.reshape(b, t, DSA_HEADS, HEAD_DIM), w_uk[l])
        ckv = rms_norm(ckv, ckv_norm_g[l])
        q_idx = q_idx.reshape(b, t, IDX_HEADS, IDX_DIM)
        w_idx = w_idx * IDX_HEADS ** -0.5

        def sweep(i):
            bi = i // n_qblk
            q0 = (i % n_qblk) * QBLK
            tq = q0 + jnp.arange(QBLK)

            def sl(a):
                return lax.dynamic_slice_in_dim(a[bi], q0, QBLK, axis=0)

            o_a = nsa_block(sl(q_a), k_cmp[bi], v_cmp[bi], ks[bi], vs[bi], kw[bi], vw[bi], sl(g_nsa), rel_a, cmp_map, tq, q0)
            o_lat = dsa_block(sl(q_lat), ckv[bi], sl(q_idx), k_idx[bi], sl(w_idx), rel_b, tq)
            return o_a, o_lat

        o_a, o_lat = lax.map(sweep, jnp.arange(b * n_qblk))
        o_a = o_a.reshape(b, t, NSA_WIDTH)
        o_b = jnp.einsum('bthr,hrd->bthd', o_lat.reshape(b, t, DSA_HEADS, KV_RANK), w_uv[l]).reshape(b, t, DSA_WIDTH)
        merged = jax.nn.sigmoid(gate_a) * (o_a @ w_branch_a[l]) + jax.nn.sigmoid(gate_b) * (o_b @ w_branch_b[l])
        h = layer_norm(ALPHA * h + merged @ w_out[l], ln1_g[l], ln1_b[l])
        h = layer_norm(ALPHA * h + moe(h, w_grp[l], b_grp[l], w_rtr[l], b_rtr[l], w_gate[l], w_up[l], w_down[l]), ln2_g[l], ln2_b[l])
    return h
```

```python
import functools
import math

import numpy as np
import jax
import jax.numpy as jnp
from jax import lax
from jax.experimental import pallas as pl
from jax.experimental.pallas import tpu as pltpu

F32 = jnp.float32
BF16 = jnp.bfloat16
I32 = jnp.int32

D_MODEL = 1024
HEAD_DIM = 64
NSA_HEADS = 8
NSA_GROUPS = 2
NSA_HPG = NSA_HEADS // NSA_GROUPS
CMP_BLOCK = 32
CMP_STRIDE = 16
CMP_HIDDEN = 256
SLC_BLOCK = 64
SLC_TOPN = 16
SLC_FORCE = 1e4
WINDOW = 512
DSA_HEADS = 8
KV_RANK = 128
IDX_HEADS = 4
IDX_DIM = 64
IDX_TOPK_MAX = 256
REL_BUCKETS = 32
REL_EXACT = 16
REL_MAX_DIST = 1024
N_EXPERT_GROUPS = 4
EXPERTS_PER_GROUP = 8
N_EXPERTS = N_EXPERT_GROUPS * EXPERTS_PER_GROUP
EXPERT_TOPK = 2
D_EXPERT = 256
EXPERT_BLOCK = 256
NSA_WIDTH = NSA_HEADS * HEAD_DIM
DSA_WIDTH = DSA_HEADS * HEAD_DIM
NEG = -1e30
SPLIT_SIZES = (NSA_WIDTH,) + (NSA_GROUPS * HEAD_DIM,) * 6 + (
    NSA_HEADS * 3, DSA_WIDTH, KV_RANK, IDX_HEADS * IDX_DIM, IDX_DIM, IDX_HEADS, D_MODEL, D_MODEL)
SPLIT_POINTS = tuple(int(v) for v in np.cumsum(SPLIT_SIZES)[:-1])

LANE = 128
QT = 128
KT = 256
M_FLOOR = -1e29
BIG = 3e38

FAR_TILES = 9
STRIP_A = FAR_TILES * QT
STRIP_W = STRIP_A + KT
WIN_A = WINDOW
WIN_KEYS = WINDOW + QT
WIN_W = WIN_A + WIN_KEYS

O_QA, O_KC, O_VC, O_KV4, O_GN, O_QB, O_CKV, O_QIDX, O_KIDX, O_WIDX, O_GA, O_GB, PROJ_W = (
    0, 1024, 1152, 1280, 1792, 1920, 2432, 2560, 3072, 3200, 3328, 4352, 5376)

_NT = (((1,), (1,)), ((), ()))


def _dot(a, b):
    return jnp.dot(a, b, preferred_element_type=F32)


def _dot_nt(a, b):
    return lax.dot_general(a, b, _NT, preferred_element_type=F32)


def _cparams(sem, vmem_mb=None):
    kw = dict(dimension_semantics=sem)
    if vmem_mb is not None:
        kw["vmem_limit_bytes"] = vmem_mb * 1024 * 1024
    return pltpu.CompilerParams(**kw)


def _proj_weights(w_in):
    (q_a, kc, vc, ks, vs, kw, vw, g_nsa, q_b, ckv, q_idx, k_idx, w_idx, ga, gb) = jnp.split(w_in, SPLIT_POINTS, axis=1)
    d = w_in.shape[0]
    scale = HEAD_DIM ** -0.5
    z64 = jnp.zeros((d, NSA_HPG, HEAD_DIM), F32)
    qa = (q_a * scale).reshape(d, NSA_GROUPS, NSA_HPG, HEAD_DIM)
    qa_pad = jnp.concatenate([
        jnp.concatenate([qa[:, 0], z64], axis=-1).reshape(d, NSA_HPG * LANE),
        jnp.concatenate([z64, qa[:, 1]], axis=-1).reshape(d, NSA_HPG * LANE)], axis=1)
    qi = q_idx.reshape(d, IDX_HEADS, IDX_DIM)
    qi_pad = jnp.concatenate([qi, jnp.zeros_like(qi)], axis=-1).reshape(d, IDX_HEADS * LANE)

    def pad(a):
        return jnp.pad(a, ((0, 0), (0, LANE - a.shape[1])))

    w_idx_s = w_idx * (IDX_HEADS ** -0.5 * IDX_DIM ** -0.5)
    cols = [qa_pad, kc, vc, ks, vs, kw, vw, pad(g_nsa), q_b * scale, ckv, qi_pad, pad(k_idx), pad(w_idx_s), ga, gb]
    w = jnp.concatenate(cols, axis=1)
    assert w.shape[1] == PROJ_W
    return w.astype(BF16)


def _proj_kernel(x_ref, w_ref, wuk_ref, g_ref, qa_o, kc_o, vc_o, kv4_o, gn_o, qlat_o, ckv_o, qidx_o, kidx_o,
                 widx_o, sga_o, sgb_o):
    xb = x_ref[...].astype(BF16)

    def mm(lo, n):
        return _dot(xb, w_ref[:, lo:lo + n])

    qa_o[...] = mm(O_QA, 1024).astype(BF16)
    kc_o[...] = mm(O_KC, 128).astype(BF16)
    vc_o[...] = mm(O_VC, 128).astype(BF16)
    kv4_o[...] = mm(O_KV4, 512).astype(BF16)
    gn_o[...] = jax.nn.sigmoid(mm(O_GN, 128))
    qb = mm(O_QB, 512).astype(BF16)
    for k in range(DSA_HEADS // 2):
        qlat_o[:, 256 * k:256 * (k + 1)] = _dot(qb[:, 128 * k:128 * (k + 1)], wuk_ref[k]).astype(BF16)
    c = mm(O_CKV, 128)
    ms = jnp.mean(c * c, axis=-1, keepdims=True)
    ckv_o[...] = (c * lax.rsqrt(ms + 1e-6) * g_ref[...]).astype(BF16)
    qidx_o[...] = mm(O_QIDX, 512).astype(BF16)
    kidx_o[...] = mm(O_KIDX, 128).astype(BF16)
    widx_o[...] = mm(O_WIDX, 128)
    sga_o[...] = jax.nn.sigmoid(mm(O_GA, 1024)).astype(BF16)
    sgb_o[...] = jax.nn.sigmoid(mm(O_GB, 1024)).astype(BF16)


def _proj(x2, w_pad, wuk_pairs, ckv_g, rows=512):
    n, d = x2.shape
    widths = [(1024, BF16), (128, BF16), (128, BF16), (512, BF16), (128, F32), (1024, BF16), (128, BF16),
              (512, BF16), (128, BF16), (128, F32), (1024, BF16), (1024, BF16)]
    return pl.pallas_call(
        _proj_kernel,
        grid=(n // rows,),
        in_specs=[
            pl.BlockSpec((rows, d), lambda i: (i, 0)),
            pl.BlockSpec((d, PROJ_W), lambda i: (0, 0)),
            pl.BlockSpec((DSA_HEADS // 2, 128, 256), lambda i: (0, 0, 0)),
            pl.BlockSpec((1, KV_RANK), lambda i: (0, 0)),
        ],
        out_specs=[pl.BlockSpec((rows, w), lambda i: (i, 0)) for w, _ in widths],
        out_shape=[jax.ShapeDtypeStruct((n, w), dt) for w, dt in widths],
        compiler_params=_cparams(("arbitrary",), 56),
        name="proj",
    )(x2, w_pad, wuk_pairs, ckv_g)


def _cmp_weights(w1, w2, pe):
    half = CMP_BLOCK // 2
    w1r = w1.reshape(CMP_BLOCK, HEAD_DIM, CMP_HIDDEN)
    eye = jnp.eye(NSA_GROUPS, dtype=F32)

    def expand(wl):
        return jnp.einsum('ldj,gh->lgdhj', wl, eye).reshape(half * NSA_GROUPS * HEAD_DIM, NSA_GROUPS * CMP_HIDDEN)

    top, bot = expand(w1r[:half]), expand(w1r[half:])

    def pe_rows(p):
        return jnp.broadcast_to(p[:, None, :], (half, NSA_GROUPS, HEAD_DIM)).reshape(1, -1)

    w2bd = jnp.einsum('jd,gh->gjhd', w2, eye).reshape(NSA_GROUPS * CMP_HIDDEN, NSA_GROUPS * HEAD_DIM)
    return (top.astype(BF16), bot.astype(BF16), pe_rows(pe[:half]).astype(BF16), pe_rows(pe[half:]).astype(BF16),
            w2bd.astype(BF16))


def _compress_kernel(hk_ref, hv_ref, kt_ref, kb_ref, kpt_ref, kpb_ref, k2_ref, vt_ref, vb_ref, vpt_ref, vpb_ref,
                     v2_ref, ko_ref, vo_ref):
    ncp = hk_ref.shape[0]

    def one(h_ref, top_ref, bot_ref, pt_ref, pb_ref, w2_ref, o_ref):
        h = h_ref[...]
        a = _dot(h, top_ref[...])
        b = _dot(h, bot_ref[...])
        pe8t = jnp.broadcast_to(pt_ref[...], (8, pt_ref.shape[1]))
        pe8b = jnp.broadcast_to(pb_ref[...], (8, pb_ref.shape[1]))
        pe_term = (_dot(pe8t, top_ref[...]) + _dot(pe8b, bot_ref[...]))[0:1]
        pre = a + pltpu.roll(b, ncp - 1, 0) + pe_term
        hid = jax.nn.gelu(pre, approximate=True)
        o_ref[...] = _dot(hid.astype(BF16), w2_ref[...]).astype(BF16)

    one(hk_ref, kt_ref, kb_ref, kpt_ref, kpb_ref, k2_ref, ko_ref)
    one(hv_ref, vt_ref, vb_ref, vpt_ref, vpb_ref, v2_ref, vo_ref)


def _compress(kc, vc, wk, wv, b, t):
    ncp = t // CMP_STRIDE
    hw = CMP_STRIDE * NSA_GROUPS * HEAD_DIM
    hk = kc.reshape(b, ncp, hw)
    hv = vc.reshape(b, ncp, hw)
    hspec = pl.BlockSpec((None, ncp, hw), lambda i: (i, 0, 0))

    def full(a):
        return pl.BlockSpec(a.shape, lambda i: (0,) * a.ndim)

    ospec = pl.BlockSpec((None, ncp, LANE), lambda i: (i, 0, 0))
    return pl.pallas_call(
        _compress_kernel,
        grid=(b,),
        in_specs=[hspec, hspec] + [full(a) for a in wk] + [full(a) for a in wv],
        out_specs=[ospec, ospec],
        out_shape=[jax.ShapeDtypeStruct((b, ncp, LANE), BF16)] * 2,
        compiler_params=_cparams(("arbitrary",), 48),
        name="compress",
    )(hk, hv, *wk, *wv)


def _rel_bucket(dist):
    n = jnp.maximum(dist, 0)
    nf = jnp.maximum(n, 1).astype(F32)
    large = REL_EXACT + (jnp.log(nf / REL_EXACT) / math.log(REL_MAX_DIST / REL_EXACT)
                         * (REL_BUCKETS - REL_EXACT)).astype(I32)
    return jnp.where(n < REL_EXACT, n, jnp.minimum(large, REL_BUCKETS - 1))


def _rel_lookup(tab_ref, bucket, col):
    bits = [(bucket & (1 << k)) != 0 for k in range(5)]
    vals = [jnp.where(bits[0], tab_ref[2 * k + 1, col], tab_ref[2 * k, col]) for k in range(REL_BUCKETS // 2)]
    for lvl in range(1, 5):
        vals = [jnp.where(bits[lvl], vals[2 * k + 1], vals[2 * k]) for k in range(len(vals) // 2)]
    return vals[0]


def _strip_kernel(tab_ref, o_ref, *, a, window, head0):
    h = pl.program_id(0)
    shape = o_ref.shape
    r = lax.broadcasted_iota(I32, shape, 0)
    j = lax.broadcasted_iota(I32, shape, 1)
    dist = r + a - j
    valid = dist >= 0
    if window is not None:
        valid = valid & (dist < window)
    val = _rel_lookup(tab_ref, _rel_bucket(dist), h + head0)
    o_ref[...] = jnp.where(valid, val, NEG)


def _bias_strip(rel_bias, n_heads, head0, a, width, window):
    return pl.pallas_call(
        functools.partial(_strip_kernel, a=a, window=window, head0=head0),
        grid=(n_heads,),
        in_specs=[pl.BlockSpec(memory_space=pltpu.SMEM)],
        out_specs=pl.BlockSpec((None, QT, width), lambda h: (h, 0, 0)),
        out_shape=jax.ShapeDtypeStruct((n_heads, QT, width), F32),
        compiler_params=_cparams(("arbitrary",)),
        name="bias_strip",
    )(rel_bias)


def _cmp_bias_kernel(tab_ref, o_ref, *, nc):
    i = pl.program_id(0)
    h = pl.program_id(1)
    shape = o_ref.shape
    r = lax.broadcasted_iota(I32, shape, 0)
    c = lax.broadcasted_iota(I32, shape, 1)
    dist = i * QT + r - (CMP_STRIDE * c + CMP_BLOCK - 1)
    valid = (dist >= 0) & (c < nc)
    val = _rel_lookup(tab_ref, _rel_bucket(dist), h)
    o_ref[...] = jnp.where(valid, val, NEG)


def _cmp_bias(rel_bias, t):
    n_qt = t // QT
    ncp = t // CMP_STRIDE
    nc = (t - CMP_BLOCK) // CMP_STRIDE + 1
    return pl.pallas_call(
        functools.partial(_cmp_bias_kernel, nc=nc),
        grid=(n_qt, NSA_HEADS),
        in_specs=[pl.BlockSpec(memory_space=pltpu.SMEM)],
        out_specs=pl.BlockSpec((None, None, QT, ncp), lambda i, h: (i, h, 0, 0)),
        out_shape=jax.ShapeDtypeStruct((n_qt, NSA_HEADS, QT, ncp), F32),
        compiler_params=_cparams(("arbitrary", "arbitrary")),
        name="cmp_bias",
    )(rel_bias)


def _flash_rows(s_ref, p_ref, m_ref, l_ref, acc_ref, rows, bias):
    sh = s_ref[rows, 0:KT] + bias
    m_old = m_ref[rows, :]
    m_new = jnp.maximum(m_old, jnp.max(sh, axis=-1, keepdims=True))
    alpha = jnp.exp(m_old - m_new)
    p = jnp.exp(sh - jnp.concatenate([m_new] * (KT // LANE), axis=1))
    l_ref[rows, :] = alpha * l_ref[rows, :] + jnp.sum(p, axis=-1, keepdims=True)
    acc_ref[rows, :] = alpha * acc_ref[rows, :]
    m_ref[rows, :] = m_new
    p_ref[rows, 0:KT] = p.astype(BF16)


def _nsa_kernel(qa_ref, gn_ref, kc_ref, vc_ref, kv_ref, sslc_ref, swin_ref, bcmp_ref, mapt_ref, eall_ref, o_ref,
                q_ref, s_ref, p_ref, m_ref, l_ref, acc_ref, madd_ref, st_ref, *, t):
    i = pl.program_id(1)
    q0 = i * QT
    ns = t // SLC_BLOCK
    ncp = t // CMP_STRIDE
    n_top = min(SLC_TOPN, ns)
    hrows = NSA_HPG * QT
    lane_grp = lax.broadcasted_iota(I32, (QT, LANE), 1) >> 6
    shift = SLC_BLOCK.bit_length() - 1

    for g in range(NSA_GROUPS):
        for h in range(NSA_HPG):
            hh = NSA_HPG * g + h
            q_ref[h * QT:(h + 1) * QT, :] = qa_ref[:, hh * LANE:(hh + 1) * LANE]

        s = _dot_nt(q_ref[...], kc_ref[...]) + bcmp_ref[NSA_HPG * g:NSA_HPG * (g + 1)].reshape(hrows, ncp)
        m = jnp.maximum(jnp.max(s, axis=-1, keepdims=True), M_FLOOR)
        e = jnp.exp(s - m)
        p = e / jnp.maximum(jnp.sum(e, axis=-1, keepdims=True), 1e-30)
        o_cmp = _dot(p.astype(BF16), vc_ref[...])
        psum = p[0:QT]
        for h in range(1, NSA_HPG):
            psum = psum + p[h * QT:(h + 1) * QT]
        imp_t = _dot_nt(mapt_ref[...], psum.astype(BF16))

        blk = lax.broadcasted_iota(I32, (ns, QT), 0)
        tq = q0 + lax.broadcasted_iota(I32, (ns, QT), 1)
        cur = tq >> shift
        forced = (blk == 0) | (blk == cur) | (blk == cur - 1)
        avail = (blk << shift) <= tq
        score = jnp.where(avail, imp_t + jnp.where(forced, SLC_FORCE, 0.0), NEG)
        st_ref[0:ns, :] = score

        def rank_body(jp, rank):
            rowb = jnp.broadcast_to(st_ref[pl.ds(jp, 1), :], (ns, QT))
            beats = (rowb > score) | ((rowb == score) & (jp < blk))
            return rank + jnp.where(beats, 1.0, 0.0)

        rank = lax.fori_loop(0, ns, rank_body, jnp.zeros((ns, QT), F32))
        sel_t = jnp.where((rank < n_top) & avail, 1.0, 0.0)
        if ns < LANE:
            sel_t = jnp.concatenate([sel_t, jnp.zeros((LANE - ns, QT), F32)], axis=0)
        sel = sel_t.T.astype(BF16)
        for c in range(t // 512):
            mexp = _dot(sel, eall_ref[:, c * 512:(c + 1) * 512])
            madd_ref[:, c * 512:(c + 1) * 512] = (mexp - 1.0) * (-NEG)

        m_ref[0:hrows, :] = jnp.full((hrows, LANE), M_FLOOR, F32)
        l_ref[0:hrows, :] = jnp.zeros((hrows, LANE), F32)
        acc_ref[0:hrows, :] = jnp.zeros((hrows, LANE), F32)

        def slc_body(kt, carry):
            k0 = pl.multiple_of(kt * KT, KT)
            s_ref[0:hrows, 0:KT] = _dot_nt(q_ref[...], kv_ref[pl.ds(k0, KT), 0:128])
            joff = pl.multiple_of(QT * jnp.maximum(FAR_TILES - (i - 2 * kt), 0), LANE)
            mtile = madd_ref[:, pl.ds(k0, KT)]
            for h in range(NSA_HPG):
                rows = slice(h * QT, (h + 1) * QT)
                bias = sslc_ref[NSA_HPG * g + h, :, pl.ds(joff, KT)] + mtile
                _flash_rows(s_ref, p_ref, m_ref, l_ref, acc_ref, rows, bias)
            acc_ref[0:hrows, :] += _dot(p_ref[0:hrows, 0:KT], kv_ref[pl.ds(k0, KT), 128:256])
            return carry

        lax.fori_loop(0, (i >> 1) + 1, slc_body, 0)
        o_slc = acc_ref[0:hrows, :] / jnp.maximum(l_ref[0:hrows, :], 1e-30)

        ks0 = pl.multiple_of(jnp.maximum(i - WINDOW // QT, 0) * QT, QT)
        woff = pl.multiple_of(jnp.maximum(WINDOW // QT - i, 0) * QT, LANE)
        s_ref[0:hrows, :] = _dot_nt(q_ref[...], kv_ref[pl.ds(ks0, WIN_KEYS), 256:384])
        for h in range(NSA_HPG):
            rows = slice(h * QT, (h + 1) * QT)
            sh = s_ref[rows, :] + swin_ref[NSA_HPG * g + h, :, pl.ds(woff, WIN_KEYS)]
            mw = jnp.max(sh, axis=-1, keepdims=True)
            ew = jnp.exp(sh - mw)
            l_ref[rows, :] = jnp.broadcast_to(jnp.sum(ew, axis=-1, keepdims=True), (QT, LANE))
            p_ref[rows, :] = ew.astype(BF16)
        o_win = _dot(p_ref[0:hrows, :], kv_ref[pl.ds(ks0, WIN_KEYS), 384:512]) / jnp.maximum(l_ref[0:hrows, :], 1e-30)

        for h in range(NSA_HPG):
            hh = NSA_HPG * g + h
            rows = slice(h * QT, (h + 1) * QT)
            o = (gn_ref[:, 3 * hh:3 * hh + 1] * o_cmp[rows] + gn_ref[:, 3 * hh + 1:3 * hh + 2] * o_slc[rows]
                 + gn_ref[:, 3 * hh + 2:3 * hh + 3] * o_win[rows])
            o_ref[:, hh * LANE:(hh + 1) * LANE] = jnp.where(lane_grp == g, o, 0.0).astype(BF16)


def _nsa(qa, gn, kcmp, vcmp, kv4, sslc, swin, bcmp, mapt, eall, b, t):
    n_qt = t // QT
    ncp = t // CMP_STRIDE
    ns = t // SLC_BLOCK
    hrows = NSA_HPG * QT

    def full(a):
        return pl.BlockSpec(a.shape, lambda bi, i: (0,) * a.ndim)

    return pl.pallas_call(
        functools.partial(_nsa_kernel, t=t),
        grid=(b, n_qt),
        in_specs=[
            pl.BlockSpec((None, QT, NSA_HEADS * LANE), lambda bi, i: (bi, i, 0)),
            pl.BlockSpec((None, QT, LANE), lambda bi, i: (bi, i, 0)),
            pl.BlockSpec((None, ncp, LANE), lambda bi, i: (bi, 0, 0)),
            pl.BlockSpec((None, ncp, LANE), lambda bi, i: (bi, 0, 0)),
            pl.BlockSpec((None, t, 512), lambda bi, i: (bi, 0, 0)),
            full(sslc), full(swin),
            pl.BlockSpec((None, NSA_HEADS, QT, ncp), lambda bi, i: (i, 0, 0, 0)),
            full(mapt), full(eall),
        ],
        out_specs=pl.BlockSpec((None, QT, NSA_HEADS * LANE), lambda bi, i: (bi, i, 0)),
        out_shape=jax.ShapeDtypeStruct((b, t, NSA_HEADS * LANE), BF16),
        scratch_shapes=[
            pltpu.VMEM((hrows, LANE), BF16),
            pltpu.VMEM((hrows, WIN_KEYS), F32),
            pltpu.VMEM((hrows, WIN_KEYS), BF16),
            pltpu.VMEM((hrows, LANE), F32),
            pltpu.VMEM((hrows, LANE), F32),
            pltpu.VMEM((hrows, LANE), F32),
            pltpu.VMEM((QT, t), F32),
            pltpu.VMEM((max(ns, 8), QT), F32),
        ],
        compiler_params=_cparams(("arbitrary", "arbitrary"), 56),
        name="nsa",
    )(qa, gn, kcmp, vcmp, kv4, sslc, swin, bcmp, mapt, eall)


IDX_CHUNK = 512


def _dsa_kernel(ql_ref, qi_ref, wi_ref, ki_ref, ckv_ref, strip_ref, wuv_ref, o_ref,
                q_ref, idx_ref, s_ref, p_ref, m_ref, l_ref, acc_ref, *, t, k_sel):
    i = pl.program_id(1)
    q0 = i * QT
    nch = (i >> 2) + 1
    hrows = DSA_HEADS * QT
    sub = IDX_CHUNK // LANE

    for h in range(IDX_HEADS):
        q_ref[h * QT:(h + 1) * QT, :] = qi_ref[:, h * LANE:(h + 1) * LANE]
    tq = q0 + lax.broadcasted_iota(I32, (QT, IDX_CHUNK), 0)
    col = lax.broadcasted_iota(I32, (QT, IDX_CHUNK), 1)

    def idx_body(c, carry):
        c0 = pl.multiple_of(c * IDX_CHUNK, IDX_CHUNK)
        d = jnp.maximum(_dot_nt(q_ref[0:IDX_HEADS * QT, :], ki_ref[pl.ds(c0, IDX_CHUNK), :]), 0.0)
        acc = d[0:QT] * wi_ref[:, 0:1]
        for h in range(1, IDX_HEADS):
            acc = acc + d[h * QT:(h + 1) * QT] * wi_ref[:, h:h + 1]
        idx_ref[:, pl.ds(c0, IDX_CHUNK)] = jnp.where(col + c0 <= tq, acc, NEG)
        return carry

    lax.fori_loop(0, nch, idx_body, 0)

    zeros = jnp.zeros((QT, LANE), F32)

    def scan(fn, init):
        def body(c, carry):
            for s in range(sub):
                off = pl.multiple_of(c * IDX_CHUNK + s * LANE, LANE)
                carry = fn(idx_ref[:, pl.ds(off, LANE)], off, carry)
            return carry
        return lax.fori_loop(0, nch, body, init)

    def search(_):
        def init_fn(x, off, carry):
            lo, hi = carry
            return jnp.minimum(lo, jnp.where(x > 0.5 * NEG, x, BIG)), jnp.maximum(hi, x)

        lo, hi = scan(init_fn, (jnp.full((QT, LANE), BIG, F32), jnp.full((QT, LANE), -BIG, F32)))
        lo = jnp.min(lo, axis=-1, keepdims=True)
        hi = jnp.max(hi, axis=-1, keepdims=True)

        def cond(carry):
            lo, hi = carry
            return jnp.max(jnp.where(lo < hi, 1, 0)) > 0

        def step(carry):
            lo, hi = carry
            mid = lo + (hi - lo) * 0.5
            mid = jnp.where(mid < hi, mid, lo)
            midb = jnp.broadcast_to(mid, (QT, LANE))

            def fn(x, off, c):
                cnt, amin, bmax = c
                gt = x > midb
                return (cnt + jnp.where(gt, 1.0, 0.0), jnp.minimum(amin, jnp.where(gt, x, BIG)),
                        jnp.maximum(bmax, jnp.where(gt, -BIG, x)))

            cnt, amin, bmax = scan(fn, (zeros, jnp.full((QT, LANE), BIG, F32), jnp.full((QT, LANE), -BIG, F32)))
            cnt = jnp.sum(cnt, axis=-1, keepdims=True)
            amin = jnp.min(amin, axis=-1, keepdims=True)
            bmax = jnp.max(bmax, axis=-1, keepdims=True)
            up = cnt >= k_sel
            return jnp.where(up, amin, lo), jnp.where(up, hi, bmax)

        thr, _ = lax.while_loop(cond, step, (lo, hi))
        thrb = jnp.broadcast_to(thr, (QT, LANE))

        def cnt_fn(x, off, c):
            n_gt, n_eq = c
            return n_gt + jnp.where(x > thrb, 1.0, 0.0), n_eq + jnp.where(x == thrb, 1.0, 0.0)

        n_gt, n_eq = scan(cnt_fn, (zeros, zeros))
        need = k_sel - jnp.sum(n_gt, axis=-1, keepdims=True)
        n_eq = jnp.sum(n_eq, axis=-1, keepdims=True)

        def tie_search(_):
            needb = need
            lane = lax.broadcasted_iota(I32, (QT, LANE), 1)

            def tstep(_, carry):
                lo_p, hi_p = carry
                mid_p = (lo_p + hi_p) >> 1
                midpb = jnp.broadcast_to(mid_p, (QT, LANE))

                def fn(x, off, c):
                    return c + jnp.where((x == thrb) & (lane + off <= midpb), 1.0, 0.0)

                c = jnp.sum(scan(fn, zeros), axis=-1, keepdims=True)
                ok = c >= needb
                return jnp.where(ok, lo_p, mid_p), jnp.where(ok, mid_p, hi_p)

            lo_p = jnp.full((QT, 1), -1, I32)
            hi_p = jnp.full((QT, 1), t - 1, I32)
            _, hi_p = lax.fori_loop(0, (t - 1).bit_length() + 1, tstep, (lo_p, hi_p))
            return hi_p

        any_tie = jnp.max(jnp.where(n_eq > need, 1, 0)) > 0
        p_cut = lax.cond(any_tie, tie_search, lambda _: jnp.full((QT, 1), t, I32), 0)
        return thr, p_cut

    thr, p_cut = lax.cond(q0 >= k_sel, search,
                          lambda _: (jnp.full((QT, 1), M_FLOOR, F32), jnp.full((QT, 1), t, I32)), 0)
    thrk = jnp.broadcast_to(thr, (QT, KT))
    pcutk = jnp.broadcast_to(p_cut, (QT, KT))
    colk = lax.broadcasted_iota(I32, (QT, KT), 1)

    for h in range(DSA_HEADS):
        q_ref[h * QT:(h + 1) * QT, :] = ql_ref[:, h * LANE:(h + 1) * LANE]
    m_ref[...] = jnp.full((hrows, LANE), M_FLOOR, F32)
    l_ref[...] = jnp.zeros((hrows, LANE), F32)
    acc_ref[...] = jnp.zeros((hrows, LANE), F32)

    def att_body(kt, carry):
        k0 = pl.multiple_of(kt * KT, KT)
        ctile = ckv_ref[pl.ds(k0, KT), :]
        s_ref[...] = _dot_nt(q_ref[...], ctile)
        joff = pl.multiple_of(QT * jnp.maximum(FAR_TILES - (i - 2 * kt), 0), LANE)
        x = idx_ref[:, pl.ds(k0, KT)]
        keep = (x > thrk) | ((x == thrk) & (colk + k0 <= pcutk))
        selm = jnp.where(keep, 0.0, NEG)
        for h in range(DSA_HEADS):
            rows = slice(h * QT, (h + 1) * QT)
            bias = strip_ref[h, :, pl.ds(joff, KT)] + selm
            _flash_rows(s_ref, p_ref, m_ref, l_ref, acc_ref, rows, bias)
        acc_ref[...] += _dot(p_ref[...], ctile)
        return carry

    lax.fori_loop(0, (i >> 1) + 1, att_body, 0)
    for h in range(DSA_HEADS):
        rows = slice(h * QT, (h + 1) * QT)
        o_lat = (acc_ref[rows, :] / jnp.maximum(l_ref[rows, :], 1e-30)).astype(BF16)
        o_ref[:, h * LANE:(h + 1) * LANE] = _dot(o_lat, wuv_ref[h]).astype(BF16)


def _dsa(qlat, qidx, widx, kidx, ckvn, strip, wuv_pad, b, t):
    n_qt = t // QT
    k_sel = min(IDX_TOPK_MAX, t // 4)
    assert k_sel % QT == 0 and t % IDX_CHUNK == 0
    hrows = DSA_HEADS * QT

    def full(a):
        return pl.BlockSpec(a.shape, lambda bi, i: (0,) * a.ndim)

    return pl.pallas_call(
        functools.partial(_dsa_kernel, t=t, k_sel=k_sel),
        grid=(b, n_qt),
        in_specs=[
            pl.BlockSpec((None, QT, DSA_HEADS * LANE), lambda bi, i: (bi, i, 0)),
            pl.BlockSpec((None, QT, IDX_HEADS * LANE), lambda bi, i: (bi, i, 0)),
            pl.BlockSpec((None, QT, LANE), lambda bi, i: (bi, i, 0)),
            pl.BlockSpec((None, t, LANE), lambda bi, i: (bi, 0, 0)),
            pl.BlockSpec((None, t, LANE), lambda bi, i: (bi, 0, 0)),
            full(strip), full(wuv_pad),
        ],
        out_specs=pl.BlockSpec((None, QT, DSA_HEADS * LANE), lambda bi, i: (bi, i, 0)),
        out_shape=jax.ShapeDtypeStruct((b, t, DSA_HEADS * LANE), BF16),
        scratch_shapes=[
            pltpu.VMEM((hrows, LANE), BF16),
            pltpu.VMEM((QT, t), F32),
            pltpu.VMEM((hrows, KT), F32),
            pltpu.VMEM((hrows, KT), BF16),
            pltpu.VMEM((hrows, LANE), F32),
            pltpu.VMEM((hrows, LANE), F32),
            pltpu.VMEM((hrows, LANE), F32),
        ],
        compiler_params=_cparams(("arbitrary", "arbitrary"), 56),
        name="dsa",
    )(qlat, qidx, widx, kidx, ckvn, strip, wuv_pad)


RT_GRP = 0
RT_EXP = 32


def _layer_norm(y, g, b):
    mu = jnp.mean(y, axis=-1, keepdims=True)
    yc = y - mu
    var = jnp.mean(yc * yc, axis=-1, keepdims=True)
    return yc * lax.rsqrt(var + 1e-5) * g + b


def _post_kernel(x_ref, oa_ref, ob_ref, sga_ref, sgb_ref, wa_ref, wb_ref, wo_ref, g_ref, b_ref, wr_ref, br_ref,
                 h_ref, rt_ref, rw_ref, *, alpha):
    rows = x_ref.shape[0]
    merged = (sga_ref[...].astype(F32) * _dot(oa_ref[...], wa_ref[...])
              + sgb_ref[...].astype(F32) * _dot(ob_ref[...], wb_ref[...]))
    y = alpha * x_ref[...] + _dot(merged.astype(BF16), wo_ref[...])
    h = _layer_norm(y, g_ref[...], b_ref[...])
    h_ref[...] = h

    z = jnp.dot(h, wr_ref[...], preferred_element_type=F32, precision=lax.Precision.HIGHEST) + br_ref[...]
    lane = lax.broadcasted_iota(I32, (rows, LANE), 1)
    is_g = lane < N_EXPERT_GROUPS
    zg = jnp.where(is_g, z, -BIG)
    gmax = jnp.max(zg, axis=-1, keepdims=True)
    g_sel = jnp.min(jnp.where(is_g & (z == gmax), lane, LANE), axis=-1, keepdims=True)
    p_grp = 1.0 / jnp.sum(jnp.where(is_g, jnp.exp(zg - gmax), 0.0), axis=-1, keepdims=True)
    in_grp = (lane >= RT_EXP) & (lane < RT_EXP + N_EXPERTS) & (((lane - RT_EXP) >> 3) == g_sel)
    ze = jnp.where(in_grp, z, -BIG)
    m1 = jnp.max(ze, axis=-1, keepdims=True)
    i1 = jnp.min(jnp.where(in_grp & (z == m1), lane, LANE), axis=-1, keepdims=True)
    ze2 = jnp.where(lane == i1, -BIG, ze)
    m2 = jnp.max(ze2, axis=-1, keepdims=True)
    i2 = jnp.min(jnp.where(in_grp & (lane != i1) & (z == m2), lane, LANE), axis=-1, keepdims=True)
    e21 = jnp.exp(m2 - m1)
    den = 1.0 + e21
    w1 = p_grp * (1.0 / den)
    w2 = p_grp * (e21 / den)
    rt_ref[...] = jnp.where(lane == 0, i1 - RT_EXP, jnp.where(lane == 1, i2 - RT_EXP, 0))
    rw_ref[...] = jnp.where(lane == 0, w1, jnp.where(lane == 1, w2, 0.0))


def _post(x2, oa, ob, sga, sgb, wa, wb, wo, g, b_, wr, br, alpha, rows=256):
    n, d = x2.shape

    def row(w):
        return pl.BlockSpec((rows, w), lambda i: (i, 0))

    def full(a):
        return pl.BlockSpec(a.shape, lambda i: (0,) * a.ndim)

    return pl.pallas_call(
        functools.partial(_post_kernel, alpha=alpha),
        grid=(n // rows,),
        in_specs=[row(d), row(1024), row(1024), row(1024), row(1024), full(wa), full(wb), full(wo), full(g),
                  full(b_), full(wr), full(br)],
        out_specs=[row(d), row(LANE), row(LANE)],
        out_shape=[jax.ShapeDtypeStruct((n, d), F32), jax.ShapeDtypeStruct((n, LANE), I32),
                   jax.ShapeDtypeStruct((n, LANE), F32)],
        compiler_params=_cparams(("arbitrary",), 48),
        name="post",
    )(x2, oa, ob, sga, sgb, wa, wb, wo, g, b_, wr, br)


def _onehots(rt_ref, rows):
    lane = lax.broadcasted_iota(I32, (rows, LANE), 1)
    oh0 = jnp.where(lane == rt_ref[:, 0:1], 1.0, 0.0)
    oh1 = jnp.where(lane == rt_ref[:, 1:2], 1.0, 0.0)
    return oh0, oh1


def _rank_kernel(rt_ref, tri_ref, rank_ref, cnt_ref, carry_ref):
    rows = rt_ref.shape[0]

    @pl.when(pl.program_id(0) == 0)
    def _():
        carry_ref[...] = jnp.zeros_like(carry_ref)

    oh0, oh1 = _onehots(rt_ref, rows)
    both = oh0 + oh1
    before = _dot(tri_ref[...], both.astype(BF16)) + carry_ref[0:1, :]
    r0 = jnp.sum(oh0 * before, axis=-1, keepdims=True)
    r1 = jnp.sum(oh1 * before, axis=-1, keepdims=True)
    lane = lax.broadcasted_iota(I32, (rows, LANE), 1)
    rank_ref[...] = jnp.where(lane == 0, r0, jnp.where(lane == 1, r1, 0.0))
    carry_ref[...] = carry_ref[...] + jnp.sum(both, axis=0, keepdims=True)
    cnt_ref[...] = carry_ref[...]


def _moe_rank(rt, rows=256):
    n = rt.shape[0]
    tri = jnp.asarray(np.tril(np.ones((rows, rows), np.float32), -1), BF16)
    return pl.pallas_call(
        _rank_kernel,
        grid=(n // rows,),
        in_specs=[pl.BlockSpec((rows, LANE), lambda i: (i, 0)), pl.BlockSpec((rows, rows), lambda i: (0, 0))],
        out_specs=[pl.BlockSpec((rows, LANE), lambda i: (i, 0)), pl.BlockSpec((8, LANE), lambda i: (0, 0))],
        out_shape=[jax.ShapeDtypeStruct((n, LANE), F32), jax.ShapeDtypeStruct((8, LANE), F32)],
        scratch_shapes=[pltpu.VMEM((8, LANE), F32)],
        compiler_params=_cparams(("arbitrary",)),
        name="moe_rank",
    )(rt, tri)


def _lane_cumsum(v):
    lane = lax.broadcasted_iota(I32, v.shape, 1)
    s = 1
    while s < LANE:
        v = v + jnp.where(lane >= s, pltpu.roll(v, s, 1), 0.0)
        s *= 2
    return v


def _dest_kernel(rt_ref, rank_ref, cnt_ref, dest_ref, bexp_ref, *, n_blk_pad):
    rows = rt_ref.shape[0]
    lane8 = lax.broadcasted_iota(I32, (8, LANE), 1)
    cnt = jnp.where(lane8 < N_EXPERTS, cnt_ref[...], 0.0)
    padded = jnp.floor((cnt + (EXPERT_BLOCK - 1)) * (1.0 / EXPERT_BLOCK)) * EXPERT_BLOCK
    pend = _lane_cumsum(padded)
    poff = (pend - padded)[0:1, :]
    oh0, oh1 = _onehots(rt_ref, rows)
    d0 = jnp.sum(oh0 * poff, axis=-1, keepdims=True) + rank_ref[:, 0:1]
    d1 = jnp.sum(oh1 * poff, axis=-1, keepdims=True) + rank_ref[:, 1:2]
    lane = lax.broadcasted_iota(I32, (rows, LANE), 1)
    dest_ref[...] = jnp.where(lane == 0, d0, jnp.where(lane == 1, d1, 0.0)).astype(I32)

    lane_b = lax.broadcasted_iota(I32, (n_blk_pad, LANE), 1)
    start = (lax.broadcasted_iota(I32, (n_blk_pad, LANE), 0) * EXPERT_BLOCK).astype(F32)
    hit = jnp.where((lane_b < N_EXPERTS) & (pend[0:1, :] <= start), 1.0, 0.0)
    e_blk = jnp.minimum(jnp.sum(hit, axis=-1, keepdims=True), N_EXPERTS - 1.0)
    used = jnp.max(pend[0:1, :], axis=-1, keepdims=True) * (1.0 / EXPERT_BLOCK)
    bexp_ref[...] = jnp.where(lane_b == 0, e_blk, jnp.where(lane_b == 1, used, 0.0)).astype(I32)


def _moe_dest(rt, rank, cnt, n_blk, rows=256):
    n = rt.shape[0]
    n_blk_pad = -(-n_blk // 8) * 8
    return pl.pallas_call(
        functools.partial(_dest_kernel, n_blk_pad=n_blk_pad),
        grid=(n // rows,),
        in_specs=[pl.BlockSpec((rows, LANE), lambda i: (i, 0)), pl.BlockSpec((rows, LANE), lambda i: (i, 0)),
                  pl.BlockSpec((8, LANE), lambda i: (0, 0))],
        out_specs=[pl.BlockSpec((rows, LANE), lambda i: (i, 0)), pl.BlockSpec((n_blk_pad, LANE), lambda i: (0, 0))],
        out_shape=[jax.ShapeDtypeStruct((n, LANE), I32), jax.ShapeDtypeStruct((n_blk_pad, LANE), I32)],
        compiler_params=_cparams(("arbitrary",)),
        name="moe_dest",
    )(rt, rank, cnt)


MOE_ROWS = 512


def _dispatch_kernel(dest_ref, h_ref, xin_ref, xpad_ref, sem):
    del xin_ref

    def row_copy(r, d):
        return pltpu.make_async_copy(h_ref.at[pl.ds(r, 1)], xpad_ref.at[pl.ds(d, 1)], sem)

    def start(r, carry):
        for j in range(EXPERT_TOPK):
            row_copy(r, dest_ref[EXPERT_TOPK * r + j]).start()
        return carry

    lax.fori_loop(0, MOE_ROWS, start, 0)

    def wait(r, carry):
        for j in range(EXPERT_TOPK):
            row_copy(r, dest_ref[EXPERT_TOPK * r + j]).wait()
        return carry

    lax.fori_loop(0, MOE_ROWS, wait, 0)


def _moe_dispatch(dest_flat, h, n_slots):
    n, d = h.shape
    zeros = jnp.zeros((n_slots, d), h.dtype)
    return pl.pallas_call(
        _dispatch_kernel,
        grid=(n // MOE_ROWS,),
        in_specs=[pl.BlockSpec((EXPERT_TOPK * MOE_ROWS,), lambda i: (i,), memory_space=pltpu.SMEM),
                  pl.BlockSpec((MOE_ROWS, d), lambda i: (i, 0)),
                  pl.BlockSpec(memory_space=pl.ANY)],
        out_specs=pl.BlockSpec(memory_space=pl.ANY),
        out_shape=jax.ShapeDtypeStruct((n_slots, d), h.dtype),
        scratch_shapes=[pltpu.SemaphoreType.DMA(())],
        input_output_aliases={2: 0},
        compiler_params=_cparams(("arbitrary",)),
        name="moe_dispatch",
    )(dest_flat, h, zeros)


def _expert_kernel(bexp_ref, used_ref, x_ref, wg_ref, wu_ref, wd_ref, y_ref):
    blk = pl.program_id(0)

    @pl.when(blk < used_ref[0])
    def _():
        xb = x_ref[...].astype(BF16)
        gate = _dot(xb, wg_ref[...].astype(BF16))
        up = _dot(xb, wu_ref[...].astype(BF16))
        act = (jax.nn.silu(gate) * up).astype(BF16)
        y_ref[...] = _dot(act, wd_ref[...].astype(BF16))

    @pl.when(blk >= used_ref[0])
    def _():
        y_ref[...] = jnp.zeros_like(y_ref)


def _moe_experts(bexp, used, xpad, w_gate, w_up, w_down):
    n_slots, d = xpad.shape
    n_blk = n_slots // EXPERT_BLOCK
    de = w_gate.shape[-1]
    grid_spec = pltpu.PrefetchScalarGridSpec(
        num_scalar_prefetch=2,
        grid=(n_blk,),
        in_specs=[
            pl.BlockSpec((EXPERT_BLOCK, d), lambda i, be, us: (i, 0)),
            pl.BlockSpec((None, d, de), lambda i, be, us: (be[i], 0, 0)),
            pl.BlockSpec((None, d, de), lambda i, be, us: (be[i], 0, 0)),
            pl.BlockSpec((None, de, d), lambda i, be, us: (be[i], 0, 0)),
        ],
        out_specs=pl.BlockSpec((EXPERT_BLOCK, d), lambda i, be, us: (i, 0)),
    )
    return pl.pallas_call(
        _expert_kernel,
        grid_spec=grid_spec,
        out_shape=jax.ShapeDtypeStruct((n_slots, d), F32),
        compiler_params=_cparams(("arbitrary",), 48),
        name="moe_experts",
    )(bexp, used, xpad, w_gate, w_up, w_down)


def _combine_kernel(dest_ref, h_ref, rw_ref, g_ref, b_ref, y_ref, o_ref, buf_ref, sem, *, alpha):
    def row_copy(r, j):
        return pltpu.make_async_copy(y_ref.at[pl.ds(dest_ref[EXPERT_TOPK * r + j], 1)],
                                     buf_ref.at[j, pl.ds(r, 1)], sem)

    def start(r, carry):
        for j in range(EXPERT_TOPK):
            row_copy(r, j).start()
        return carry

    lax.fori_loop(0, MOE_ROWS, start, 0)

    def wait(r, carry):
        for j in range(EXPERT_TOPK):
            row_copy(r, j).wait()
        return carry

    lax.fori_loop(0, MOE_ROWS, wait, 0)
    moe = buf_ref[0] * rw_ref[:, 0:1] + buf_ref[1] * rw_ref[:, 1:2]
    o_ref[...] = _layer_norm(alpha * h_ref[...] + moe, g_ref[...], b_ref[...])


def _moe_combine(dest_flat, h, rw, g, b_, ypad, alpha):
    n, d = h.shape
    return pl.pallas_call(
        functools.partial(_combine_kernel, alpha=alpha),
        grid=(n // MOE_ROWS,),
        in_specs=[pl.BlockSpec((EXPERT_TOPK * MOE_ROWS,), lambda i: (i,), memory_space=pltpu.SMEM),
                  pl.BlockSpec((MOE_ROWS, d), lambda i: (i, 0)),
                  pl.BlockSpec((MOE_ROWS, LANE), lambda i: (i, 0)),
                  pl.BlockSpec((1, d), lambda i: (0, 0)),
                  pl.BlockSpec((1, d), lambda i: (0, 0)),
                  pl.BlockSpec(memory_space=pl.ANY)],
        out_specs=pl.BlockSpec((MOE_ROWS, d), lambda i: (i, 0)),
        out_shape=jax.ShapeDtypeStruct((n, d), F32),
        scratch_shapes=[pltpu.VMEM((EXPERT_TOPK, MOE_ROWS, d), F32), pltpu.SemaphoreType.DMA(())],
        compiler_params=_cparams(("arbitrary",), 48),
        name="moe_combine",
    )(dest_flat, h, rw, g, b_, ypad)


def _cmp_map_t(t):
    nc = (t - CMP_BLOCK) // CMP_STRIDE + 1
    ns = t // SLC_BLOCK
    ncp = t // CMP_STRIDE
    cs = CMP_STRIDE * np.arange(nc)[:, None]
    ss = SLC_BLOCK * np.arange(ns)[None, :]
    ov = np.minimum(cs + CMP_BLOCK, ss + SLC_BLOCK) - np.maximum(cs, ss)
    m = np.clip(ov, 0, None).astype(np.float32) / CMP_STRIDE
    out = np.zeros((ns, ncp), np.float32)
    out[:, :nc] = m.T
    return jnp.asarray(out, BF16)


def _block_expand(t):
    ns = t // SLC_BLOCK
    rows = max(ns, LANE)
    e = np.zeros((rows, t), np.float32)
    e[np.arange(t) // SLC_BLOCK, np.arange(t)] = 1.0
    return jnp.asarray(e, BF16)


def kernel(x, w_in, cmp_pe_k, cmp_pe_v, cmp_w1_k, cmp_w2_k, cmp_w1_v, cmp_w2_v, ckv_norm_g, w_uk, w_uv, rel_bias,
           w_branch_a, w_branch_b, w_out, ln1_g, ln1_b, w_grp, b_grp, w_rtr, b_rtr, w_gate, w_up, w_down, ln2_g,
           ln2_b):
    b, t, d = x.shape
    n = b * t
    depth = w_in.shape[0]
    alpha = (2.0 * depth) ** 0.25
    assert t % 512 == 0 and t >= WIN_KEYS and n % MOE_ROWS == 0

    sslc = _bias_strip(rel_bias, NSA_HEADS, 0, STRIP_A, STRIP_W, None)
    sdsa = _bias_strip(rel_bias, DSA_HEADS, NSA_HEADS, STRIP_A, STRIP_W, None)
    swin = _bias_strip(rel_bias, NSA_HEADS, 0, WIN_A, WIN_W, WINDOW)
    bcmp = _cmp_bias(rel_bias, t)
    mapt = _cmp_map_t(t)
    eall = _block_expand(t)

    n_a = n * EXPERT_TOPK
    n_blk = -(-n_a // EXPERT_BLOCK) + N_EXPERTS
    n_slots = n_blk * EXPERT_BLOCK

    h = x.reshape(n, d)
    for l in range(depth):
        w_pad = _proj_weights(w_in[l])
        wuk = w_uk[l]
        z = jnp.zeros_like(wuk[0])
        wuk_pairs = jnp.stack([
            jnp.concatenate([jnp.concatenate([wuk[2 * k], z], axis=1), jnp.concatenate([z, wuk[2 * k + 1]], axis=1)],
                            axis=0) for k in range(DSA_HEADS // 2)]).astype(BF16)
        wuv_pad = jnp.pad(w_uv[l], ((0, 0), (0, 0), (0, LANE - HEAD_DIM))).astype(BF16)
        wk = _cmp_weights(cmp_w1_k[l], cmp_w2_k[l], cmp_pe_k[l])
        wv = _cmp_weights(cmp_w1_v[l], cmp_w2_v[l], cmp_pe_v[l])
        wa = w_branch_a[l].reshape(NSA_GROUPS, NSA_HPG, HEAD_DIM, d)
        za = jnp.zeros_like(wa[0])
        wa_pad = jnp.concatenate([jnp.concatenate([wa[0], za], axis=1), jnp.concatenate([za, wa[1]], axis=1)],
                                 axis=0).reshape(NSA_HEADS * LANE, d).astype(BF16)
        wb = w_branch_b[l].reshape(DSA_HEADS, HEAD_DIM, d)
        wb_pad = jnp.concatenate([wb, jnp.zeros_like(wb)], axis=1).reshape(DSA_HEADS * LANE, d).astype(BF16)
        wr = jnp.zeros((d, LANE), F32).at[:, RT_GRP:RT_GRP + N_EXPERT_GROUPS].set(w_grp[l])
        wr = wr.at[:, RT_EXP:RT_EXP + N_EXPERTS].set(w_rtr[l])
        br = jnp.zeros((1, LANE), F32).at[0, RT_GRP:RT_GRP + N_EXPERT_GROUPS].set(b_grp[l])
        br = br.at[0, RT_EXP:RT_EXP + N_EXPERTS].set(b_rtr[l])

        (qa, kc, vc, kv4, gn, qlat, ckvn, qidx, kidx, widx, sga, sgb) = _proj(
            h, w_pad, wuk_pairs, ckv_norm_g[l].reshape(1, KV_RANK))
        kcmp, vcmp = _compress(kc, vc, wk, wv, b, t)

        def b3(a):
            return a.reshape(b, t, a.shape[-1])

        oa = _nsa(b3(qa), b3(gn), kcmp, vcmp, b3(kv4), sslc, swin, bcmp, mapt, eall, b, t)
        ob = _dsa(b3(qlat), b3(qidx), b3(widx), b3(kidx), b3(ckvn), sdsa, wuv_pad, b, t)

        h1, rt, rw = _post(h, oa.reshape(n, -1), ob.reshape(n, -1), sga, sgb, wa_pad, wb_pad, w_out[l].astype(BF16),
                           ln1_g[l].reshape(1, d), ln1_b[l].reshape(1, d), wr, br, alpha)

        rank, cnt = _moe_rank(rt)
        dest, bexp = _moe_dest(rt, rank, cnt, n_blk)
        dest_flat = dest[:, :EXPERT_TOPK].reshape(n_a)
        xpad = _moe_dispatch(dest_flat, h1, n_slots)
        ypad = _moe_experts(bexp[:n_blk, 0], bexp[:1, 1], xpad, w_gate[l], w_up[l], w_down[l])
        h = _moe_combine(dest_flat, h1, rw, ln2_g[l].reshape(1, d), ln2_b[l].reshape(1, d), ypad, alpha)
    return h.reshape(b, t, d)
```

```python
import functools
import math

import numpy as np
import jax
import jax.numpy as jnp
from jax import lax
from jax.experimental import pallas as pl
from jax.experimental.pallas import tpu as pltpu

F32 = jnp.float32
BF16 = jnp.bfloat16
I32 = jnp.int32

D_MODEL = 1024
HEAD_DIM = 64
NSA_HEADS = 8
NSA_GROUPS = 2
NSA_HPG = NSA_HEADS // NSA_GROUPS
CMP_BLOCK = 32
CMP_STRIDE = 16
CMP_HIDDEN = 256
SLC_BLOCK = 64
SLC_TOPN = 16
SLC_FORCE = 1e4
WINDOW = 512
DSA_HEADS = 8
KV_RANK = 128
IDX_HEADS = 4
IDX_DIM = 64
IDX_TOPK_MAX = 256
REL_BUCKETS = 32
REL_EXACT = 16
REL_MAX_DIST = 1024
N_EXPERT_GROUPS = 4
EXPERTS_PER_GROUP = 8
N_EXPERTS = N_EXPERT_GROUPS * EXPERTS_PER_GROUP
EXPERT_TOPK = 2
D_EXPERT = 256
EXPERT_BLOCK = 256
NSA_WIDTH = NSA_HEADS * HEAD_DIM
DSA_WIDTH = DSA_HEADS * HEAD_DIM
NEG = -1e30
SPLIT_SIZES = (NSA_WIDTH,) + (NSA_GROUPS * HEAD_DIM,) * 6 + (
    NSA_HEADS * 3, DSA_WIDTH, KV_RANK, IDX_HEADS * IDX_DIM, IDX_DIM, IDX_HEADS, D_MODEL, D_MODEL)
SPLIT_POINTS = tuple(int(v) for v in np.cumsum(SPLIT_SIZES)[:-1])

LANE = 128
QT = 128
KT = 512
TPK = KT // QT
RB = 64
M_FLOOR = -1e29
BIG = 3e38
LOG2E = math.log2(math.e)

FAR_TILES = 11
STRIP_A = FAR_TILES * QT
STRIP_W = STRIP_A + KT
assert REL_EXACT + int(math.log((STRIP_A - KT + 1) / REL_EXACT) / math.log(REL_MAX_DIST / REL_EXACT)
                       * (REL_BUCKETS - REL_EXACT)) >= REL_BUCKETS - 1
WIN_A = WINDOW
WIN_KEYS = WINDOW + QT
WIN_W = WIN_A + WIN_KEYS

O_QA, O_KC, O_VC, O_KV, O_GN, O_QB, O_CKV, O_QIDX, O_KIDX, O_WIDX, O_GA, O_GB, PROJ_W = (
    int(v) for v in np.cumsum([0, 1024, 128, 128, 768, 128, 512, 128, 512, 128, 128, 1024, 1024]))
KV_KS, KV_VS, KV_KW, KV_VW, KV_W = 0, 128, 384, 512, 768
ONES_LANE = HEAD_DIM

_NT = (((1,), (1,)), ((), ()))


def _dot(a, b):
    return jnp.dot(a, b, preferred_element_type=F32)


def _dot_nt(a, b):
    return lax.dot_general(a, b, _NT, preferred_element_type=F32)


def _cparams(sem, vmem_mb=None):
    kw = dict(dimension_semantics=sem)
    if vmem_mb is not None:
        kw["vmem_limit_bytes"] = vmem_mb * 1024 * 1024
    return pltpu.CompilerParams(**kw)


def _proj_weights(w_in):
    (q_a, kc, vc, ks, vs, kw, vw, g_nsa, q_b, ckv, q_idx, k_idx, w_idx, ga, gb) = jnp.split(w_in, SPLIT_POINTS, axis=1)
    d = w_in.shape[0]
    scale = HEAD_DIM ** -0.5 * LOG2E
    z64 = jnp.zeros((d, NSA_HPG, HEAD_DIM), F32)
    qa = (q_a * scale).reshape(d, NSA_GROUPS, NSA_HPG, HEAD_DIM)
    qa_pad = jnp.concatenate([
        jnp.concatenate([qa[:, 0], z64], axis=-1).reshape(d, NSA_HPG * LANE),
        jnp.concatenate([z64, qa[:, 1]], axis=-1).reshape(d, NSA_HPG * LANE)], axis=1)
    qi = q_idx.reshape(d, IDX_HEADS, IDX_DIM)
    qi_pad = jnp.concatenate([qi, jnp.zeros_like(qi)], axis=-1).reshape(d, IDX_HEADS * LANE)

    def pad(a):
        return jnp.pad(a, ((0, 0), (0, LANE - a.shape[1])))

    def per_group(v):
        return jnp.concatenate([pad(v[:, :HEAD_DIM]), pad(v[:, HEAD_DIM:])], axis=1)

    w_idx_s = w_idx * (IDX_HEADS ** -0.5 * IDX_DIM ** -0.5)
    cols = [qa_pad, kc, vc, ks, per_group(vs), kw, per_group(vw), pad(g_nsa), q_b * scale, ckv, qi_pad, pad(k_idx),
            pad(w_idx_s), ga, gb]
    w = jnp.concatenate(cols, axis=1)
    assert w.shape[1] == PROJ_W
    return w.astype(BF16)


def _proj_kernel(x_ref, w_ref, wuk_ref, g_ref, qa_o, kc_o, vc_o, kv_o, gn_o, qlat_o, ckv_o, qidx_o, kidx_o,
                 widx_o, sga_o, sgb_o):
    xb = x_ref[...].astype(BF16)

    def mm(lo, n):
        return _dot(xb, w_ref[:, lo:lo + n])

    qa_o[...] = mm(O_QA, 1024).astype(BF16)
    kc_o[...] = mm(O_KC, 128).astype(BF16)
    vc_o[...] = mm(O_VC, 128).astype(BF16)
    lane = lax.broadcasted_iota(I32, (x_ref.shape[0], KV_W), 1)
    is_one = ((lane & (LANE - 1)) == ONES_LANE) & (((lane >= KV_VS) & (lane < KV_KW)) | (lane >= KV_VW))
    kv_o[...] = jnp.where(is_one, 1.0, mm(O_KV, KV_W)).astype(BF16)
    gn_o[...] = jax.nn.sigmoid(mm(O_GN, 128))
    qb = mm(O_QB, 512).astype(BF16)
    for k in range(DSA_HEADS // 2):
        qlat_o[:, 256 * k:256 * (k + 1)] = _dot(qb[:, 128 * k:128 * (k + 1)], wuk_ref[k]).astype(BF16)
    c = mm(O_CKV, 128)
    ms = jnp.mean(c * c, axis=-1, keepdims=True)
    ckv_o[...] = (c * lax.rsqrt(ms + 1e-6) * g_ref[...]).astype(BF16)
    qidx_o[...] = mm(O_QIDX, 512).astype(BF16)
    kidx_o[...] = mm(O_KIDX, 128).astype(BF16)
    widx_o[...] = mm(O_WIDX, 128)
    sga_o[...] = jax.nn.sigmoid(mm(O_GA, 1024)).astype(BF16)
    sgb_o[...] = jax.nn.sigmoid(mm(O_GB, 1024)).astype(BF16)


def _proj(x2, w_pad, wuk_pairs, ckv_g, rows=512):
    n, d = x2.shape
    widths = [(1024, BF16), (128, BF16), (128, BF16), (KV_W, BF16), (128, F32), (1024, BF16), (128, BF16),
              (512, BF16), (128, BF16), (128, F32), (1024, BF16), (1024, BF16)]
    return pl.pallas_call(
        _proj_kernel,
        grid=(n // rows,),
        in_specs=[
            pl.BlockSpec((rows, d), lambda i: (i, 0)),
            pl.BlockSpec((d, PROJ_W), lambda i: (0, 0)),
            pl.BlockSpec((DSA_HEADS // 2, 128, 256), lambda i: (0, 0, 0)),
            pl.BlockSpec((1, KV_RANK), lambda i: (0, 0)),
        ],
        out_specs=[pl.BlockSpec((rows, w), lambda i: (i, 0)) for w, _ in widths],
        out_shape=[jax.ShapeDtypeStruct((n, w), dt) for w, dt in widths],
        compiler_params=_cparams(("arbitrary",), 56),
        name="proj",
    )(x2, w_pad, wuk_pairs, ckv_g)


def _cmp_weights(w1, w2, pe, out_lanes):
    half = CMP_BLOCK // 2
    w1r = w1.reshape(CMP_BLOCK, HEAD_DIM, CMP_HIDDEN)
    eye = jnp.eye(NSA_GROUPS, dtype=F32)

    def expand(wl):
        return jnp.einsum('ldj,gh->lgdhj', wl, eye).reshape(half * NSA_GROUPS * HEAD_DIM, NSA_GROUPS * CMP_HIDDEN)

    top, bot = expand(w1r[:half]), expand(w1r[half:])

    def pe_rows(p):
        return jnp.broadcast_to(p[:, None, :], (half, NSA_GROUPS, HEAD_DIM)).reshape(1, -1)

    w2p = w2 if out_lanes == HEAD_DIM else jnp.pad(w2, ((0, 0), (0, out_lanes - HEAD_DIM)))
    w2bd = jnp.einsum('jd,gh->gjhd', w2p, eye).reshape(NSA_GROUPS * CMP_HIDDEN, NSA_GROUPS * out_lanes)
    return (top.astype(BF16), bot.astype(BF16), pe_rows(pe[:half]).astype(BF16), pe_rows(pe[half:]).astype(BF16),
            w2bd.astype(BF16))


def _compress_kernel(hk_ref, hv_ref, kt_ref, kb_ref, kpt_ref, kpb_ref, k2_ref, vt_ref, vb_ref, vpt_ref, vpb_ref,
                     v2_ref, ko_ref, vo_ref):
    ncp = hk_ref.shape[0]

    def one(h_ref, top_ref, bot_ref, pt_ref, pb_ref, w2_ref, o_ref):
        h = h_ref[...]
        a = _dot(h, top_ref[...])
        b = _dot(h, bot_ref[...])
        pe8t = jnp.broadcast_to(pt_ref[...], (8, pt_ref.shape[1]))
        pe8b = jnp.broadcast_to(pb_ref[...], (8, pb_ref.shape[1]))
        pe_term = (_dot(pe8t, top_ref[...]) + _dot(pe8b, bot_ref[...]))[0:1]
        pre = a + pltpu.roll(b, ncp - 1, 0) + pe_term
        hid = jax.nn.gelu(pre, approximate=True)
        o_ref[...] = _dot(hid.astype(BF16), w2_ref[...]).astype(BF16)

    one(hk_ref, kt_ref, kb_ref, kpt_ref, kpb_ref, k2_ref, ko_ref)
    one(hv_ref, vt_ref, vb_ref, vpt_ref, vpb_ref, v2_ref, vo_ref)


def _compress(kc, vc, wk, wv, b, t):
    ncp = t // CMP_STRIDE
    hw = CMP_STRIDE * NSA_GROUPS * HEAD_DIM
    hk = kc.reshape(b, ncp, hw)
    hv = vc.reshape(b, ncp, hw)
    hspec = pl.BlockSpec((None, ncp, hw), lambda i: (i, 0, 0))

    def full(a):
        return pl.BlockSpec(a.shape, lambda i: (0,) * a.ndim)

    widths = (wk[-1].shape[1], wv[-1].shape[1])
    return pl.pallas_call(
        _compress_kernel,
        grid=(b,),
        in_specs=[hspec, hspec] + [full(a) for a in wk] + [full(a) for a in wv],
        out_specs=[pl.BlockSpec((None, ncp, w), lambda i: (i, 0, 0)) for w in widths],
        out_shape=[jax.ShapeDtypeStruct((b, ncp, w), BF16) for w in widths],
        compiler_params=_cparams(("arbitrary",), 48),
        name="compress",
    )(hk, hv, *wk, *wv)


def _rel_bucket(dist):
    n = jnp.maximum(dist, 0)
    nf = jnp.maximum(n, 1).astype(F32)
    large = REL_EXACT + (jnp.log(nf / REL_EXACT) / math.log(REL_MAX_DIST / REL_EXACT)
                         * (REL_BUCKETS - REL_EXACT)).astype(I32)
    return jnp.where(n < REL_EXACT, n, jnp.minimum(large, REL_BUCKETS - 1))


def _rel_lookup(tab_ref, bucket, col):
    bits = [(bucket & (1 << k)) != 0 for k in range(5)]
    vals = [jnp.where(bits[0], tab_ref[2 * k + 1, col], tab_ref[2 * k, col]) for k in range(REL_BUCKETS // 2)]
    for lvl in range(1, 5):
        vals = [jnp.where(bits[lvl], vals[2 * k + 1], vals[2 * k]) for k in range(len(vals) // 2)]
    return vals[0]


def _strip_kernel(tab_ref, o_ref, *, a, window, head0, rel_far):
    h = pl.program_id(0)
    shape = o_ref.shape
    r = lax.broadcasted_iota(I32, shape, 0)
    j = lax.broadcasted_iota(I32, shape, 1)
    dist = r + a - j
    valid = dist >= 0
    if window is not None:
        valid = valid & (dist < window)
    val = _rel_lookup(tab_ref, _rel_bucket(dist), h + head0)
    if rel_far:
        val = val - tab_ref[REL_BUCKETS - 1, h + head0]
    o_ref[...] = jnp.where(valid, val * LOG2E, NEG)


def _bias_strip(rel_bias, n_heads, head0, a, width, window, rel_far):
    return pl.pallas_call(
        functools.partial(_strip_kernel, a=a, window=window, head0=head0, rel_far=rel_far),
        grid=(n_heads,),
        in_specs=[pl.BlockSpec(memory_space=pltpu.SMEM)],
        out_specs=pl.BlockSpec((None, QT, width), lambda h: (h, 0, 0)),
        out_shape=jax.ShapeDtypeStruct((n_heads, QT, width), F32),
        compiler_params=_cparams(("arbitrary",)),
        name="bias_strip",
    )(rel_bias)


def _cmp_bias_kernel(tab_ref, o_ref, *, nc):
    i = pl.program_id(0)
    h = pl.program_id(1)
    shape = o_ref.shape
    r = lax.broadcasted_iota(I32, shape, 0)
    c = lax.broadcasted_iota(I32, shape, 1)
    dist = i * QT + r - (CMP_STRIDE * c + CMP_BLOCK - 1)
    valid = (dist >= 0) & (c < nc)
    val = _rel_lookup(tab_ref, _rel_bucket(dist), h)
    o_ref[...] = jnp.where(valid, val * LOG2E, NEG)


def _cmp_bias(rel_bias, t):
    n_qt = t // QT
    ncp = t // CMP_STRIDE
    nc = (t - CMP_BLOCK) // CMP_STRIDE + 1
    return pl.pallas_call(
        functools.partial(_cmp_bias_kernel, nc=nc),
        grid=(n_qt, NSA_HEADS),
        in_specs=[pl.BlockSpec(memory_space=pltpu.SMEM)],
        out_specs=pl.BlockSpec((None, None, QT, ncp), lambda i, h: (i, h, 0, 0)),
        out_shape=jax.ShapeDtypeStruct((n_qt, NSA_HEADS, QT, ncp), F32),
        compiler_params=_cparams(("arbitrary", "arbitrary")),
        name="cmp_bias",
    )(rel_bias)


def _lane_groups(op, x):
    parts = [x[:, c * LANE:(c + 1) * LANE] for c in range(x.shape[1] // LANE)]
    while len(parts) > 1:
        parts = [op(parts[j], parts[j + 1]) for j in range(0, len(parts), 2)]
    return parts[0]


def _far_tiles(i):
    return jnp.maximum((i - (FAR_TILES - TPK)) >> (TPK.bit_length() - 1), 0)


def _strip_offset(i, kt):
    return pl.multiple_of(QT * jnp.maximum(FAR_TILES - (i - TPK * kt), 0), LANE)


def _nsa_kernel(qa_ref, gn_ref, kc_ref, vc_ref, kv_ref, sslc_ref, swin_ref, bcmp_ref, mapt_ref, eall_ref, o_ref,
                q_ref, s_ref, z_ref, p_ref, m_ref, acc_ref, ocmp_ref, sel_ref, st_ref, *, t):
    i = pl.program_id(1)
    q0 = i * QT
    ns = t // SLC_BLOCK
    ncp = t // CMP_STRIDE
    n_top = min(SLC_TOPN, ns)
    grows = NSA_HPG * QT
    arows = NSA_HEADS * QT
    shift = SLC_BLOCK.bit_length() - 1
    halves = QT // RB

    for h in range(NSA_HEADS):
        q_ref[h * QT:(h + 1) * QT, :] = qa_ref[:, h * LANE:(h + 1) * LANE]

    s_ref[:, 0:ncp] = _dot_nt(q_ref[...], kc_ref[...])
    imp = []
    for g in range(NSA_GROUPS):
        psum = None
        for h in range(NSA_HPG * g, NSA_HPG * (g + 1)):
            rows = slice(h * QT, (h + 1) * QT)
            z = s_ref[rows, 0:ncp] + bcmp_ref[h]
            m = jnp.maximum(jnp.max(z, axis=-1, keepdims=True), M_FLOOR)
            e = jnp.exp2(z - m)
            p = e / jnp.maximum(jnp.sum(e, axis=-1, keepdims=True), 1e-30)
            p_ref[rows, 0:ncp] = p.astype(BF16)
            psum = p if psum is None else psum + p
        grp = slice(g * grows, (g + 1) * grows)
        ocmp_ref[grp, :] = _dot(p_ref[grp, 0:ncp], vc_ref[:, g * LANE:(g + 1) * LANE])
        imp.append(_dot_nt(mapt_ref[...], psum.astype(BF16)))
    imp_t = jnp.concatenate(imp, axis=1)

    blk = lax.broadcasted_iota(I32, (ns, NSA_GROUPS * QT), 0)
    tq = q0 + (lax.broadcasted_iota(I32, (ns, NSA_GROUPS * QT), 1) & (QT - 1))
    cur = tq >> shift
    forced = (blk == 0) | (blk == cur) | (blk == cur - 1)
    avail = (blk << shift) <= tq
    score = jnp.where(avail, imp_t + jnp.where(forced, SLC_FORCE, 0.0), NEG)
    st_ref[0:ns, :] = score

    def rank_body(jp, rank):
        rowb = jnp.broadcast_to(st_ref[pl.ds(jp, 1), :], score.shape)
        beats = (rowb > score) | ((rowb == score) & (jp < blk))
        return rank + jnp.where(beats, 1.0, 0.0)

    rank = lax.fori_loop(0, ns, rank_body, jnp.zeros(score.shape, F32))
    sel_t = jnp.where((rank < n_top) & avail, 1.0, 0.0)
    if ns < LANE:
        sel_t = jnp.concatenate([sel_t, jnp.zeros((LANE - ns, NSA_GROUPS * QT), F32)], axis=0)
    for g in range(NSA_GROUPS):
        sel_ref[g] = sel_t[:, g * QT:(g + 1) * QT].T.astype(BF16)

    m_ref[...] = jnp.full((arows, LANE), M_FLOOR, F32)
    acc_ref[...] = jnp.zeros((arows, LANE), F32)

    def group_logits(g, k0, kt, near):
        madd = (_dot(sel_ref[g], eall_ref[:, pl.ds(k0, KT)]) - 1.0) * (-NEG)
        for h in range(NSA_HPG * g, NSA_HPG * (g + 1)):
            for half in range(halves):
                qrows = slice(half * RB, (half + 1) * RB)
                rows = slice(h * QT + half * RB, h * QT + (half + 1) * RB)
                z = z_ref[rows, pl.ds(k0, KT)] + madd[qrows]
                if near:
                    z = z + sslc_ref[h, qrows, pl.ds(_strip_offset(i, kt), KT)]
                yield rows, z

    def max_step(kt, near):
        k0 = pl.multiple_of(kt * KT, KT)
        z_ref[:, pl.ds(k0, KT)] = _dot_nt(q_ref[...], kv_ref[pl.ds(k0, KT), KV_KS:KV_KS + LANE])
        for g in range(NSA_GROUPS):
            for rows, z in group_logits(g, k0, kt, near):
                m_ref[rows, :] = jnp.maximum(m_ref[rows, :], _lane_groups(jnp.maximum, z))

    n_far = _far_tiles(i)
    n_kt = (i >> (TPK.bit_length() - 1)) + 1
    lax.fori_loop(0, n_far, lambda kt, c: (max_step(kt, False), c)[1], 0)
    lax.fori_loop(n_far, n_kt, lambda kt, c: (max_step(kt, True), c)[1], 0)
    m_ref[...] = jnp.broadcast_to(jnp.max(m_ref[...], axis=-1, keepdims=True), (arows, LANE))

    def sum_step(kt, near):
        k0 = pl.multiple_of(kt * KT, KT)
        for g in range(NSA_GROUPS):
            for rows, z in group_logits(g, k0, kt, near):
                m = m_ref[rows, :]
                p_ref[rows, 0:KT] = jnp.exp2(z - jnp.concatenate([m] * (KT // LANE), axis=1)).astype(BF16)
            grp = slice(g * grows, (g + 1) * grows)
            acc_ref[grp, :] += _dot(p_ref[grp, 0:KT], kv_ref[pl.ds(k0, KT), KV_VS + g * LANE:KV_VS + (g + 1) * LANE])

    lax.fori_loop(0, n_far, lambda kt, c: (sum_step(kt, False), c)[1], 0)
    lax.fori_loop(n_far, n_kt, lambda kt, c: (sum_step(kt, True), c)[1], 0)

    ks0 = pl.multiple_of(jnp.maximum(i - WINDOW // QT, 0) * QT, QT)
    woff = pl.multiple_of(jnp.maximum(WINDOW // QT - i, 0) * QT, LANE)
    s_ref[...] = _dot_nt(q_ref[...], kv_ref[pl.ds(ks0, WIN_KEYS), KV_KW:KV_KW + LANE])
    for h in range(NSA_HEADS):
        for half in range(halves):
            qrows = slice(half * RB, (half + 1) * RB)
            rows = slice(h * QT + half * RB, h * QT + (half + 1) * RB)
            z = s_ref[rows, :] + swin_ref[h, qrows, pl.ds(woff, WIN_KEYS)]
            p_ref[rows, :] = jnp.exp2(z - jnp.max(z, axis=-1, keepdims=True)).astype(BF16)

    lane_ok = lax.broadcasted_iota(I32, (QT, LANE), 1) < HEAD_DIM
    for g in range(NSA_GROUPS):
        grp = slice(g * grows, (g + 1) * grows)
        o_win = _dot(p_ref[grp, :], kv_ref[pl.ds(ks0, WIN_KEYS), KV_VW + g * LANE:KV_VW + (g + 1) * LANE])
        for hp in range(NSA_HPG):
            h = NSA_HPG * g + hp
            rows = slice(h * QT, (h + 1) * QT)
            wrows = slice(hp * QT, (hp + 1) * QT)
            slc = acc_ref[rows, :]
            o_slc = slc / jnp.maximum(slc[:, ONES_LANE:ONES_LANE + 1], 1e-30)
            win = o_win[wrows]
            o_w = win / jnp.maximum(win[:, ONES_LANE:ONES_LANE + 1], 1e-30)
            o = (gn_ref[:, 3 * h:3 * h + 1] * ocmp_ref[rows, :] + gn_ref[:, 3 * h + 1:3 * h + 2] * o_slc
                 + gn_ref[:, 3 * h + 2:3 * h + 3] * o_w)
            o_ref[:, h * LANE:(h + 1) * LANE] = jnp.where(lane_ok, o, 0.0).astype(BF16)


def _nsa(qa, gn, kcmp, vcmp, kv, sslc, swin, bcmp, mapt, eall, b, t):
    n_qt = t // QT
    ncp = t // CMP_STRIDE
    ns = t // SLC_BLOCK
    arows = NSA_HEADS * QT
    assert ncp <= WIN_KEYS

    def full(a):
        return pl.BlockSpec(a.shape, lambda bi, i: (0,) * a.ndim)

    def once(a):
        return pl.BlockSpec(a.shape, lambda bi, i: (0,) * a.ndim, pipeline_mode=pl.Buffered(1))

    return pl.pallas_call(
        functools.partial(_nsa_kernel, t=t),
        grid=(b, n_qt),
        in_specs=[
            pl.BlockSpec((None, QT, NSA_HEADS * LANE), lambda bi, i: (bi, i, 0)),
            pl.BlockSpec((None, QT, LANE), lambda bi, i: (bi, i, 0)),
            pl.BlockSpec((None, ncp, LANE), lambda bi, i: (bi, 0, 0)),
            pl.BlockSpec((None, ncp, NSA_GROUPS * LANE), lambda bi, i: (bi, 0, 0)),
            pl.BlockSpec((None, t, KV_W), lambda bi, i: (bi, 0, 0), pipeline_mode=pl.Buffered(1)),
            once(sslc), once(swin),
            pl.BlockSpec((None, NSA_HEADS, QT, ncp), lambda bi, i: (i, 0, 0, 0)),
            full(mapt), once(eall),
        ],
        out_specs=pl.BlockSpec((None, QT, NSA_HEADS * LANE), lambda bi, i: (bi, i, 0)),
        out_shape=jax.ShapeDtypeStruct((b, t, NSA_HEADS * LANE), BF16),
        scratch_shapes=[
            pltpu.VMEM((arows, LANE), BF16),
            pltpu.VMEM((arows, WIN_KEYS), F32),
            pltpu.VMEM((arows, t), F32),
            pltpu.VMEM((arows, WIN_KEYS), BF16),
            pltpu.VMEM((arows, LANE), F32),
            pltpu.VMEM((arows, LANE), F32),
            pltpu.VMEM((arows, LANE), F32),
            pltpu.VMEM((NSA_GROUPS, QT, LANE), BF16),
            pltpu.VMEM((max(ns, 8), NSA_GROUPS * QT), F32),
        ],
        compiler_params=_cparams(("arbitrary", "arbitrary"), 56),
        name="nsa",
    )(qa, gn, kcmp, vcmp, kv, sslc, swin, bcmp, mapt, eall)


IDX_CHUNK = 512


def _dsa_kernel(ql_ref, qi_ref, wi_ref, ki_ref, ckv_ref, strip_ref, wuv_ref, o_ref,
                q_ref, idx_ref, tmp_ref, s_ref, p_ref, m_ref, l_ref, acc_ref, *, t, k_sel):
    i = pl.program_id(1)
    q0 = i * QT
    nch = (i >> 2) + 1
    hrows = DSA_HEADS * QT
    sub = IDX_CHUNK // LANE
    halves = QT // RB

    for h in range(IDX_HEADS):
        q_ref[h * QT:(h + 1) * QT, :] = qi_ref[:, h * LANE:(h + 1) * LANE]
    tq = q0 + lax.broadcasted_iota(I32, (QT, IDX_CHUNK), 0)
    col = lax.broadcasted_iota(I32, (QT, IDX_CHUNK), 1)

    def idx_body(c, carry):
        c0 = pl.multiple_of(c * IDX_CHUNK, IDX_CHUNK)
        d = jnp.maximum(_dot_nt(q_ref[0:IDX_HEADS * QT, :], ki_ref[pl.ds(c0, IDX_CHUNK), :]), 0.0)
        acc = d[0:QT] * wi_ref[:, 0:1]
        for h in range(1, IDX_HEADS):
            acc = acc + d[h * QT:(h + 1) * QT] * wi_ref[:, h:h + 1]
        idx_ref[:, pl.ds(c0, IDX_CHUNK)] = jnp.where(col + c0 <= tq, acc, NEG)
        return carry

    lax.fori_loop(0, nch, idx_body, 0)

    zeros = jnp.zeros((QT, LANE), F32)

    def scan(fn, init):
        def body(c, carry):
            for s in range(sub):
                off = pl.multiple_of(c * IDX_CHUNK + s * LANE, LANE)
                carry = fn(idx_ref[:, pl.ds(off, LANE)], off, carry)
            return carry
        return lax.fori_loop(0, nch, body, init)

    def search(_):
        def init_fn(x, off, carry):
            lo, hi = carry
            return jnp.minimum(lo, jnp.where(x > 0.5 * NEG, x, BIG)), jnp.maximum(hi, x)

        lo, hi = scan(init_fn, (jnp.full((QT, LANE), BIG, F32), jnp.full((QT, LANE), -BIG, F32)))
        lo = jnp.min(lo, axis=-1, keepdims=True)
        hi = jnp.max(hi, axis=-1, keepdims=True)
        c_lo = (q0 + 1 + lax.broadcasted_iota(I32, (QT, 1), 0)).astype(F32)
        c_hi = jnp.zeros((QT, 1), F32)

        def cond(carry):
            return jnp.max(jnp.where(carry[0] < carry[1], 1, 0)) > 0

        def step(carry):
            lo, hi, c_lo, c_hi, it = carry
            frac = jnp.where((it & 1) == 0, (c_lo - (k_sel - 0.5)) / (c_lo - c_hi), 0.5)
            mid = lo + (hi - lo) * frac
            mid = jnp.where(mid < hi, mid, lo)
            midb = jnp.broadcast_to(mid, (QT, LANE))

            def fn(x, off, c):
                cnt, amin, bmax = c
                gt = x > midb
                return (cnt + jnp.where(gt, 1.0, 0.0), jnp.minimum(amin, jnp.where(gt, x, BIG)),
                        jnp.maximum(bmax, jnp.where(gt, -BIG, x)))

            cnt, amin, bmax = scan(fn, (zeros, jnp.full((QT, LANE), BIG, F32), jnp.full((QT, LANE), -BIG, F32)))
            cnt = jnp.sum(cnt, axis=-1, keepdims=True)
            amin = jnp.min(amin, axis=-1, keepdims=True)
            bmax = jnp.max(bmax, axis=-1, keepdims=True)
            up = cnt >= k_sel
            return (jnp.where(up, amin, lo), jnp.where(up, hi, bmax), jnp.where(up, cnt, c_lo),
                    jnp.where(up, c_hi, cnt), it + 1)

        thr, _, c_ge, c_gt, _ = lax.while_loop(cond, step, (lo, hi, c_lo, c_hi, jnp.int32(0)))
        thrb = jnp.broadcast_to(thr, (QT, LANE))
        need = k_sel - c_gt

        def tie_search(_):
            lane = lax.broadcasted_iota(I32, (QT, LANE), 1)

            def tstep(_, carry):
                lo_p, hi_p = carry
                mid_p = (lo_p + hi_p) >> 1
                midpb = jnp.broadcast_to(mid_p, (QT, LANE))

                def fn(x, off, c):
                    return c + jnp.where((x == thrb) & (lane + off <= midpb), 1.0, 0.0)

                c = jnp.sum(scan(fn, zeros), axis=-1, keepdims=True)
                ok = c >= need
                return jnp.where(ok, lo_p, mid_p), jnp.where(ok, mid_p, hi_p)

            lo_p = jnp.full((QT, 1), -1, I32)
            hi_p = jnp.full((QT, 1), t - 1, I32)
            _, hi_p = lax.fori_loop(0, (t - 1).bit_length() + 1, tstep, (lo_p, hi_p))
            return hi_p

        any_tie = jnp.max(jnp.where(c_ge > k_sel, 1, 0)) > 0
        p_cut = lax.cond(any_tie, tie_search, lambda _: jnp.full((QT, 1), t, I32), 0)
        return thr, p_cut

    thr, p_cut = lax.cond(q0 >= k_sel, search,
                          lambda _: (jnp.full((QT, 1), M_FLOOR, F32), jnp.full((QT, 1), t, I32)), 0)
    thrk = jnp.broadcast_to(thr, (QT, KT))
    pcutk = jnp.broadcast_to(p_cut, (QT, KT))
    colk = lax.broadcasted_iota(I32, (QT, KT), 1)

    for h in range(DSA_HEADS):
        q_ref[h * QT:(h + 1) * QT, :] = ql_ref[:, h * LANE:(h + 1) * LANE]
    m_ref[...] = jnp.full((hrows, LANE), M_FLOOR, F32)
    l_ref[...] = jnp.zeros((hrows, LANE), F32)
    acc_ref[...] = jnp.zeros((hrows, LANE), F32)
    blocks = [(h, half) for h in range(DSA_HEADS) for half in range(halves)]

    n_far = _far_tiles(i)
    n_kt = (i >> (TPK.bit_length() - 1)) + 1

    def max_step(kt, near):
        k0 = pl.multiple_of(kt * KT, KT)
        s = _dot_nt(q_ref[...], ckv_ref[pl.ds(k0, KT), :])
        x = idx_ref[:, pl.ds(k0, KT)]
        keep = (x > thrk) | ((x == thrk) & (colk + k0 <= pcutk))
        selm = jnp.where(keep, 0.0, NEG)
        for h, half in blocks:
            qrows = slice(half * RB, (half + 1) * RB)
            rows = slice(h * QT + half * RB, h * QT + (half + 1) * RB)
            z = s[rows] + selm[qrows]
            if near:
                z = z + strip_ref[h, qrows, pl.ds(_strip_offset(i, kt), KT)]
            s_ref[rows, pl.ds(k0, KT)] = z
            m_ref[rows, :] = jnp.maximum(m_ref[rows, :], _lane_groups(jnp.maximum, z))

    lax.fori_loop(0, n_far, lambda kt, c: (max_step(kt, False), c)[1], 0)
    lax.fori_loop(n_far, n_kt, lambda kt, c: (max_step(kt, True), c)[1], 0)
    m_ref[...] = jnp.broadcast_to(jnp.max(m_ref[...], axis=-1, keepdims=True), (hrows, LANE))

    def sum_step(kt, c):
        k0 = pl.multiple_of(kt * KT, KT)
        ps = []
        for h, half in blocks:
            rows = slice(h * QT + half * RB, h * QT + (half + 1) * RB)
            m = m_ref[rows, :]
            p = jnp.exp2(s_ref[rows, pl.ds(k0, KT)] - jnp.concatenate([m] * (KT // LANE), axis=1))
            l_ref[rows, :] += _lane_groups(jnp.add, p)
            ps.append(p.astype(BF16))
        acc_ref[...] += _dot(jnp.concatenate(ps, axis=0), ckv_ref[pl.ds(k0, KT), :])
        return c

    lax.fori_loop(0, n_kt, sum_step, 0)
    for h in range(DSA_HEADS):
        rows = slice(h * QT, (h + 1) * QT)
        l = jnp.sum(l_ref[rows, :], axis=-1, keepdims=True)
        o_lat = (acc_ref[rows, :] / jnp.maximum(l, 1e-30)).astype(BF16)
        o_ref[:, h * LANE:(h + 1) * LANE] = _dot(o_lat, wuv_ref[h]).astype(BF16)


def _dsa(qlat, qidx, widx, kidx, ckvn, strip, wuv_pad, b, t):
    n_qt = t // QT
    k_sel = min(IDX_TOPK_MAX, t // 4)
    assert k_sel % QT == 0 and t % IDX_CHUNK == 0
    hrows = DSA_HEADS * QT

    def full(a):
        return pl.BlockSpec(a.shape, lambda bi, i: (0,) * a.ndim)

    return pl.pallas_call(
        functools.partial(_dsa_kernel, t=t, k_sel=k_sel),
        grid=(b, n_qt),
        in_specs=[
            pl.BlockSpec((None, QT, DSA_HEADS * LANE), lambda bi, i: (bi, i, 0)),
            pl.BlockSpec((None, QT, IDX_HEADS * LANE), lambda bi, i: (bi, i, 0)),
            pl.BlockSpec((None, QT, LANE), lambda bi, i: (bi, i, 0)),
            pl.BlockSpec((None, t, LANE), lambda bi, i: (bi, 0, 0)),
            pl.BlockSpec((None, t, LANE), lambda bi, i: (bi, 0, 0)),
            pl.BlockSpec(strip.shape, lambda bi, i: (0, 0, 0), pipeline_mode=pl.Buffered(1)),
            full(wuv_pad),
        ],
        out_specs=pl.BlockSpec((None, QT, DSA_HEADS * LANE), lambda bi, i: (bi, i, 0)),
        out_shape=jax.ShapeDtypeStruct((b, t, DSA_HEADS * LANE), BF16),
        scratch_shapes=[
            pltpu.VMEM((hrows, LANE), BF16),
            pltpu.VMEM((QT, t), F32),
            pltpu.VMEM((hrows, KT), F32),
            pltpu.VMEM((hrows, t), F32),
            pltpu.VMEM((hrows, KT), BF16),
            pltpu.VMEM((hrows, LANE), F32),
            pltpu.VMEM((hrows, LANE), F32),
            pltpu.VMEM((hrows, LANE), F32),
        ],
        compiler_params=_cparams(("arbitrary", "arbitrary"), 56),
        name="dsa",
    )(qlat, qidx, widx, kidx, ckvn, strip, wuv_pad)


RT_GRP = 0
RT_EXP = 32


def _layer_norm(y, g, b):
    mu = jnp.mean(y, axis=-1, keepdims=True)
    yc = y - mu
    var = jnp.mean(yc * yc, axis=-1, keepdims=True)
    return yc * lax.rsqrt(var + 1e-5) * g + b


def _post_kernel(x_ref, oa_ref, ob_ref, sga_ref, sgb_ref, wa_ref, wb_ref, wo_ref, g_ref, b_ref, wr_ref, br_ref,
                 h_ref, rt_ref, rw_ref, *, alpha):
    rows = x_ref.shape[0]
    merged = (sga_ref[...].astype(F32) * _dot(oa_ref[...], wa_ref[...])
              + sgb_ref[...].astype(F32) * _dot(ob_ref[...], wb_ref[...]))
    y = alpha * x_ref[...] + _dot(merged.astype(BF16), wo_ref[...])
    h = _layer_norm(y, g_ref[...], b_ref[...])
    h_ref[...] = h

    z = jnp.dot(h, wr_ref[...], preferred_element_type=F32, precision=lax.Precision.HIGHEST) + br_ref[...]
    lane = lax.broadcasted_iota(I32, (rows, LANE), 1)
    is_g = lane < N_EXPERT_GROUPS
    zg = jnp.where(is_g, z, -BIG)
    gmax = jnp.max(zg, axis=-1, keepdims=True)
    g_sel = jnp.min(jnp.where(is_g & (z == gmax), lane, LANE), axis=-1, keepdims=True)
    p_grp = 1.0 / jnp.sum(jnp.where(is_g, jnp.exp(zg - gmax), 0.0), axis=-1, keepdims=True)
    in_grp = (lane >= RT_EXP) & (lane < RT_EXP + N_EXPERTS) & (((lane - RT_EXP) >> 3) == g_sel)
    ze = jnp.where(in_grp, z, -BIG)
    m1 = jnp.max(ze, axis=-1, keepdims=True)
    i1 = jnp.min(jnp.where(in_grp & (z == m1), lane, LANE), axis=-1, keepdims=True)
    ze2 = jnp.where(lane == i1, -BIG, ze)
    m2 = jnp.max(ze2, axis=-1, keepdims=True)
    i2 = jnp.min(jnp.where(in_grp & (lane != i1) & (z == m2), lane, LANE), axis=-1, keepdims=True)
    e21 = jnp.exp(m2 - m1)
    den = 1.0 + e21
    w1 = p_grp * (1.0 / den)
    w2 = p_grp * (e21 / den)
    rt_ref[...] = jnp.where(lane == 0, i1 - RT_EXP, jnp.where(lane == 1, i2 - RT_EXP, 0))
    rw_ref[...] = jnp.where(lane == 0, w1, jnp.where(lane == 1, w2, 0.0))


def _post(x2, oa, ob, sga, sgb, wa, wb, wo, g, b_, wr, br, alpha, rows=256):
    n, d = x2.shape

    def row(w):
        return pl.BlockSpec((rows, w), lambda i: (i, 0))

    def full(a):
        return pl.BlockSpec(a.shape, lambda i: (0,) * a.ndim)

    return pl.pallas_call(
        functools.partial(_post_kernel, alpha=alpha),
        grid=(n // rows,),
        in_specs=[row(d), row(1024), row(1024), row(1024), row(1024), full(wa), full(wb), full(wo), full(g),
                  full(b_), full(wr), full(br)],
        out_specs=[row(d), row(LANE), row(LANE)],
        out_shape=[jax.ShapeDtypeStruct((n, d), F32), jax.ShapeDtypeStruct((n, LANE), I32),
                   jax.ShapeDtypeStruct((n, LANE), F32)],
        compiler_params=_cparams(("arbitrary",), 48),
        name="post",
    )(x2, oa, ob, sga, sgb, wa, wb, wo, g, b_, wr, br)


def _onehots(rt_ref, rows):
    lane = lax.broadcasted_iota(I32, (rows, LANE), 1)
    oh0 = jnp.where(lane == rt_ref[:, 0:1], 1.0, 0.0)
    oh1 = jnp.where(lane == rt_ref[:, 1:2], 1.0, 0.0)
    return oh0, oh1


def _rank_kernel(rt_ref, tri_ref, rank_ref, cnt_ref, carry_ref):
    rows = rt_ref.shape[0]

    @pl.when(pl.program_id(0) == 0)
    def _():
        carry_ref[...] = jnp.zeros_like(carry_ref)

    oh0, oh1 = _onehots(rt_ref, rows)
    both = oh0 + oh1
    before = _dot(tri_ref[...], both.astype(BF16)) + carry_ref[0:1, :]
    r0 = jnp.sum(oh0 * before, axis=-1, keepdims=True)
    r1 = jnp.sum(oh1 * before, axis=-1, keepdims=True)
    lane = lax.broadcasted_iota(I32, (rows, LANE), 1)
    rank_ref[...] = jnp.where(lane == 0, r0, jnp.where(lane == 1, r1, 0.0))
    carry_ref[...] = carry_ref[...] + jnp.sum(both, axis=0, keepdims=True)
    cnt_ref[...] = carry_ref[...]


def _moe_rank(rt, rows=256):
    n = rt.shape[0]
    tri = jnp.asarray(np.tril(np.ones((rows, rows), np.float32), -1), BF16)
    return pl.pallas_call(
        _rank_kernel,
        grid=(n // rows,),
        in_specs=[pl.BlockSpec((rows, LANE), lambda i: (i, 0)), pl.BlockSpec((rows, rows), lambda i: (0, 0))],
        out_specs=[pl.BlockSpec((rows, LANE), lambda i: (i, 0)), pl.BlockSpec((8, LANE), lambda i: (0, 0))],
        out_shape=[jax.ShapeDtypeStruct((n, LANE), F32), jax.ShapeDtypeStruct((8, LANE), F32)],
        scratch_shapes=[pltpu.VMEM((8, LANE), F32)],
        compiler_params=_cparams(("arbitrary",)),
        name="moe_rank",
    )(rt, tri)


def _lane_cumsum(v):
    lane = lax.broadcasted_iota(I32, v.shape, 1)
    s = 1
    while s < LANE:
        v = v + jnp.where(lane >= s, pltpu.roll(v, s, 1), 0.0)
        s *= 2
    return v


def _dest_kernel(rt_ref, rank_ref, cnt_ref, dest_ref, bexp_ref, *, n_blk_pad):
    rows = rt_ref.shape[0]
    lane8 = lax.broadcasted_iota(I32, (8, LANE), 1)
    cnt = jnp.where(lane8 < N_EXPERTS, cnt_ref[...], 0.0)
    padded = jnp.floor((cnt + (EXPERT_BLOCK - 1)) * (1.0 / EXPERT_BLOCK)) * EXPERT_BLOCK
    pend = _lane_cumsum(padded)
    poff = (pend - padded)[0:1, :]
    oh0, oh1 = _onehots(rt_ref, rows)
    d0 = jnp.sum(oh0 * poff, axis=-1, keepdims=True) + rank_ref[:, 0:1]
    d1 = jnp.sum(oh1 * poff, axis=-1, keepdims=True) + rank_ref[:, 1:2]
    lane = lax.broadcasted_iota(I32, (rows, LANE), 1)
    dest_ref[...] = jnp.where(lane == 0, d0, jnp.where(lane == 1, d1, 0.0)).astype(I32)

    lane_b = lax.broadcasted_iota(I32, (n_blk_pad, LANE), 1)
    start = (lax.broadcasted_iota(I32, (n_blk_pad, LANE), 0) * EXPERT_BLOCK).astype(F32)
    hit = jnp.where((lane_b < N_EXPERTS) & (pend[0:1, :] <= start), 1.0, 0.0)
    e_blk = jnp.minimum(jnp.sum(hit, axis=-1, keepdims=True), N_EXPERTS - 1.0)
    used = jnp.max(pend[0:1, :], axis=-1, keepdims=True) * (1.0 / EXPERT_BLOCK)
    bexp_ref[...] = jnp.where(lane_b == 0, e_blk, jnp.where(lane_b == 1, used, 0.0)).astype(I32)


def _moe_dest(rt, rank, cnt, n_blk, rows=256):
    n = rt.shape[0]
    n_blk_pad = -(-n_blk // 8) * 8
    return pl.pallas_call(
        functools.partial(_dest_kernel, n_blk_pad=n_blk_pad),
        grid=(n // rows,),
        in_specs=[pl.BlockSpec((rows, LANE), lambda i: (i, 0)), pl.BlockSpec((rows, LANE), lambda i: (i, 0)),
                  pl.BlockSpec((8, LANE), lambda i: (0, 0))],
        out_specs=[pl.BlockSpec((rows, LANE), lambda i: (i, 0)), pl.BlockSpec((n_blk_pad, LANE), lambda i: (0, 0))],
        out_shape=[jax.ShapeDtypeStruct((n, LANE), I32), jax.ShapeDtypeStruct((n_blk_pad, LANE), I32)],
        compiler_params=_cparams(("arbitrary",)),
        name="moe_dest",
    )(rt, rank, cnt)


MOE_ROWS = 512


def _dispatch_kernel(dest_ref, h_ref, xin_ref, xpad_ref, sem):
    del xin_ref

    def row_copy(r, d):
        return pltpu.make_async_copy(h_ref.at[pl.ds(r, 1)], xpad_ref.at[pl.ds(d, 1)], sem)

    def start(r, carry):
        for j in range(EXPERT_TOPK):
            row_copy(r, dest_ref[EXPERT_TOPK * r + j]).start(priority=j)
        return carry

    lax.fori_loop(0, MOE_ROWS, start, 0)

    def wait(r, carry):
        for j in range(EXPERT_TOPK):
            row_copy(r, dest_ref[EXPERT_TOPK * r + j]).wait()
        return carry

    lax.fori_loop(0, MOE_ROWS, wait, 0)


def _moe_dispatch(dest_flat, h, n_slots):
    n, d = h.shape
    zeros = jnp.zeros((n_slots, d), h.dtype)
    return pl.pallas_call(
        _dispatch_kernel,
        grid=(n // MOE_ROWS,),
        in_specs=[pl.BlockSpec((EXPERT_TOPK * MOE_ROWS,), lambda i: (i,), memory_space=pltpu.SMEM),
                  pl.BlockSpec((MOE_ROWS, d), lambda i: (i, 0)),
                  pl.BlockSpec(memory_space=pl.ANY)],
        out_specs=pl.BlockSpec(memory_space=pl.ANY),
        out_shape=jax.ShapeDtypeStruct((n_slots, d), h.dtype),
        scratch_shapes=[pltpu.SemaphoreType.DMA(())],
        input_output_aliases={2: 0},
        compiler_params=_cparams(("arbitrary",)),
        name="moe_dispatch",
    )(dest_flat, h, zeros)


def _expert_kernel(bexp_ref, used_ref, x_ref, wg_ref, wu_ref, wd_ref, y_ref):
    blk = pl.program_id(0)

    @pl.when(blk < used_ref[0])
    def _():
        xb = x_ref[...].astype(BF16)
        gate = _dot(xb, wg_ref[...].astype(BF16))
        up = _dot(xb, wu_ref[...].astype(BF16))
        act = (jax.nn.silu(gate) * up).astype(BF16)
        y_ref[...] = _dot(act, wd_ref[...].astype(BF16))

    @pl.when(blk >= used_ref[0])
    def _():
        y_ref[...] = jnp.zeros_like(y_ref)


def _moe_experts(bexp, used, xpad, w_gate, w_up, w_down):
    n_slots, d = xpad.shape
    n_blk = n_slots // EXPERT_BLOCK
    de = w_gate.shape[-1]
    grid_spec = pltpu.PrefetchScalarGridSpec(
        num_scalar_prefetch=2,
        grid=(n_blk,),
        in_specs=[
            pl.BlockSpec((EXPERT_BLOCK, d), lambda i, be, us: (i, 0)),
            pl.BlockSpec((None, d, de), lambda i, be, us: (be[i], 0, 0)),
            pl.BlockSpec((None, d, de), lambda i, be, us: (be[i], 0, 0)),
            pl.BlockSpec((None, de, d), lambda i, be, us: (be[i], 0, 0)),
        ],
        out_specs=pl.BlockSpec((EXPERT_BLOCK, d), lambda i, be, us: (i, 0)),
    )
    return pl.pallas_call(
        _expert_kernel,
        grid_spec=grid_spec,
        out_shape=jax.ShapeDtypeStruct((n_slots, d), F32),
        compiler_params=_cparams(("arbitrary",), 48),
        name="moe_experts",
    )(bexp, used, xpad, w_gate, w_up, w_down)


def _combine_kernel(dest_ref, h_ref, rw_ref, g_ref, b_ref, y_ref, o_ref, buf_ref, sem, *, alpha):
    def row_copy(r, j):
        return pltpu.make_async_copy(y_ref.at[pl.ds(dest_ref[EXPERT_TOPK * r + j], 1)],
                                     buf_ref.at[j, pl.ds(r, 1)], sem)

    def start(r, carry):
        for j in range(EXPERT_TOPK):
            row_copy(r, j).start(priority=j)
        return carry

    lax.fori_loop(0, MOE_ROWS, start, 0)

    def wait(r, carry):
        for j in range(EXPERT_TOPK):
            row_copy(r, j).wait()
        return carry

    lax.fori_loop(0, MOE_ROWS, wait, 0)
    moe = buf_ref[0] * rw_ref[:, 0:1] + buf_ref[1] * rw_ref[:, 1:2]
    o_ref[...] = _layer_norm(alpha * h_ref[...] + moe, g_ref[...], b_ref[...])


def _moe_combine(dest_flat, h, rw, g, b_, ypad, alpha):
    n, d = h.shape
    return pl.pallas_call(
        functools.partial(_combine_kernel, alpha=alpha),
        grid=(n // MOE_ROWS,),
        in_specs=[pl.BlockSpec((EXPERT_TOPK * MOE_ROWS,), lambda i: (i,), memory_space=pltpu.SMEM),
                  pl.BlockSpec((MOE_ROWS, d), lambda i: (i, 0)),
                  pl.BlockSpec((MOE_ROWS, LANE), lambda i: (i, 0)),
                  pl.BlockSpec((1, d), lambda i: (0, 0)),
                  pl.BlockSpec((1, d), lambda i: (0, 0)),
                  pl.BlockSpec(memory_space=pl.ANY)],
        out_specs=pl.BlockSpec((MOE_ROWS, d), lambda i: (i, 0)),
        out_shape=jax.ShapeDtypeStruct((n, d), F32),
        scratch_shapes=[pltpu.VMEM((EXPERT_TOPK, MOE_ROWS, d), F32), pltpu.SemaphoreType.DMA(())],
        compiler_params=_cparams(("arbitrary",), 48),
        name="moe_combine",
    )(dest_flat, h, rw, g, b_, ypad)


def _cmp_map_t(t):
    nc = (t - CMP_BLOCK) // CMP_STRIDE + 1
    ns = t // SLC_BLOCK
    ncp = t // CMP_STRIDE
    cs = CMP_STRIDE * np.arange(nc)[:, None]
    ss = SLC_BLOCK * np.arange(ns)[None, :]
    ov = np.minimum(cs + CMP_BLOCK, ss + SLC_BLOCK) - np.maximum(cs, ss)
    m = np.clip(ov, 0, None).astype(np.float32) / CMP_STRIDE
    out = np.zeros((ns, ncp), np.float32)
    out[:, :nc] = m.T
    return jnp.asarray(out, BF16)


def _block_expand(t):
    ns = t // SLC_BLOCK
    rows = max(ns, LANE)
    e = np.zeros((rows, t), np.float32)
    e[np.arange(t) // SLC_BLOCK, np.arange(t)] = 1.0
    return jnp.asarray(e, BF16)


def _pad_head_rows(w, n_heads):
    wh = w.reshape(n_heads, HEAD_DIM, w.shape[-1])
    return jnp.concatenate([wh, jnp.zeros_like(wh)], axis=1).reshape(n_heads * LANE, w.shape[-1]).astype(BF16)


def kernel(x, w_in, cmp_pe_k, cmp_pe_v, cmp_w1_k, cmp_w2_k, cmp_w1_v, cmp_w2_v, ckv_norm_g, w_uk, w_uv, rel_bias,
           w_branch_a, w_branch_b, w_out, ln1_g, ln1_b, w_grp, b_grp, w_rtr, b_rtr, w_gate, w_up, w_down, ln2_g,
           ln2_b):
    b, t, d = x.shape
    n = b * t
    depth = w_in.shape[0]
    alpha = (2.0 * depth) ** 0.25
    assert t % 512 == 0 and t >= WIN_KEYS and n % MOE_ROWS == 0

    sslc = _bias_strip(rel_bias, NSA_HEADS, 0, STRIP_A, STRIP_W, None, True)
    sdsa = _bias_strip(rel_bias, DSA_HEADS, NSA_HEADS, STRIP_A, STRIP_W, None, True)
    swin = _bias_strip(rel_bias, NSA_HEADS, 0, WIN_A, WIN_W, WINDOW, False)
    bcmp = _cmp_bias(rel_bias, t)
    mapt = _cmp_map_t(t)
    eall = _block_expand(t)

    n_a = n * EXPERT_TOPK
    n_blk = -(-n_a // EXPERT_BLOCK) + N_EXPERTS
    n_slots = n_blk * EXPERT_BLOCK

    h = x.reshape(n, d)
    for l in range(depth):
        w_pad = _proj_weights(w_in[l])
        wuk = w_uk[l]
        z = jnp.zeros_like(wuk[0])
        wuk_pairs = jnp.stack([
            jnp.concatenate([jnp.concatenate([wuk[2 * k], z], axis=1), jnp.concatenate([z, wuk[2 * k + 1]], axis=1)],
                            axis=0) for k in range(DSA_HEADS // 2)]).astype(BF16)
        wuv_pad = jnp.pad(w_uv[l], ((0, 0), (0, 0), (0, LANE - HEAD_DIM))).astype(BF16)
        wk = _cmp_weights(cmp_w1_k[l], cmp_w2_k[l], cmp_pe_k[l], HEAD_DIM)
        wv = _cmp_weights(cmp_w1_v[l], cmp_w2_v[l], cmp_pe_v[l], LANE)
        wa_pad = _pad_head_rows(w_branch_a[l], NSA_HEADS)
        wb_pad = _pad_head_rows(w_branch_b[l], DSA_HEADS)
        wr = jnp.zeros((d, LANE), F32).at[:, RT_GRP:RT_GRP + N_EXPERT_GROUPS].set(w_grp[l])
        wr = wr.at[:, RT_EXP:RT_EXP + N_EXPERTS].set(w_rtr[l])
        br = jnp.zeros((1, LANE), F32).at[0, RT_GRP:RT_GRP + N_EXPERT_GROUPS].set(b_grp[l])
        br = br.at[0, RT_EXP:RT_EXP + N_EXPERTS].set(b_rtr[l])

        (qa, kc, vc, kv, gn, qlat, ckvn, qidx, kidx, widx, sga, sgb) = _proj(
            h, w_pad, wuk_pairs, ckv_norm_g[l].reshape(1, KV_RANK))
        kcmp, vcmp = _compress(kc, vc, wk, wv, b, t)

        def b3(a):
            return a.reshape(b, t, a.shape[-1])

        oa = _nsa(b3(qa), b3(gn), kcmp, vcmp, b3(kv), sslc, swin, bcmp, mapt, eall, b, t)
        ob = _dsa(b3(qlat), b3(qidx), b3(widx), b3(kidx), b3(ckvn), sdsa, wuv_pad, b, t)

        h1, rt, rw = _post(h, oa.reshape(n, -1), ob.reshape(n, -1), sga, sgb, wa_pad, wb_pad, w_out[l].astype(BF16),
                           ln1_g[l].reshape(1, d), ln1_b[l].reshape(1, d), wr, br, alpha)

        rank, cnt = _moe_rank(rt)
        dest, bexp = _moe_dest(rt, rank, cnt, n_blk)
        dest_flat = dest[:, :EXPERT_TOPK].reshape(n_a)
        xpad = _moe_dispatch(dest_flat, h1, n_slots)
        ypad = _moe_experts(bexp[:n_blk, 0], bexp[:1, 1], xpad, w_gate[l], w_up[l], w_down[l])
        h = _moe_combine(dest_flat, h1, rw, ln2_g[l].reshape(1, d), ln2_b[l].reshape(1, d), ypad, alpha)
    return h.reshape(b, t, d)
```

```python
import functools
import math

import numpy as np
import jax
import jax.numpy as jnp
from jax import lax
from jax.experimental import pallas as pl
from jax.experimental.pallas import tpu as pltpu

F32 = jnp.float32
BF16 = jnp.bfloat16
I32 = jnp.int32

D_MODEL = 1024
HEAD_DIM = 64
NSA_HEADS = 8
NSA_GROUPS = 2
NSA_HPG = NSA_HEADS // NSA_GROUPS
CMP_BLOCK = 32
CMP_STRIDE = 16
CMP_HIDDEN = 256
SLC_BLOCK = 64
SLC_TOPN = 16
SLC_FORCE = 1e4
WINDOW = 512
DSA_HEADS = 8
KV_RANK = 128
IDX_HEADS = 4
IDX_DIM = 64
IDX_TOPK_MAX = 256
REL_BUCKETS = 32
REL_EXACT = 16
REL_MAX_DIST = 1024
N_EXPERT_GROUPS = 4
EXPERTS_PER_GROUP = 8
N_EXPERTS = N_EXPERT_GROUPS * EXPERTS_PER_GROUP
EXPERT_TOPK = 2
D_EXPERT = 256
EXPERT_BLOCK = 256
NSA_WIDTH = NSA_HEADS * HEAD_DIM
DSA_WIDTH = DSA_HEADS * HEAD_DIM
NEG = -1e30
SPLIT_SIZES = (NSA_WIDTH,) + (NSA_GROUPS * HEAD_DIM,) * 6 + (
    NSA_HEADS * 3, DSA_WIDTH, KV_RANK, IDX_HEADS * IDX_DIM, IDX_DIM, IDX_HEADS, D_MODEL, D_MODEL)
SPLIT_POINTS = tuple(int(v) for v in np.cumsum(SPLIT_SIZES)[:-1])

LANE = 128
QT = 128
KT = 512
TPK = KT // QT
RB = 64
M_FLOOR = -1e29
BIG = 3e38
LOG2E = math.log2(math.e)

FAR_TILES = 11
STRIP_A = FAR_TILES * QT
STRIP_W = STRIP_A + KT
assert REL_EXACT + int(math.log((STRIP_A - KT + 1) / REL_EXACT) / math.log(REL_MAX_DIST / REL_EXACT)
                       * (REL_BUCKETS - REL_EXACT)) >= REL_BUCKETS - 1
WIN_A = WINDOW
WIN_KEYS = WINDOW + QT
WIN_W = WIN_A + WIN_KEYS

O_QA, O_KC, O_VC, O_KV, O_GN, O_QB, O_CKV, O_QIDX, O_KIDX, O_WIDX, O_GA, O_GB, PROJ_W = (
    int(v) for v in np.cumsum([0, 1024, 128, 128, 768, 128, 512, 128, 512, 128, 128, 1024, 1024]))
KV_KS, KV_VS, KV_KW, KV_VW, KV_W = 0, 128, 384, 512, 768
ONES_LANE = HEAD_DIM

_NT = (((1,), (1,)), ((), ()))


def _dot(a, b):
    return jnp.dot(a, b, preferred_element_type=F32)


def _dot_nt(a, b):
    return lax.dot_general(a, b, _NT, preferred_element_type=F32)


def _cparams(sem, vmem_mb=None):
    kw = dict(dimension_semantics=sem)
    if vmem_mb is not None:
        kw["vmem_limit_bytes"] = vmem_mb * 1024 * 1024
    return pltpu.CompilerParams(**kw)


def _proj_weights(w_in):
    (q_a, kc, vc, ks, vs, kw, vw, g_nsa, q_b, ckv, q_idx, k_idx, w_idx, ga, gb) = jnp.split(w_in, SPLIT_POINTS, axis=1)
    d = w_in.shape[0]
    scale = HEAD_DIM ** -0.5 * LOG2E
    z64 = jnp.zeros((d, NSA_HPG, HEAD_DIM), F32)
    qa = (q_a * scale).reshape(d, NSA_GROUPS, NSA_HPG, HEAD_DIM)
    qa_pad = jnp.concatenate([
        jnp.concatenate([qa[:, 0], z64], axis=-1).reshape(d, NSA_HPG * LANE),
        jnp.concatenate([z64, qa[:, 1]], axis=-1).reshape(d, NSA_HPG * LANE)], axis=1)
    qi = q_idx.reshape(d, IDX_HEADS, IDX_DIM)
    qi_pad = jnp.concatenate([qi, jnp.zeros_like(qi)], axis=-1).reshape(d, IDX_HEADS * LANE)

    def pad(a):
        return jnp.pad(a, ((0, 0), (0, LANE - a.shape[1])))

    def per_group(v):
        return jnp.concatenate([pad(v[:, :HEAD_DIM]), pad(v[:, HEAD_DIM:])], axis=1)

    w_idx_s = w_idx * (IDX_HEADS ** -0.5 * IDX_DIM ** -0.5)
    cols = [qa_pad, kc, vc, ks, per_group(vs), kw, per_group(vw), pad(g_nsa), q_b * scale, ckv, qi_pad, pad(k_idx),
            pad(w_idx_s), ga, gb]
    w = jnp.concatenate(cols, axis=1)
    assert w.shape[1] == PROJ_W
    return w.astype(BF16)


def _proj_kernel(x_ref, w_ref, wuk_ref, g_ref, qa_o, kc_o, vc_o, kv_o, gn_o, qlat_o, ckv_o, qidx_o, kidx_o,
                 widx_o, sga_o, sgb_o):
    xb = x_ref[...].astype(BF16)

    def mm(lo, n):
        return _dot(xb, w_ref[:, lo:lo + n])

    qa_o[...] = mm(O_QA, 1024).astype(BF16)
    kc_o[...] = mm(O_KC, 128).astype(BF16)
    vc_o[...] = mm(O_VC, 128).astype(BF16)
    lane = lax.broadcasted_iota(I32, (x_ref.shape[0], KV_W), 1)
    is_one = ((lane & (LANE - 1)) == ONES_LANE) & (((lane >= KV_VS) & (lane < KV_KW)) | (lane >= KV_VW))
    kv_o[...] = jnp.where(is_one, 1.0, mm(O_KV, KV_W)).astype(BF16)
    gn_o[...] = jax.nn.sigmoid(mm(O_GN, 128))
    qb = mm(O_QB, 512).astype(BF16)
    for k in range(DSA_HEADS // 2):
        qlat_o[:, 256 * k:256 * (k + 1)] = _dot(qb[:, 128 * k:128 * (k + 1)], wuk_ref[k]).astype(BF16)
    c = mm(O_CKV, 128)
    ms = jnp.mean(c * c, axis=-1, keepdims=True)
    ckv_o[...] = (c * lax.rsqrt(ms + 1e-6) * g_ref[...]).astype(BF16)
    qidx_o[...] = mm(O_QIDX, 512).astype(BF16)
    kidx_o[...] = mm(O_KIDX, 128).astype(BF16)
    widx_o[...] = mm(O_WIDX, 128)
    sga_o[...] = jax.nn.sigmoid(mm(O_GA, 1024)).astype(BF16)
    sgb_o[...] = jax.nn.sigmoid(mm(O_GB, 1024)).astype(BF16)


def _proj(x2, w_pad, wuk_pairs, ckv_g, rows=512):
    n, d = x2.shape
    widths = [(1024, BF16), (128, BF16), (128, BF16), (KV_W, BF16), (128, F32), (1024, BF16), (128, BF16),
              (512, BF16), (128, BF16), (128, F32), (1024, BF16), (1024, BF16)]
    return pl.pallas_call(
        _proj_kernel,
        grid=(n // rows,),
        in_specs=[
            pl.BlockSpec((rows, d), lambda i: (i, 0)),
            pl.BlockSpec((d, PROJ_W), lambda i: (0, 0)),
            pl.BlockSpec((DSA_HEADS // 2, 128, 256), lambda i: (0, 0, 0)),
            pl.BlockSpec((1, KV_RANK), lambda i: (0, 0)),
        ],
        out_specs=[pl.BlockSpec((rows, w), lambda i: (i, 0)) for w, _ in widths],
        out_shape=[jax.ShapeDtypeStruct((n, w), dt) for w, dt in widths],
        compiler_params=_cparams(("arbitrary",), 56),
        name="proj",
    )(x2, w_pad, wuk_pairs, ckv_g)


def _cmp_weights(w1, w2, pe, out_lanes):
    half = CMP_BLOCK // 2
    w1r = w1.reshape(CMP_BLOCK, HEAD_DIM, CMP_HIDDEN)
    eye = jnp.eye(NSA_GROUPS, dtype=F32)

    def expand(wl):
        return jnp.einsum('ldj,gh->lgdhj', wl, eye).reshape(half * NSA_GROUPS * HEAD_DIM, NSA_GROUPS * CMP_HIDDEN)

    top, bot = expand(w1r[:half]), expand(w1r[half:])

    def pe_rows(p):
        return jnp.broadcast_to(p[:, None, :], (half, NSA_GROUPS, HEAD_DIM)).reshape(1, -1)

    w2p = w2 if out_lanes == HEAD_DIM else jnp.pad(w2, ((0, 0), (0, out_lanes - HEAD_DIM)))
    w2bd = jnp.einsum('jd,gh->gjhd', w2p, eye).reshape(NSA_GROUPS * CMP_HIDDEN, NSA_GROUPS * out_lanes)
    return (top.astype(BF16), bot.astype(BF16), pe_rows(pe[:half]).astype(BF16), pe_rows(pe[half:]).astype(BF16),
            w2bd.astype(BF16))


def _compress_kernel(hk_ref, hv_ref, kt_ref, kb_ref, kpt_ref, kpb_ref, k2_ref, vt_ref, vb_ref, vpt_ref, vpb_ref,
                     v2_ref, ko_ref, vo_ref):
    ncp = hk_ref.shape[0]

    def one(h_ref, top_ref, bot_ref, pt_ref, pb_ref, w2_ref, o_ref):
        h = h_ref[...]
        a = _dot(h, top_ref[...])
        b = _dot(h, bot_ref[...])
        pe8t = jnp.broadcast_to(pt_ref[...], (8, pt_ref.shape[1]))
        pe8b = jnp.broadcast_to(pb_ref[...], (8, pb_ref.shape[1]))
        pe_term = (_dot(pe8t, top_ref[...]) + _dot(pe8b, bot_ref[...]))[0:1]
        pre = a + pltpu.roll(b, ncp - 1, 0) + pe_term
        hid = jax.nn.gelu(pre, approximate=True)
        o_ref[...] = _dot(hid.astype(BF16), w2_ref[...]).astype(BF16)

    one(hk_ref, kt_ref, kb_ref, kpt_ref, kpb_ref, k2_ref, ko_ref)
    one(hv_ref, vt_ref, vb_ref, vpt_ref, vpb_ref, v2_ref, vo_ref)


def _compress(kc, vc, wk, wv, b, t):
    ncp = t // CMP_STRIDE
    hw = CMP_STRIDE * NSA_GROUPS * HEAD_DIM
    hk = kc.reshape(b, ncp, hw)
    hv = vc.reshape(b, ncp, hw)
    hspec = pl.BlockSpec((None, ncp, hw), lambda i: (i, 0, 0))

    def full(a):
        return pl.BlockSpec(a.shape, lambda i: (0,) * a.ndim)

    widths = (wk[-1].shape[1], wv[-1].shape[1])
    return pl.pallas_call(
        _compress_kernel,
        grid=(b,),
        in_specs=[hspec, hspec] + [full(a) for a in wk] + [full(a) for a in wv],
        out_specs=[pl.BlockSpec((None, ncp, w), lambda i: (i, 0, 0)) for w in widths],
        out_shape=[jax.ShapeDtypeStruct((b, ncp, w), BF16) for w in widths],
        compiler_params=_cparams(("arbitrary",), 48),
        name="compress",
    )(hk, hv, *wk, *wv)


def _rel_bucket(dist):
    n = jnp.maximum(dist, 0)
    nf = jnp.maximum(n, 1).astype(F32)
    large = REL_EXACT + (jnp.log(nf / REL_EXACT) / math.log(REL_MAX_DIST / REL_EXACT)
                         * (REL_BUCKETS - REL_EXACT)).astype(I32)
    return jnp.where(n < REL_EXACT, n, jnp.minimum(large, REL_BUCKETS - 1))


def _rel_lookup(tab_ref, bucket, col):
    bits = [(bucket & (1 << k)) != 0 for k in range(5)]
    vals = [jnp.where(bits[0], tab_ref[2 * k + 1, col], tab_ref[2 * k, col]) for k in range(REL_BUCKETS // 2)]
    for lvl in range(1, 5):
        vals = [jnp.where(bits[lvl], vals[2 * k + 1], vals[2 * k]) for k in range(len(vals) // 2)]
    return vals[0]


def _strip_kernel(tab_ref, o_ref, *, a, window, head0, rel_far):
    h = pl.program_id(0)
    shape = o_ref.shape
    r = lax.broadcasted_iota(I32, shape, 0)
    j = lax.broadcasted_iota(I32, shape, 1)
    dist = r + a - j
    valid = dist >= 0
    if window is not None:
        valid = valid & (dist < window)
    val = _rel_lookup(tab_ref, _rel_bucket(dist), h + head0)
    if rel_far:
        val = val - tab_ref[REL_BUCKETS - 1, h + head0]
    o_ref[...] = jnp.where(valid, val * LOG2E, NEG)


def _bias_strip(rel_bias, n_heads, head0, a, width, window, rel_far):
    return pl.pallas_call(
        functools.partial(_strip_kernel, a=a, window=window, head0=head0, rel_far=rel_far),
        grid=(n_heads,),
        in_specs=[pl.BlockSpec(memory_space=pltpu.SMEM)],
        out_specs=pl.BlockSpec((None, QT, width), lambda h: (h, 0, 0)),
        out_shape=jax.ShapeDtypeStruct((n_heads, QT, width), F32),
        compiler_params=_cparams(("arbitrary",)),
        name="bias_strip",
    )(rel_bias)


def _cmp_bias_kernel(tab_ref, o_ref, *, nc):
    i = pl.program_id(0)
    h = pl.program_id(1)
    shape = o_ref.shape
    r = lax.broadcasted_iota(I32, shape, 0)
    c = lax.broadcasted_iota(I32, shape, 1)
    dist = i * QT + r - (CMP_STRIDE * c + CMP_BLOCK - 1)
    valid = (dist >= 0) & (c < nc)
    val = _rel_lookup(tab_ref, _rel_bucket(dist), h)
    o_ref[...] = jnp.where(valid, val * LOG2E, NEG)


def _cmp_bias(rel_bias, t):
    n_qt = t // QT
    ncp = t // CMP_STRIDE
    nc = (t - CMP_BLOCK) // CMP_STRIDE + 1
    return pl.pallas_call(
        functools.partial(_cmp_bias_kernel, nc=nc),
        grid=(n_qt, NSA_HEADS),
        in_specs=[pl.BlockSpec(memory_space=pltpu.SMEM)],
        out_specs=pl.BlockSpec((None, None, QT, ncp), lambda i, h: (i, h, 0, 0)),
        out_shape=jax.ShapeDtypeStruct((n_qt, NSA_HEADS, QT, ncp), F32),
        compiler_params=_cparams(("arbitrary", "arbitrary")),
        name="cmp_bias",
    )(rel_bias)


def _lane_groups(op, x):
    parts = [x[:, c * LANE:(c + 1) * LANE] for c in range(x.shape[1] // LANE)]
    while len(parts) > 1:
        parts = [op(parts[j], parts[j + 1]) for j in range(0, len(parts), 2)]
    return parts[0]


def _far_tiles(i):
    return jnp.maximum((i - (FAR_TILES - TPK)) >> (TPK.bit_length() - 1), 0)


def _strip_offset(i, kt):
    return pl.multiple_of(QT * jnp.maximum(FAR_TILES - (i - TPK * kt), 0), LANE)


def _nsa_kernel(qa_ref, gn_ref, kc_ref, vc_ref, kv_ref, sslc_ref, swin_ref, bcmp_ref, mapt_ref, eall_ref, o_ref,
                q_ref, s_ref, z_ref, p_ref, m_ref, acc_ref, ocmp_ref, sel_ref, st_ref, *, t):
    i = pl.program_id(1)
    q0 = i * QT
    ns = t // SLC_BLOCK
    ncp = t // CMP_STRIDE
    n_top = min(SLC_TOPN, ns)
    grows = NSA_HPG * QT
    arows = NSA_HEADS * QT
    shift = SLC_BLOCK.bit_length() - 1
    halves = QT // RB

    for h in range(NSA_HEADS):
        q_ref[h * QT:(h + 1) * QT, :] = qa_ref[:, h * LANE:(h + 1) * LANE]

    s_ref[:, 0:ncp] = _dot_nt(q_ref[...], kc_ref[...])
    imp = []
    for g in range(NSA_GROUPS):
        psum = None
        for h in range(NSA_HPG * g, NSA_HPG * (g + 1)):
            rows = slice(h * QT, (h + 1) * QT)
            z = s_ref[rows, 0:ncp] + bcmp_ref[h]
            m = jnp.maximum(jnp.max(z, axis=-1, keepdims=True), M_FLOOR)
            e = jnp.exp2(z - m)
            p = e / jnp.maximum(jnp.sum(e, axis=-1, keepdims=True), 1e-30)
            p_ref[rows, 0:ncp] = p.astype(BF16)
            psum = p if psum is None else psum + p
        grp = slice(g * grows, (g + 1) * grows)
        ocmp_ref[grp, :] = _dot(p_ref[grp, 0:ncp], vc_ref[:, g * LANE:(g + 1) * LANE])
        imp.append(_dot_nt(mapt_ref[...], psum.astype(BF16)))
    imp_t = jnp.concatenate(imp, axis=1)

    blk = lax.broadcasted_iota(I32, (ns, NSA_GROUPS * QT), 0)
    tq = q0 + (lax.broadcasted_iota(I32, (ns, NSA_GROUPS * QT), 1) & (QT - 1))
    cur = tq >> shift
    forced = (blk == 0) | (blk == cur) | (blk == cur - 1)
    avail = (blk << shift) <= tq
    score = jnp.where(avail, imp_t + jnp.where(forced, SLC_FORCE, 0.0), NEG)
    st_ref[0:ns, :] = score

    sub8 = 8
    groups = [score[v * sub8:(v + 1) * sub8] for v in range(ns // sub8)]
    blk8 = lax.broadcasted_iota(I32, (sub8, NSA_GROUPS * QT), 0)
    ranks = [jnp.zeros((sub8, NSA_GROUPS * QT), F32) for _ in groups]
    for jp in range(ns):
        rowb = jnp.broadcast_to(st_ref[jp:jp + 1, :], (sub8, NSA_GROUPS * QT))
        for v, sv in enumerate(groups):
            ge = jnp.where(rowb >= sv, 1.0, 0.0)
            gt = jnp.where(rowb > sv, 1.0, 0.0)
            if v * sub8 > jp:
                inc = ge
            elif v * sub8 + sub8 - 1 <= jp:
                inc = gt
            else:
                inc = jnp.where(blk8 + v * sub8 > jp, ge, gt)
            ranks[v] = ranks[v] + inc
    rank = jnp.concatenate(ranks, axis=0)
    sel_t = jnp.where((rank < n_top) & avail, 1.0, 0.0)
    if ns < LANE:
        sel_t = jnp.concatenate([sel_t, jnp.zeros((LANE - ns, NSA_GROUPS * QT), F32)], axis=0)
    for g in range(NSA_GROUPS):
        sel_ref[g] = sel_t[:, g * QT:(g + 1) * QT].T.astype(BF16)

    m_ref[...] = jnp.full((arows, LANE), M_FLOOR, F32)
    acc_ref[...] = jnp.zeros((arows, LANE), F32)

    def max_step(kt, near):
        k0 = pl.multiple_of(kt * KT, KT)
        s = _dot_nt(q_ref[...], kv_ref[pl.ds(k0, KT), KV_KS:KV_KS + LANE])
        for g in range(NSA_GROUPS):
            madd = (_dot(sel_ref[g], eall_ref[:, pl.ds(k0, KT)]) - 1.0) * (-NEG)
            for h in range(NSA_HPG * g, NSA_HPG * (g + 1)):
                for half in range(halves):
                    qrows = slice(half * RB, (half + 1) * RB)
                    rows = slice(h * QT + half * RB, h * QT + (half + 1) * RB)
                    z = s[rows] + madd[qrows]
                    if near:
                        z = z + sslc_ref[h, qrows, pl.ds(_strip_offset(i, kt), KT)]
                    z_ref[rows, pl.ds(k0, KT)] = z
                    m_ref[rows, :] = jnp.maximum(m_ref[rows, :], _lane_groups(jnp.maximum, z))

    n_far = _far_tiles(i)
    n_kt = (i >> (TPK.bit_length() - 1)) + 1
    lax.fori_loop(0, n_far, lambda kt, c: (max_step(kt, False), c)[1], 0)
    lax.fori_loop(n_far, n_kt, lambda kt, c: (max_step(kt, True), c)[1], 0)
    m_ref[...] = jnp.broadcast_to(jnp.max(m_ref[...], axis=-1, keepdims=True), (arows, LANE))

    def sum_step(kt, c):
        k0 = pl.multiple_of(kt * KT, KT)
        for g in range(NSA_GROUPS):
            ps = []
            for h in range(NSA_HPG * g, NSA_HPG * (g + 1)):
                for half in range(halves):
                    rows = slice(h * QT + half * RB, h * QT + (half + 1) * RB)
                    m = m_ref[rows, :]
                    z = z_ref[rows, pl.ds(k0, KT)]
                    ps.append(jnp.exp2(z - jnp.concatenate([m] * (KT // LANE), axis=1)).astype(BF16))
            grp = slice(g * grows, (g + 1) * grows)
            acc_ref[grp, :] += _dot(jnp.concatenate(ps, axis=0),
                                    kv_ref[pl.ds(k0, KT), KV_VS + g * LANE:KV_VS + (g + 1) * LANE])
        return c

    lax.fori_loop(0, n_kt, sum_step, 0)

    ks0 = pl.multiple_of(jnp.maximum(i - WINDOW // QT, 0) * QT, QT)
    woff = pl.multiple_of(jnp.maximum(WINDOW // QT - i, 0) * QT, LANE)
    s_ref[...] = _dot_nt(q_ref[...], kv_ref[pl.ds(ks0, WIN_KEYS), KV_KW:KV_KW + LANE])
    for h in range(NSA_HEADS):
        for half in range(halves):
            qrows = slice(half * RB, (half + 1) * RB)
            rows = slice(h * QT + half * RB, h * QT + (half + 1) * RB)
            z = s_ref[rows, :] + swin_ref[h, qrows, pl.ds(woff, WIN_KEYS)]
            p_ref[rows, :] = jnp.exp2(z - jnp.max(z, axis=-1, keepdims=True)).astype(BF16)

    lane_ok = lax.broadcasted_iota(I32, (QT, LANE), 1) < HEAD_DIM
    for g in range(NSA_GROUPS):
        grp = slice(g * grows, (g + 1) * grows)
        o_win = _dot(p_ref[grp, :], kv_ref[pl.ds(ks0, WIN_KEYS), KV_VW + g * LANE:KV_VW + (g + 1) * LANE])
        for hp in range(NSA_HPG):
            h = NSA_HPG * g + hp
            rows = slice(h * QT, (h + 1) * QT)
            wrows = slice(hp * QT, (hp + 1) * QT)
            slc = acc_ref[rows, :]
            o_slc = slc / jnp.maximum(slc[:, ONES_LANE:ONES_LANE + 1], 1e-30)
            win = o_win[wrows]
            o_w = win / jnp.maximum(win[:, ONES_LANE:ONES_LANE + 1], 1e-30)
            o = (gn_ref[:, 3 * h:3 * h + 1] * ocmp_ref[rows, :] + gn_ref[:, 3 * h + 1:3 * h + 2] * o_slc
                 + gn_ref[:, 3 * h + 2:3 * h + 3] * o_w)
            o_ref[:, h * LANE:(h + 1) * LANE] = jnp.where(lane_ok, o, 0.0).astype(BF16)


def _nsa(qa, gn, kcmp, vcmp, kv, sslc, swin, bcmp, mapt, eall, b, t):
    n_qt = t // QT
    ncp = t // CMP_STRIDE
    ns = t // SLC_BLOCK
    arows = NSA_HEADS * QT
    assert ncp <= WIN_KEYS

    def full(a):
        return pl.BlockSpec(a.shape, lambda bi, i: (0,) * a.ndim)

    def once(a):
        return pl.BlockSpec(a.shape, lambda bi, i: (0,) * a.ndim, pipeline_mode=pl.Buffered(1))

    return pl.pallas_call(
        functools.partial(_nsa_kernel, t=t),
        grid=(b, n_qt),
        in_specs=[
            pl.BlockSpec((None, QT, NSA_HEADS * LANE), lambda bi, i: (bi, i, 0)),
            pl.BlockSpec((None, QT, LANE), lambda bi, i: (bi, i, 0)),
            pl.BlockSpec((None, ncp, LANE), lambda bi, i: (bi, 0, 0)),
            pl.BlockSpec((None, ncp, NSA_GROUPS * LANE), lambda bi, i: (bi, 0, 0)),
            pl.BlockSpec((None, t, KV_W), lambda bi, i: (bi, 0, 0), pipeline_mode=pl.Buffered(1)),
            once(sslc), once(swin),
            pl.BlockSpec((None, NSA_HEADS, QT, ncp), lambda bi, i: (i, 0, 0, 0)),
            full(mapt), once(eall),
        ],
        out_specs=pl.BlockSpec((None, QT, NSA_HEADS * LANE), lambda bi, i: (bi, i, 0)),
        out_shape=jax.ShapeDtypeStruct((b, t, NSA_HEADS * LANE), BF16),
        scratch_shapes=[
            pltpu.VMEM((arows, LANE), BF16),
            pltpu.VMEM((arows, WIN_KEYS), F32),
            pltpu.VMEM((arows, t), F32),
            pltpu.VMEM((arows, WIN_KEYS), BF16),
            pltpu.VMEM((arows, LANE), F32),
            pltpu.VMEM((arows, LANE), F32),
            pltpu.VMEM((arows, LANE), F32),
            pltpu.VMEM((NSA_GROUPS, QT, LANE), BF16),
            pltpu.VMEM((max(ns, 8), NSA_GROUPS * QT), F32),
        ],
        compiler_params=_cparams(("arbitrary", "arbitrary"), 56),
        name="nsa",
    )(qa, gn, kcmp, vcmp, kv, sslc, swin, bcmp, mapt, eall)


IDX_CHUNK = 512
FAST_PASSES = 10
LANE_SHIFT = LANE.bit_length() - 1


def _dsa_kernel(ql_ref, qi_ref, wi_ref, ki_ref, ckv_ref, strip_ref, wuv_ref, o_ref,
                q_ref, idx_ref, tmp_ref, s_ref, p_ref, m_ref, l_ref, acc_ref, *, t, k_sel):
    i = pl.program_id(1)
    q0 = i * QT
    nch = (i >> 2) + 1
    hrows = DSA_HEADS * QT
    sub = IDX_CHUNK // LANE
    halves = QT // RB

    for h in range(IDX_HEADS):
        q_ref[h * QT:(h + 1) * QT, :] = qi_ref[:, h * LANE:(h + 1) * LANE]
    tq = q0 + lax.broadcasted_iota(I32, (QT, IDX_CHUNK), 0)
    col = lax.broadcasted_iota(I32, (QT, IDX_CHUNK), 1)

    def idx_body(c, carry):
        c0 = pl.multiple_of(c * IDX_CHUNK, IDX_CHUNK)
        d = jnp.maximum(_dot_nt(q_ref[0:IDX_HEADS * QT, :], ki_ref[pl.ds(c0, IDX_CHUNK), :]), 0.0)
        acc = d[0:QT] * wi_ref[:, 0:1]
        for h in range(1, IDX_HEADS):
            acc = acc + d[h * QT:(h + 1) * QT] * wi_ref[:, h:h + 1]
        idx_ref[:, pl.ds(c0, IDX_CHUNK)] = jnp.where(col + c0 <= tq, acc, NEG)
        return carry

    lax.fori_loop(0, nch, idx_body, 0)

    zeros = jnp.zeros((QT, LANE), F32)

    def scan(fn, init):
        def body(c, carry):
            for s in range(sub):
                off = pl.multiple_of(c * IDX_CHUNK + s * LANE, LANE)
                carry = fn(idx_ref[:, pl.ds(off, LANE)], off, carry)
            return carry
        return lax.fori_loop(0, nch, body, init)

    def search(_):
        kf = float(k_sel)
        big = jnp.full((QT, LANE), BIG, F32)

        def count_gt(pivot):
            pb = jnp.broadcast_to(pivot, (QT, LANE))
            cnt = scan(lambda x, off, c: c + jnp.where(x > pb, 1.0, 0.0), zeros)
            return jnp.sum(cnt, axis=-1, keepdims=True)

        def init_fn(x, off, carry):
            lo, hi = carry
            return jnp.minimum(lo, jnp.where(x > 0.5 * NEG, x, BIG)), jnp.maximum(hi, x)

        row_min, row_max = scan(init_fn, (big, -big))
        row_min = jnp.min(row_min, axis=-1, keepdims=True)
        row_max = jnp.max(row_max, axis=-1, keepdims=True)
        n_valid = (q0 + 1 + lax.broadcasted_iota(I32, (QT, 1), 0)).astype(F32)

        def fast_cond(c):
            return (jnp.min(c[7]) == 0) & (c[8] < FAST_PASSES)

        def fast_step(c):
            lo, hi, g_lo, g_hi, f_lo, f_hi, side, done_i, it = c
            done = done_i > 0
            mid = lo + (hi - lo) * (g_lo / (g_lo - g_hi))
            mid = jnp.where((mid > lo) & (mid < hi), mid, lo + (hi - lo) * 0.5)
            cnt = count_gt(mid)
            hit = cnt == kf
            up = cnt > kf
            move_lo = up | hit
            move_hi = jnp.logical_not(up)
            g_lo_n = jnp.where(up, cnt - kf, jnp.where(side == 2, g_lo * 0.5, g_lo))
            g_hi_n = jnp.where(up, jnp.where(side == 1, g_hi * 0.5, g_hi), cnt - kf)
            keep_old = done | hit
            return (jnp.where(done, lo, jnp.where(move_lo, mid, lo)), jnp.where(done, hi, jnp.where(move_hi, mid, hi)),
                    jnp.where(keep_old, g_lo, g_lo_n), jnp.where(keep_old, g_hi, g_hi_n),
                    jnp.where(done | jnp.logical_not(up), f_lo, cnt), jnp.where(done | up | hit, f_hi, cnt),
                    jnp.where(up, 1, 2), jnp.where(done | hit, 1, 0), it + 1)

        lo0 = row_min - (jnp.abs(row_min) + 1.0)
        init = (lo0, row_max, n_valid - kf, jnp.full((QT, 1), -kf, F32), n_valid, jnp.zeros((QT, 1), F32),
                jnp.zeros((QT, 1), I32), jnp.zeros((QT, 1), I32), jnp.int32(0))
        lo, hi, _, _, f_lo, f_hi, _, done_i, _ = lax.while_loop(fast_cond, fast_step, init)
        done = done_i > 0

        def exact(_):
            lob = jnp.broadcast_to(lo, (QT, LANE))
            hib = jnp.broadcast_to(hi, (QT, LANE))

            def snap_fn(x, off, c):
                a, b_ = c
                return (jnp.minimum(a, jnp.where(x > lob, x, BIG)), jnp.maximum(b_, jnp.where(x > hib, -BIG, x)))

            a, b_ = scan(snap_fn, (big, -big))
            lo_d = jnp.where(done, lo, jnp.min(a, axis=-1, keepdims=True))
            hi_d = jnp.where(done, hi, jnp.max(b_, axis=-1, keepdims=True))

            def cond(c):
                return jnp.max(jnp.where(c[0] < c[1], 1, 0)) > 0

            def step(c):
                lo, hi, c_lo, c_hi = c
                mid = lo + (hi - lo) * 0.5
                mid = jnp.where(mid < hi, mid, lo)
                midb = jnp.broadcast_to(mid, (QT, LANE))

                def fn(x, off, cc):
                    cnt, amin, bmax = cc
                    gt = x > midb
                    return (cnt + jnp.where(gt, 1.0, 0.0), jnp.minimum(amin, jnp.where(gt, x, BIG)),
                            jnp.maximum(bmax, jnp.where(gt, -BIG, x)))

                cnt, amin, bmax = scan(fn, (zeros, big, -big))
                cnt = jnp.sum(cnt, axis=-1, keepdims=True)
                amin = jnp.min(amin, axis=-1, keepdims=True)
                bmax = jnp.max(bmax, axis=-1, keepdims=True)
                up = cnt >= kf
                same = lo >= hi
                return (jnp.where(same | jnp.logical_not(up), lo, amin), jnp.where(same | up, hi, bmax),
                        jnp.where(same | jnp.logical_not(up), c_lo, cnt), jnp.where(same | up, c_hi, cnt))

            thr, _, c_ge, c_gt = lax.while_loop(cond, step, (lo_d, hi_d, f_lo, f_hi))
            return thr, c_ge, c_gt

        all_done = jnp.min(done_i) > 0
        thr, c_ge, c_gt = lax.cond(all_done, lambda _: (lo, f_lo, f_hi), exact, 0)
        c_ge = jnp.where(done, kf, c_ge)
        c_gt = jnp.where(done, kf, c_gt)
        thrb = jnp.broadcast_to(thr, (QT, LANE))
        need = k_sel - c_gt

        def tie_search(_):
            lane = lax.broadcasted_iota(I32, (QT, LANE), 1)
            lanef = lane.astype(F32)

            def grp_fn(x, off, g_cnt):
                c = jnp.sum(jnp.where(x == thrb, 1.0, 0.0), axis=-1, keepdims=True)
                return jnp.where(lane == (off >> LANE_SHIFT), c, g_cnt)

            g_cum = _lane_cumsum(scan(grp_fn, zeros))
            g_star = jnp.sum(jnp.where(g_cum < need, 1.0, 0.0), axis=-1, keepdims=True)
            before = jnp.sum(jnp.where(lanef == g_star - 1.0, g_cum, 0.0), axis=-1, keepdims=True)

            def slab_fn(x, off, slab):
                return jnp.where(g_star == (off >> LANE_SHIFT).astype(F32), x, slab)

            slab = scan(slab_fn, jnp.full((QT, LANE), NEG, F32))
            pre = _lane_cumsum(jnp.where(slab == thrb, 1.0, 0.0))
            lane_cut = jnp.sum(jnp.where(pre < need - before, 1.0, 0.0), axis=-1, keepdims=True)
            return (g_star * LANE + lane_cut).astype(I32)

        any_tie = jnp.max(jnp.where(c_ge > k_sel, 1, 0)) > 0
        p_cut = lax.cond(any_tie, tie_search, lambda _: jnp.full((QT, 1), t, I32), 0)
        return thr, jnp.where(done, -1, p_cut)

    thr, p_cut = lax.cond(q0 >= k_sel, search,
                          lambda _: (jnp.full((QT, 1), M_FLOOR, F32), jnp.full((QT, 1), t, I32)), 0)
    thrk = jnp.broadcast_to(thr, (QT, KT))
    pcutk = jnp.broadcast_to(p_cut, (QT, KT))
    colk = lax.broadcasted_iota(I32, (QT, KT), 1)

    for h in range(DSA_HEADS):
        q_ref[h * QT:(h + 1) * QT, :] = ql_ref[:, h * LANE:(h + 1) * LANE]
    m_ref[...] = jnp.full((hrows, LANE), M_FLOOR, F32)
    l_ref[...] = jnp.zeros((hrows, LANE), F32)
    acc_ref[...] = jnp.zeros((hrows, LANE), F32)
    blocks = [(h, half) for h in range(DSA_HEADS) for half in range(halves)]

    n_far = _far_tiles(i)
    n_kt = (i >> (TPK.bit_length() - 1)) + 1

    def max_step(kt, near):
        k0 = pl.multiple_of(kt * KT, KT)
        s = _dot_nt(q_ref[...], ckv_ref[pl.ds(k0, KT), :])
        x = idx_ref[:, pl.ds(k0, KT)]
        keep = (x > thrk) | ((x == thrk) & (colk + k0 <= pcutk))
        selm = jnp.where(keep, 0.0, NEG)
        for h, half in blocks:
            qrows = slice(half * RB, (half + 1) * RB)
            rows = slice(h * QT + half * RB, h * QT + (half + 1) * RB)
            z = s[rows] + selm[qrows]
            if near:
                z = z + strip_ref[h, qrows, pl.ds(_strip_offset(i, kt), KT)]
            s_ref[rows, pl.ds(k0, KT)] = z
            m_ref[rows, :] = jnp.maximum(m_ref[rows, :], _lane_groups(jnp.maximum, z))

    lax.fori_loop(0, n_far, lambda kt, c: (max_step(kt, False), c)[1], 0)
    lax.fori_loop(n_far, n_kt, lambda kt, c: (max_step(kt, True), c)[1], 0)
    m_ref[...] = jnp.broadcast_to(jnp.max(m_ref[...], axis=-1, keepdims=True), (hrows, LANE))

    def sum_step(kt, c):
        k0 = pl.multiple_of(kt * KT, KT)
        ps = []
        for h, half in blocks:
            rows = slice(h * QT + half * RB, h * QT + (half + 1) * RB)
            m = m_ref[rows, :]
            p = jnp.exp2(s_ref[rows, pl.ds(k0, KT)] - jnp.concatenate([m] * (KT // LANE), axis=1))
            l_ref[rows, :] += _lane_groups(jnp.add, p)
            ps.append(p.astype(BF16))
        acc_ref[...] += _dot(jnp.concatenate(ps, axis=0), ckv_ref[pl.ds(k0, KT), :])
        return c

    lax.fori_loop(0, n_kt, sum_step, 0)
    for h in range(DSA_HEADS):
        rows = slice(h * QT, (h + 1) * QT)
        l = jnp.sum(l_ref[rows, :], axis=-1, keepdims=True)
        o_lat = (acc_ref[rows, :] / jnp.maximum(l, 1e-30)).astype(BF16)
        o_ref[:, h * LANE:(h + 1) * LANE] = _dot(o_lat, wuv_ref[h]).astype(BF16)


def _dsa(qlat, qidx, widx, kidx, ckvn, strip, wuv_pad, b, t):
    n_qt = t // QT
    k_sel = min(IDX_TOPK_MAX, t // 4)
    assert k_sel % QT == 0 and t % IDX_CHUNK == 0 and t // LANE <= LANE
    hrows = DSA_HEADS * QT

    def full(a):
        return pl.BlockSpec(a.shape, lambda bi, i: (0,) * a.ndim)

    return pl.pallas_call(
        functools.partial(_dsa_kernel, t=t, k_sel=k_sel),
        grid=(b, n_qt),
        in_specs=[
            pl.BlockSpec((None, QT, DSA_HEADS * LANE), lambda bi, i: (bi, i, 0)),
            pl.BlockSpec((None, QT, IDX_HEADS * LANE), lambda bi, i: (bi, i, 0)),
            pl.BlockSpec((None, QT, LANE), lambda bi, i: (bi, i, 0)),
            pl.BlockSpec((None, t, LANE), lambda bi, i: (bi, 0, 0)),
            pl.BlockSpec((None, t, LANE), lambda bi, i: (bi, 0, 0)),
            pl.BlockSpec(strip.shape, lambda bi, i: (0, 0, 0), pipeline_mode=pl.Buffered(1)),
            full(wuv_pad),
        ],
        out_specs=pl.BlockSpec((None, QT, DSA_HEADS * LANE), lambda bi, i: (bi, i, 0)),
        out_shape=jax.ShapeDtypeStruct((b, t, DSA_HEADS * LANE), BF16),
        scratch_shapes=[
            pltpu.VMEM((hrows, LANE), BF16),
            pltpu.VMEM((QT, t), F32),
            pltpu.VMEM((hrows, KT), F32),
            pltpu.VMEM((hrows, t), F32),
            pltpu.VMEM((hrows, KT), BF16),
            pltpu.VMEM((hrows, LANE), F32),
            pltpu.VMEM((hrows, LANE), F32),
            pltpu.VMEM((hrows, LANE), F32),
        ],
        compiler_params=_cparams(("arbitrary", "arbitrary"), 56),
        name="dsa",
    )(qlat, qidx, widx, kidx, ckvn, strip, wuv_pad)


RT_GRP = 0
RT_EXP = 32


def _layer_norm(y, g, b):
    mu = jnp.mean(y, axis=-1, keepdims=True)
    yc = y - mu
    var = jnp.mean(yc * yc, axis=-1, keepdims=True)
    return yc * lax.rsqrt(var + 1e-5) * g + b


def _post_kernel(x_ref, oa_ref, ob_ref, sga_ref, sgb_ref, wa_ref, wb_ref, wo_ref, g_ref, b_ref, wr_ref, br_ref,
                 h_ref, rt_ref, rw_ref, *, alpha):
    rows = x_ref.shape[0]
    merged = (sga_ref[...].astype(F32) * _dot(oa_ref[...], wa_ref[...])
              + sgb_ref[...].astype(F32) * _dot(ob_ref[...], wb_ref[...]))
    y = alpha * x_ref[...] + _dot(merged.astype(BF16), wo_ref[...])
    h = _layer_norm(y, g_ref[...], b_ref[...])
    h_ref[...] = h

    h_hi = h.astype(BF16)
    h_lo = (h - h_hi.astype(F32)).astype(BF16)
    zz = _dot(h_hi, wr_ref[...])
    z = zz[:, 0:LANE] + zz[:, LANE:2 * LANE] + _dot(h_lo, wr_ref[:, 0:LANE]) + br_ref[...]
    lane = lax.broadcasted_iota(I32, (rows, LANE), 1)
    is_g = lane < N_EXPERT_GROUPS
    zg = jnp.where(is_g, z, -BIG)
    gmax = jnp.max(zg, axis=-1, keepdims=True)
    g_sel = jnp.min(jnp.where(is_g & (z == gmax), lane, LANE), axis=-1, keepdims=True)
    p_grp = 1.0 / jnp.sum(jnp.where(is_g, jnp.exp(zg - gmax), 0.0), axis=-1, keepdims=True)
    in_grp = (lane >= RT_EXP) & (lane < RT_EXP + N_EXPERTS) & (((lane - RT_EXP) >> 3) == g_sel)
    ze = jnp.where(in_grp, z, -BIG)
    m1 = jnp.max(ze, axis=-1, keepdims=True)
    i1 = jnp.min(jnp.where(in_grp & (z == m1), lane, LANE), axis=-1, keepdims=True)
    ze2 = jnp.where(lane == i1, -BIG, ze)
    m2 = jnp.max(ze2, axis=-1, keepdims=True)
    i2 = jnp.min(jnp.where(in_grp & (lane != i1) & (z == m2), lane, LANE), axis=-1, keepdims=True)
    e21 = jnp.exp(m2 - m1)
    den = 1.0 + e21
    w1 = p_grp * (1.0 / den)
    w2 = p_grp * (e21 / den)
    rt_ref[...] = jnp.where(lane == 0, i1 - RT_EXP, jnp.where(lane == 1, i2 - RT_EXP, 0))
    rw_ref[...] = jnp.where(lane == 0, w1, jnp.where(lane == 1, w2, 0.0))


def _post(x2, oa, ob, sga, sgb, wa, wb, wo, g, b_, wr, br, alpha, rows=256):
    n, d = x2.shape

    def row(w):
        return pl.BlockSpec((rows, w), lambda i: (i, 0))

    def full(a):
        return pl.BlockSpec(a.shape, lambda i: (0,) * a.ndim)

    return pl.pallas_call(
        functools.partial(_post_kernel, alpha=alpha),
        grid=(n // rows,),
        in_specs=[row(d), row(1024), row(1024), row(1024), row(1024), full(wa), full(wb), full(wo), full(g),
                  full(b_), full(wr), full(br)],
        out_specs=[row(d), row(LANE), row(LANE)],
        out_shape=[jax.ShapeDtypeStruct((n, d), F32), jax.ShapeDtypeStruct((n, LANE), I32),
                   jax.ShapeDtypeStruct((n, LANE), F32)],
        compiler_params=_cparams(("arbitrary",), 48),
        name="post",
    )(x2, oa, ob, sga, sgb, wa, wb, wo, g, b_, wr, br)


def _onehots(rt_ref, rows):
    lane = lax.broadcasted_iota(I32, (rows, LANE), 1)
    oh0 = jnp.where(lane == rt_ref[:, 0:1], 1.0, 0.0)
    oh1 = jnp.where(lane == rt_ref[:, 1:2], 1.0, 0.0)
    return oh0, oh1


def _rank_kernel(rt_ref, tri_ref, rank_ref, cnt_ref, carry_ref):
    rows = rt_ref.shape[0]

    @pl.when(pl.program_id(0) == 0)
    def _():
        carry_ref[...] = jnp.zeros_like(carry_ref)

    oh0, oh1 = _onehots(rt_ref, rows)
    both = oh0 + oh1
    before = _dot(tri_ref[...], both.astype(BF16)) + carry_ref[0:1, :]
    r0 = jnp.sum(oh0 * before, axis=-1, keepdims=True)
    r1 = jnp.sum(oh1 * before, axis=-1, keepdims=True)
    lane = lax.broadcasted_iota(I32, (rows, LANE), 1)
    rank_ref[...] = jnp.where(lane == 0, r0, jnp.where(lane == 1, r1, 0.0))
    carry_ref[...] = carry_ref[...] + jnp.sum(both, axis=0, keepdims=True)
    cnt_ref[...] = carry_ref[...]


def _moe_rank(rt, rows=512):
    n = rt.shape[0]
    tri = jnp.asarray(np.tril(np.ones((rows, rows), np.float32), -1), BF16)
    return pl.pallas_call(
        _rank_kernel,
        grid=(n // rows,),
        in_specs=[pl.BlockSpec((rows, LANE), lambda i: (i, 0)), pl.BlockSpec((rows, rows), lambda i: (0, 0))],
        out_specs=[pl.BlockSpec((rows, LANE), lambda i: (i, 0)), pl.BlockSpec((8, LANE), lambda i: (0, 0))],
        out_shape=[jax.ShapeDtypeStruct((n, LANE), F32), jax.ShapeDtypeStruct((8, LANE), F32)],
        scratch_shapes=[pltpu.VMEM((8, LANE), F32)],
        compiler_params=_cparams(("arbitrary",)),
        name="moe_rank",
    )(rt, tri)


def _lane_cumsum(v):
    lane = lax.broadcasted_iota(I32, v.shape, 1)
    s = 1
    while s < LANE:
        v = v + jnp.where(lane >= s, pltpu.roll(v, s, 1), 0.0)
        s *= 2
    return v


def _dest_kernel(rt_ref, rank_ref, cnt_ref, dest_ref, bexp_ref, *, n_blk_pad):
    rows = rt_ref.shape[0]
    lane8 = lax.broadcasted_iota(I32, (8, LANE), 1)
    cnt = jnp.where(lane8 < N_EXPERTS, cnt_ref[...], 0.0)
    padded = jnp.floor((cnt + (EXPERT_BLOCK - 1)) * (1.0 / EXPERT_BLOCK)) * EXPERT_BLOCK
    pend = _lane_cumsum(padded)
    poff = (pend - padded)[0:1, :]
    oh0, oh1 = _onehots(rt_ref, rows)
    d0 = jnp.sum(oh0 * poff, axis=-1, keepdims=True) + rank_ref[:, 0:1]
    d1 = jnp.sum(oh1 * poff, axis=-1, keepdims=True) + rank_ref[:, 1:2]
    lane = lax.broadcasted_iota(I32, (rows, LANE), 1)
    dest_ref[...] = jnp.where(lane == 0, d0, jnp.where(lane == 1, d1, 0.0)).astype(I32)

    lane_b = lax.broadcasted_iota(I32, (n_blk_pad, LANE), 1)
    start = (lax.broadcasted_iota(I32, (n_blk_pad, LANE), 0) * EXPERT_BLOCK).astype(F32)
    hit = jnp.where((lane_b < N_EXPERTS) & (pend[0:1, :] <= start), 1.0, 0.0)
    e_blk = jnp.minimum(jnp.sum(hit, axis=-1, keepdims=True), N_EXPERTS - 1.0)
    used = jnp.max(pend[0:1, :], axis=-1, keepdims=True) * (1.0 / EXPERT_BLOCK)
    bexp_ref[...] = jnp.where(lane_b == 0, e_blk, jnp.where(lane_b == 1, used, 0.0)).astype(I32)


def _moe_dest(rt, rank, cnt, n_blk, rows=1024):
    n = rt.shape[0]
    n_blk_pad = -(-n_blk // 8) * 8
    return pl.pallas_call(
        functools.partial(_dest_kernel, n_blk_pad=n_blk_pad),
        grid=(n // rows,),
        in_specs=[pl.BlockSpec((rows, LANE), lambda i: (i, 0)), pl.BlockSpec((rows, LANE), lambda i: (i, 0)),
                  pl.BlockSpec((8, LANE), lambda i: (0, 0))],
        out_specs=[pl.BlockSpec((rows, LANE), lambda i: (i, 0)), pl.BlockSpec((n_blk_pad, LANE), lambda i: (0, 0))],
        out_shape=[jax.ShapeDtypeStruct((n, LANE), I32), jax.ShapeDtypeStruct((n_blk_pad, LANE), I32)],
        compiler_params=_cparams(("arbitrary",)),
        name="moe_dest",
    )(rt, rank, cnt)


MOE_ROWS = 512
DMA_UNROLL = 8


def _dispatch_kernel(dest_ref, h_ref, xin_ref, xpad_ref, sem):
    del xin_ref

    def row_copy(r, d):
        return pltpu.make_async_copy(h_ref.at[pl.ds(r, 1)], xpad_ref.at[pl.ds(d, 1)], sem)

    def start(r, carry):
        for j in range(EXPERT_TOPK):
            row_copy(r, dest_ref[EXPERT_TOPK * r + j]).start(priority=j)
        return carry

    lax.fori_loop(0, MOE_ROWS, start, 0, unroll=DMA_UNROLL)

    def wait(r, carry):
        for j in range(EXPERT_TOPK):
            row_copy(r, dest_ref[EXPERT_TOPK * r + j]).wait()
        return carry

    lax.fori_loop(0, MOE_ROWS, wait, 0, unroll=DMA_UNROLL)


def _moe_dispatch(dest_flat, h, n_slots):
    n, d = h.shape
    zeros = jnp.zeros((n_slots, d), h.dtype)
    return pl.pallas_call(
        _dispatch_kernel,
        grid=(n // MOE_ROWS,),
        in_specs=[pl.BlockSpec((EXPERT_TOPK * MOE_ROWS,), lambda i: (i,), memory_space=pltpu.SMEM),
                  pl.BlockSpec((MOE_ROWS, d), lambda i: (i, 0)),
                  pl.BlockSpec(memory_space=pl.ANY)],
        out_specs=pl.BlockSpec(memory_space=pl.ANY),
        out_shape=jax.ShapeDtypeStruct((n_slots, d), h.dtype),
        scratch_shapes=[pltpu.SemaphoreType.DMA(())],
        input_output_aliases={2: 0},
        compiler_params=_cparams(("arbitrary",)),
        name="moe_dispatch",
    )(dest_flat, h, zeros)


def _expert_kernel(bexp_ref, used_ref, x_ref, wg_ref, wu_ref, wd_ref, y_ref):
    blk = pl.program_id(0)

    @pl.when(blk < used_ref[0])
    def _():
        xb = x_ref[...].astype(BF16)
        gate = _dot(xb, wg_ref[...].astype(BF16))
        up = _dot(xb, wu_ref[...].astype(BF16))
        act = (jax.nn.silu(gate) * up).astype(BF16)
        y_ref[...] = _dot(act, wd_ref[...].astype(BF16))

    @pl.when(blk >= used_ref[0])
    def _():
        y_ref[...] = jnp.zeros_like(y_ref)


def _moe_experts(bexp, used, xpad, w_gate, w_up, w_down):
    n_slots, d = xpad.shape
    n_blk = n_slots // EXPERT_BLOCK
    de = w_gate.shape[-1]
    grid_spec = pltpu.PrefetchScalarGridSpec(
        num_scalar_prefetch=2,
        grid=(n_blk,),
        in_specs=[
            pl.BlockSpec((EXPERT_BLOCK, d), lambda i, be, us: (i, 0)),
            pl.BlockSpec((None, d, de), lambda i, be, us: (be[i], 0, 0)),
            pl.BlockSpec((None, d, de), lambda i, be, us: (be[i], 0, 0)),
            pl.BlockSpec((None, de, d), lambda i, be, us: (be[i], 0, 0)),
        ],
        out_specs=pl.BlockSpec((EXPERT_BLOCK, d), lambda i, be, us: (i, 0)),
    )
    return pl.pallas_call(
        _expert_kernel,
        grid_spec=grid_spec,
        out_shape=jax.ShapeDtypeStruct((n_slots, d), F32),
        compiler_params=_cparams(("arbitrary",), 48),
        name="moe_experts",
    )(bexp, used, xpad, w_gate, w_up, w_down)


def _combine_kernel(dest_ref, h_ref, rw_ref, g_ref, b_ref, y_ref, o_ref, buf_ref, sem, *, alpha):
    def row_copy(r, j):
        return pltpu.make_async_copy(y_ref.at[pl.ds(dest_ref[EXPERT_TOPK * r + j], 1)],
                                     buf_ref.at[j, pl.ds(r, 1)], sem)

    def start(r, carry):
        for j in range(EXPERT_TOPK):
            row_copy(r, j).start(priority=j)
        return carry

    lax.fori_loop(0, MOE_ROWS, start, 0, unroll=DMA_UNROLL)

    def wait(r, carry):
        for j in range(EXPERT_TOPK):
            row_copy(r, j).wait()
        return carry

    lax.fori_loop(0, MOE_ROWS, wait, 0, unroll=DMA_UNROLL)
    moe = buf_ref[0] * rw_ref[:, 0:1] + buf_ref[1] * rw_ref[:, 1:2]
    o_ref[...] = _layer_norm(alpha * h_ref[...] + moe, g_ref[...], b_ref[...])


def _moe_combine(dest_flat, h, rw, g, b_, ypad, alpha):
    n, d = h.shape
    return pl.pallas_call(
        functools.partial(_combine_kernel, alpha=alpha),
        grid=(n // MOE_ROWS,),
        in_specs=[pl.BlockSpec((EXPERT_TOPK * MOE_ROWS,), lambda i: (i,), memory_space=pltpu.SMEM),
                  pl.BlockSpec((MOE_ROWS, d), lambda i: (i, 0)),
                  pl.BlockSpec((MOE_ROWS, LANE), lambda i: (i, 0)),
                  pl.BlockSpec((1, d), lambda i: (0, 0)),
                  pl.BlockSpec((1, d), lambda i: (0, 0)),
                  pl.BlockSpec(memory_space=pl.ANY)],
        out_specs=pl.BlockSpec((MOE_ROWS, d), lambda i: (i, 0)),
        out_shape=jax.ShapeDtypeStruct((n, d), F32),
        scratch_shapes=[pltpu.VMEM((EXPERT_TOPK, MOE_ROWS, d), F32), pltpu.SemaphoreType.DMA(())],
        compiler_params=_cparams(("arbitrary",), 48),
        name="moe_combine",
    )(dest_flat, h, rw, g, b_, ypad)


def _cmp_map_t(t):
    nc = (t - CMP_BLOCK) // CMP_STRIDE + 1
    ns = t // SLC_BLOCK
    ncp = t // CMP_STRIDE
    cs = CMP_STRIDE * np.arange(nc)[:, None]
    ss = SLC_BLOCK * np.arange(ns)[None, :]
    ov = np.minimum(cs + CMP_BLOCK, ss + SLC_BLOCK) - np.maximum(cs, ss)
    m = np.clip(ov, 0, None).astype(np.float32) / CMP_STRIDE
    out = np.zeros((ns, ncp), np.float32)
    out[:, :nc] = m.T
    return jnp.asarray(out, BF16)


def _block_expand(t):
    ns = t // SLC_BLOCK
    rows = max(ns, LANE)
    e = np.zeros((rows, t), np.float32)
    e[np.arange(t) // SLC_BLOCK, np.arange(t)] = 1.0
    return jnp.asarray(e, BF16)


def _pad_head_rows(w, n_heads):
    wh = w.reshape(n_heads, HEAD_DIM, w.shape[-1])
    return jnp.concatenate([wh, jnp.zeros_like(wh)], axis=1).reshape(n_heads * LANE, w.shape[-1]).astype(BF16)


def kernel(x, w_in, cmp_pe_k, cmp_pe_v, cmp_w1_k, cmp_w2_k, cmp_w1_v, cmp_w2_v, ckv_norm_g, w_uk, w_uv, rel_bias,
           w_branch_a, w_branch_b, w_out, ln1_g, ln1_b, w_grp, b_grp, w_rtr, b_rtr, w_gate, w_up, w_down, ln2_g,
           ln2_b):
    b, t, d = x.shape
    n = b * t
    depth = w_in.shape[0]
    alpha = (2.0 * depth) ** 0.25
    assert t % 512 == 0 and t >= WIN_KEYS and n % MOE_ROWS == 0

    sslc = _bias_strip(rel_bias, NSA_HEADS, 0, STRIP_A, STRIP_W, None, True)
    sdsa = _bias_strip(rel_bias, DSA_HEADS, NSA_HEADS, STRIP_A, STRIP_W, None, True)
    swin = _bias_strip(rel_bias, NSA_HEADS, 0, WIN_A, WIN_W, WINDOW, False)
    bcmp = _cmp_bias(rel_bias, t)
    mapt = _cmp_map_t(t)
    eall = _block_expand(t)

    n_a = n * EXPERT_TOPK
    n_blk = -(-n_a // EXPERT_BLOCK) + N_EXPERTS
    n_slots = n_blk * EXPERT_BLOCK

    h = x.reshape(n, d)
    for l in range(depth):
        w_pad = _proj_weights(w_in[l])
        wuk = w_uk[l]
        z = jnp.zeros_like(wuk[0])
        wuk_pairs = jnp.stack([
            jnp.concatenate([jnp.concatenate([wuk[2 * k], z], axis=1), jnp.concatenate([z, wuk[2 * k + 1]], axis=1)],
                            axis=0) for k in range(DSA_HEADS // 2)]).astype(BF16)
        wuv_pad = jnp.pad(w_uv[l], ((0, 0), (0, 0), (0, LANE - HEAD_DIM))).astype(BF16)
        wk = _cmp_weights(cmp_w1_k[l], cmp_w2_k[l], cmp_pe_k[l], HEAD_DIM)
        wv = _cmp_weights(cmp_w1_v[l], cmp_w2_v[l], cmp_pe_v[l], LANE)
        wa_pad = _pad_head_rows(w_branch_a[l], NSA_HEADS)
        wb_pad = _pad_head_rows(w_branch_b[l], DSA_HEADS)
        wr = jnp.zeros((d, LANE), F32).at[:, RT_GRP:RT_GRP + N_EXPERT_GROUPS].set(w_grp[l])
        wr = wr.at[:, RT_EXP:RT_EXP + N_EXPERTS].set(w_rtr[l])
        wr_hi = wr.astype(BF16)
        wr = jnp.concatenate([wr_hi, (wr - wr_hi.astype(F32)).astype(BF16)], axis=1)
        br = jnp.zeros((1, LANE), F32).at[0, RT_GRP:RT_GRP + N_EXPERT_GROUPS].set(b_grp[l])
        br = br.at[0, RT_EXP:RT_EXP + N_EXPERTS].set(b_rtr[l])

        (qa, kc, vc, kv, gn, qlat, ckvn, qidx, kidx, widx, sga, sgb) = _proj(
            h, w_pad, wuk_pairs, ckv_norm_g[l].reshape(1, KV_RANK))
        kcmp, vcmp = _compress(kc, vc, wk, wv, b, t)

        def b3(a):
            return a.reshape(b, t, a.shape[-1])

        oa = _nsa(b3(qa), b3(gn), kcmp, vcmp, b3(kv), sslc, swin, bcmp, mapt, eall, b, t)
        ob = _dsa(b3(qlat), b3(qidx), b3(widx), b3(kidx), b3(ckvn), sdsa, wuv_pad, b, t)

        h1, rt, rw = _post(h, oa.reshape(n, -1), ob.reshape(n, -1), sga, sgb, wa_pad, wb_pad, w_out[l].astype(BF16),
                           ln1_g[l].reshape(1, d), ln1_b[l].reshape(1, d), wr, br, alpha)

        rank, cnt = _moe_rank(rt)
        dest, bexp = _moe_dest(rt, rank, cnt, n_blk)
        dest_flat = dest[:, :EXPERT_TOPK].reshape(n_a)
        xpad = _moe_dispatch(dest_flat, h1, n_slots)
        ypad = _moe_experts(bexp[:n_blk, 0], bexp[:1, 1], xpad, w_gate[l], w_up[l], w_down[l])
        h = _moe_combine(dest_flat, h1, rw, ln2_g[l].reshape(1, d), ln2_b[l].reshape(1, d), ypad, alpha)
    return h.reshape(b, t, d)
```

```python
import functools
import math

import numpy as np
import jax
import jax.numpy as jnp
from jax import lax
from jax.experimental import pallas as pl
from jax.experimental.pallas import tpu as pltpu

F32 = jnp.float32
BF16 = jnp.bfloat16
I32 = jnp.int32

D_MODEL = 1024
HEAD_DIM = 64
NSA_HEADS = 8
NSA_GROUPS = 2
NSA_HPG = NSA_HEADS // NSA_GROUPS
CMP_BLOCK = 32
CMP_STRIDE = 16
CMP_HIDDEN = 256
SLC_BLOCK = 64
SLC_TOPN = 16
SLC_FORCE = 1e4
WINDOW = 512
DSA_HEADS = 8
KV_RANK = 128
IDX_HEADS = 4
IDX_DIM = 64
IDX_TOPK_MAX = 256
REL_BUCKETS = 32
REL_EXACT = 16
REL_MAX_DIST = 1024
N_EXPERT_GROUPS = 4
EXPERTS_PER_GROUP = 8
N_EXPERTS = N_EXPERT_GROUPS * EXPERTS_PER_GROUP
EXPERT_TOPK = 2
D_EXPERT = 256
EXPERT_BLOCK = 256
NSA_WIDTH = NSA_HEADS * HEAD_DIM
DSA_WIDTH = DSA_HEADS * HEAD_DIM
NEG = -1e30
SPLIT_SIZES = (NSA_WIDTH,) + (NSA_GROUPS * HEAD_DIM,) * 6 + (
    NSA_HEADS * 3, DSA_WIDTH, KV_RANK, IDX_HEADS * IDX_DIM, IDX_DIM, IDX_HEADS, D_MODEL, D_MODEL)
SPLIT_POINTS = tuple(int(v) for v in np.cumsum(SPLIT_SIZES)[:-1])

LANE = 128
QT = 128
KT = 512
TPK = KT // QT
RB = 64
M_FLOOR = -1e29
BIG = 3e38
LOG2E = math.log2(math.e)

FAR_TILES = 11
STRIP_A = FAR_TILES * QT
STRIP_W = STRIP_A + KT
assert REL_EXACT + int(math.log((STRIP_A - KT + 1) / REL_EXACT) / math.log(REL_MAX_DIST / REL_EXACT)
                       * (REL_BUCKETS - REL_EXACT)) >= REL_BUCKETS - 1
WIN_A = WINDOW
WIN_KEYS = WINDOW + QT
WIN_W = WIN_A + WIN_KEYS

O_QA, O_KC, O_VC, O_KV, O_GN, O_QB, O_CKV, O_QIDX, O_KIDX, O_WIDX, O_GA, O_GB, PROJ_W = (
    int(v) for v in np.cumsum([0, 1024, 128, 128, 768, 128, 512, 128, 512, 128, 128, 1024, 1024]))
KV_KS, KV_VS, KV_KW, KV_VW, KV_W = 0, 128, 384, 512, 768
ONES_LANE = HEAD_DIM

_NT = (((1,), (1,)), ((), ()))


def _dot(a, b):
    return jnp.dot(a, b, preferred_element_type=F32)


def _dot_nt(a, b):
    return lax.dot_general(a, b, _NT, preferred_element_type=F32)


def _cparams(sem, vmem_mb=None):
    kw = dict(dimension_semantics=sem)
    if vmem_mb is not None:
        kw["vmem_limit_bytes"] = vmem_mb * 1024 * 1024
    return pltpu.CompilerParams(**kw)


def _proj_weights(w_in):
    (q_a, kc, vc, ks, vs, kw, vw, g_nsa, q_b, ckv, q_idx, k_idx, w_idx, ga, gb) = jnp.split(w_in, SPLIT_POINTS, axis=1)
    d = w_in.shape[0]
    scale = HEAD_DIM ** -0.5 * LOG2E
    z64 = jnp.zeros((d, NSA_HPG, HEAD_DIM), F32)
    qa = (q_a * scale).reshape(d, NSA_GROUPS, NSA_HPG, HEAD_DIM)
    qa_pad = jnp.concatenate([
        jnp.concatenate([qa[:, 0], z64], axis=-1).reshape(d, NSA_HPG * LANE),
        jnp.concatenate([z64, qa[:, 1]], axis=-1).reshape(d, NSA_HPG * LANE)], axis=1)
    qi = q_idx.reshape(d, IDX_HEADS, IDX_DIM)
    qi_pad = jnp.concatenate([qi, jnp.zeros_like(qi)], axis=-1).reshape(d, IDX_HEADS * LANE)

    def pad(a):
        return jnp.pad(a, ((0, 0), (0, LANE - a.shape[1])))

    def per_group(v):
        return jnp.concatenate([pad(v[:, :HEAD_DIM]), pad(v[:, HEAD_DIM:])], axis=1)

    w_idx_s = w_idx * (IDX_HEADS ** -0.5 * IDX_DIM ** -0.5)
    cols = [qa_pad, kc, vc, ks, per_group(vs), kw, per_group(vw), pad(g_nsa), q_b * scale, ckv, qi_pad, pad(k_idx),
            pad(w_idx_s), ga, gb]
    w = jnp.concatenate(cols, axis=1)
    assert w.shape[1] == PROJ_W
    return w.astype(BF16)


def _proj_kernel(x_ref, w_ref, wuk_ref, g_ref, qa_o, kc_o, vc_o, kv_o, gn_o, qlat_o, ckv_o, qidx_o, kidx_o,
                 widx_o, sga_o, sgb_o):
    xb = x_ref[...].astype(BF16)

    def mm(lo, n):
        return _dot(xb, w_ref[:, lo:lo + n])

    qa_o[...] = mm(O_QA, 1024).astype(BF16)
    kc_o[...] = mm(O_KC, 128).astype(BF16)
    vc_o[...] = mm(O_VC, 128).astype(BF16)
    lane = lax.broadcasted_iota(I32, (x_ref.shape[0], KV_W), 1)
    is_one = ((lane & (LANE - 1)) == ONES_LANE) & (((lane >= KV_VS) & (lane < KV_KW)) | (lane >= KV_VW))
    kv_o[...] = jnp.where(is_one, 1.0, mm(O_KV, KV_W)).astype(BF16)
    gn_o[...] = jax.nn.sigmoid(mm(O_GN, 128))
    qb = mm(O_QB, 512).astype(BF16)
    for k in range(DSA_HEADS // 2):
        qlat_o[:, 256 * k:256 * (k + 1)] = _dot(qb[:, 128 * k:128 * (k + 1)], wuk_ref[k]).astype(BF16)
    c = mm(O_CKV, 128)
    ms = jnp.mean(c * c, axis=-1, keepdims=True)
    ckv_o[...] = (c * lax.rsqrt(ms + 1e-6) * g_ref[...]).astype(BF16)
    qidx_o[...] = mm(O_QIDX, 512).astype(BF16)
    kidx_o[...] = mm(O_KIDX, 128).astype(BF16)
    widx_o[...] = mm(O_WIDX, 128)
    sga_o[...] = jax.nn.sigmoid(mm(O_GA, 1024)).astype(BF16)
    sgb_o[...] = jax.nn.sigmoid(mm(O_GB, 1024)).astype(BF16)


def _proj(x2, w_pad, wuk_pairs, ckv_g, rows=512):
    n, d = x2.shape
    widths = [(1024, BF16), (128, BF16), (128, BF16), (KV_W, BF16), (128, F32), (1024, BF16), (128, BF16),
              (512, BF16), (128, BF16), (128, F32), (1024, BF16), (1024, BF16)]
    return pl.pallas_call(
        _proj_kernel,
        grid=(n // rows,),
        in_specs=[
            pl.BlockSpec((rows, d), lambda i: (i, 0)),
            pl.BlockSpec((d, PROJ_W), lambda i: (0, 0)),
            pl.BlockSpec((DSA_HEADS // 2, 128, 256), lambda i: (0, 0, 0)),
            pl.BlockSpec((1, KV_RANK), lambda i: (0, 0)),
        ],
        out_specs=[pl.BlockSpec((rows, w), lambda i: (i, 0)) for w, _ in widths],
        out_shape=[jax.ShapeDtypeStruct((n, w), dt) for w, dt in widths],
        compiler_params=_cparams(("arbitrary",), 56),
        name="proj",
    )(x2, w_pad, wuk_pairs, ckv_g)


def _cmp_weights(w1, w2, pe, out_lanes):
    half = CMP_BLOCK // 2
    w1r = w1.reshape(CMP_BLOCK, HEAD_DIM, CMP_HIDDEN)
    eye = jnp.eye(NSA_GROUPS, dtype=F32)

    def expand(wl):
        return jnp.einsum('ldj,gh->lgdhj', wl, eye).reshape(half * NSA_GROUPS * HEAD_DIM, NSA_GROUPS * CMP_HIDDEN)

    top, bot = expand(w1r[:half]), expand(w1r[half:])

    def pe_rows(p):
        return jnp.broadcast_to(p[:, None, :], (half, NSA_GROUPS, HEAD_DIM)).reshape(1, -1)

    w2p = w2 if out_lanes == HEAD_DIM else jnp.pad(w2, ((0, 0), (0, out_lanes - HEAD_DIM)))
    w2bd = jnp.einsum('jd,gh->gjhd', w2p, eye).reshape(NSA_GROUPS * CMP_HIDDEN, NSA_GROUPS * out_lanes)
    return (top.astype(BF16), bot.astype(BF16), pe_rows(pe[:half]).astype(BF16), pe_rows(pe[half:]).astype(BF16),
            w2bd.astype(BF16))


def _compress_kernel(hk_ref, hv_ref, kt_ref, kb_ref, kpt_ref, kpb_ref, k2_ref, vt_ref, vb_ref, vpt_ref, vpb_ref,
                     v2_ref, ko_ref, vo_ref):
    ncp = hk_ref.shape[0]

    def one(h_ref, top_ref, bot_ref, pt_ref, pb_ref, w2_ref, o_ref):
        h = h_ref[...]
        a = _dot(h, top_ref[...])
        b = _dot(h, bot_ref[...])
        pe8t = jnp.broadcast_to(pt_ref[...], (8, pt_ref.shape[1]))
        pe8b = jnp.broadcast_to(pb_ref[...], (8, pb_ref.shape[1]))
        pe_term = (_dot(pe8t, top_ref[...]) + _dot(pe8b, bot_ref[...]))[0:1]
        pre = a + pltpu.roll(b, ncp - 1, 0) + pe_term
        hid = jax.nn.gelu(pre, approximate=True)
        o_ref[...] = _dot(hid.astype(BF16), w2_ref[...]).astype(BF16)

    one(hk_ref, kt_ref, kb_ref, kpt_ref, kpb_ref, k2_ref, ko_ref)
    one(hv_ref, vt_ref, vb_ref, vpt_ref, vpb_ref, v2_ref, vo_ref)


def _compress(kc, vc, wk, wv, b, t):
    ncp = t // CMP_STRIDE
    hw = CMP_STRIDE * NSA_GROUPS * HEAD_DIM
    hk = kc.reshape(b, ncp, hw)
    hv = vc.reshape(b, ncp, hw)
    hspec = pl.BlockSpec((None, ncp, hw), lambda i: (i, 0, 0))

    def full(a):
        return pl.BlockSpec(a.shape, lambda i: (0,) * a.ndim)

    widths = (wk[-1].shape[1], wv[-1].shape[1])
    return pl.pallas_call(
        _compress_kernel,
        grid=(b,),
        in_specs=[hspec, hspec] + [full(a) for a in wk] + [full(a) for a in wv],
        out_specs=[pl.BlockSpec((None, ncp, w), lambda i: (i, 0, 0)) for w in widths],
        out_shape=[jax.ShapeDtypeStruct((b, ncp, w), BF16) for w in widths],
        compiler_params=_cparams(("arbitrary",), 48),
        name="compress",
    )(hk, hv, *wk, *wv)


def _rel_bucket(dist):
    n = jnp.maximum(dist, 0)
    nf = jnp.maximum(n, 1).astype(F32)
    large = REL_EXACT + (jnp.log(nf / REL_EXACT) / math.log(REL_MAX_DIST / REL_EXACT)
                         * (REL_BUCKETS - REL_EXACT)).astype(I32)
    return jnp.where(n < REL_EXACT, n, jnp.minimum(large, REL_BUCKETS - 1))


def _rel_lookup(tab_ref, bucket, col):
    bits = [(bucket & (1 << k)) != 0 for k in range(5)]
    vals = [jnp.where(bits[0], tab_ref[2 * k + 1, col], tab_ref[2 * k, col]) for k in range(REL_BUCKETS // 2)]
    for lvl in range(1, 5):
        vals = [jnp.where(bits[lvl], vals[2 * k + 1], vals[2 * k]) for k in range(len(vals) // 2)]
    return vals[0]


def _strip_kernel(tab_ref, o_ref, *, a, window, head0, rel_far):
    h = pl.program_id(0)
    shape = o_ref.shape
    r = lax.broadcasted_iota(I32, shape, 0)
    j = lax.broadcasted_iota(I32, shape, 1)
    dist = r + a - j
    valid = dist >= 0
    if window is not None:
        valid = valid & (dist < window)
    val = _rel_lookup(tab_ref, _rel_bucket(dist), h + head0)
    if rel_far:
        val = val - tab_ref[REL_BUCKETS - 1, h + head0]
    o_ref[...] = jnp.where(valid, val * LOG2E, NEG)


def _bias_strip(rel_bias, n_heads, head0, a, width, window, rel_far):
    return pl.pallas_call(
        functools.partial(_strip_kernel, a=a, window=window, head0=head0, rel_far=rel_far),
        grid=(n_heads,),
        in_specs=[pl.BlockSpec(memory_space=pltpu.SMEM)],
        out_specs=pl.BlockSpec((None, QT, width), lambda h: (h, 0, 0)),
        out_shape=jax.ShapeDtypeStruct((n_heads, QT, width), F32),
        compiler_params=_cparams(("arbitrary",)),
        name="bias_strip",
    )(rel_bias)


def _cmp_bias_kernel(tab_ref, o_ref, *, nc):
    i = pl.program_id(0)
    h = pl.program_id(1)
    shape = o_ref.shape
    r = lax.broadcasted_iota(I32, shape, 0)
    c = lax.broadcasted_iota(I32, shape, 1)
    dist = i * QT + r - (CMP_STRIDE * c + CMP_BLOCK - 1)
    valid = (dist >= 0) & (c < nc)
    val = _rel_lookup(tab_ref, _rel_bucket(dist), h)
    o_ref[...] = jnp.where(valid, val * LOG2E, NEG)


def _cmp_bias(rel_bias, t):
    n_qt = t // QT
    ncp = t // CMP_STRIDE
    nc = (t - CMP_BLOCK) // CMP_STRIDE + 1
    return pl.pallas_call(
        functools.partial(_cmp_bias_kernel, nc=nc),
        grid=(n_qt, NSA_HEADS),
        in_specs=[pl.BlockSpec(memory_space=pltpu.SMEM)],
        out_specs=pl.BlockSpec((None, None, QT, ncp), lambda i, h: (i, h, 0, 0)),
        out_shape=jax.ShapeDtypeStruct((n_qt, NSA_HEADS, QT, ncp), F32),
        compiler_params=_cparams(("arbitrary", "arbitrary")),
        name="cmp_bias",
    )(rel_bias)


def _lane_groups(op, x):
    parts = [x[:, c * LANE:(c + 1) * LANE] for c in range(x.shape[1] // LANE)]
    while len(parts) > 1:
        parts = [op(parts[j], parts[j + 1]) for j in range(0, len(parts), 2)]
    return parts[0]


def _far_tiles(i):
    return jnp.maximum((i - (FAR_TILES - TPK)) >> (TPK.bit_length() - 1), 0)


def _strip_offset(i, kt):
    return pl.multiple_of(QT * jnp.maximum(FAR_TILES - (i - TPK * kt), 0), LANE)


def _for_tiles(lo, hi, step):
    n = hi - lo

    def pair(j, c):
        step(lo + 2 * j)
        step(lo + 2 * j + 1)
        return c

    lax.fori_loop(0, n >> 1, pair, 0)

    @pl.when((n & 1) == 1)
    def _():
        step(hi - 1)


def _nsa_kernel(qa_ref, gn_ref, kc_ref, vc_ref, kv_ref, sslc_ref, swin_ref, bcmp_ref, mapt_ref, eall_ref, o_ref,
                q_ref, s_ref, z_ref, p_ref, m_ref, acc_ref, ocmp_ref, sel_ref, st_ref, *, t):
    i = pl.program_id(1)
    q0 = i * QT
    ns = t // SLC_BLOCK
    ncp = t // CMP_STRIDE
    n_top = min(SLC_TOPN, ns)
    grows = NSA_HPG * QT
    arows = NSA_HEADS * QT
    shift = SLC_BLOCK.bit_length() - 1
    halves = QT // RB

    for h in range(NSA_HEADS):
        q_ref[h * QT:(h + 1) * QT, :] = qa_ref[:, h * LANE:(h + 1) * LANE]

    s_ref[:, 0:ncp] = _dot_nt(q_ref[...], kc_ref[...])
    imp = []
    for g in range(NSA_GROUPS):
        psum = None
        for h in range(NSA_HPG * g, NSA_HPG * (g + 1)):
            rows = slice(h * QT, (h + 1) * QT)
            z = s_ref[rows, 0:ncp] + bcmp_ref[h]
            m = jnp.maximum(jnp.max(z, axis=-1, keepdims=True), M_FLOOR)
            e = jnp.exp2(z - m)
            p = e / jnp.maximum(jnp.sum(e, axis=-1, keepdims=True), 1e-30)
            p_ref[rows, 0:ncp] = p.astype(BF16)
            psum = p if psum is None else psum + p
        grp = slice(g * grows, (g + 1) * grows)
        ocmp_ref[grp, :] = _dot(p_ref[grp, 0:ncp], vc_ref[:, g * LANE:(g + 1) * LANE])
        imp.append(_dot_nt(mapt_ref[...], psum.astype(BF16)))
    imp_t = jnp.concatenate(imp, axis=1)

    blk = lax.broadcasted_iota(I32, (ns, NSA_GROUPS * QT), 0)
    tq = q0 + (lax.broadcasted_iota(I32, (ns, NSA_GROUPS * QT), 1) & (QT - 1))
    cur = tq >> shift
    forced = (blk == 0) | (blk == cur) | (blk == cur - 1)
    avail = (blk << shift) <= tq
    score = jnp.where(avail, imp_t + jnp.where(forced, SLC_FORCE, 0.0), NEG)
    st_ref[0:ns, :] = score

    sub8 = 8
    groups = [score[v * sub8:(v + 1) * sub8] for v in range(ns // sub8)]
    blk8 = lax.broadcasted_iota(I32, (sub8, NSA_GROUPS * QT), 0)
    ranks = [jnp.zeros((sub8, NSA_GROUPS * QT), F32) for _ in groups]
    for jp in range(ns):
        rowb = jnp.broadcast_to(st_ref[jp:jp + 1, :], (sub8, NSA_GROUPS * QT))
        for v, sv in enumerate(groups):
            ge = jnp.where(rowb >= sv, 1.0, 0.0)
            gt = jnp.where(rowb > sv, 1.0, 0.0)
            if v * sub8 > jp:
                inc = ge
            elif v * sub8 + sub8 - 1 <= jp:
                inc = gt
            else:
                inc = jnp.where(blk8 + v * sub8 > jp, ge, gt)
            ranks[v] = ranks[v] + inc
    rank = jnp.concatenate(ranks, axis=0)
    sel_t = jnp.where((rank < n_top) & avail, 1.0, 0.0)
    if ns < LANE:
        sel_t = jnp.concatenate([sel_t, jnp.zeros((LANE - ns, NSA_GROUPS * QT), F32)], axis=0)
    for g in range(NSA_GROUPS):
        sel_ref[g] = sel_t[:, g * QT:(g + 1) * QT].T.astype(BF16)

    m_ref[...] = jnp.full((arows, LANE), M_FLOOR, F32)
    acc_ref[...] = jnp.zeros((arows, LANE), F32)

    def max_step(kt, near):
        k0 = pl.multiple_of(kt * KT, KT)
        s = _dot_nt(q_ref[...], kv_ref[pl.ds(k0, KT), KV_KS:KV_KS + LANE])
        for g in range(NSA_GROUPS):
            madd = (_dot(sel_ref[g], eall_ref[:, pl.ds(k0, KT)]) - 1.0) * (-NEG)
            for h in range(NSA_HPG * g, NSA_HPG * (g + 1)):
                for half in range(halves):
                    qrows = slice(half * RB, (half + 1) * RB)
                    rows = slice(h * QT + half * RB, h * QT + (half + 1) * RB)
                    z = s[rows] + madd[qrows]
                    if near:
                        z = z + sslc_ref[h, qrows, pl.ds(_strip_offset(i, kt), KT)]
                    z_ref[rows, pl.ds(k0, KT)] = z
                    m_ref[rows, :] = jnp.maximum(m_ref[rows, :], _lane_groups(jnp.maximum, z))

    n_far = _far_tiles(i)
    n_kt = (i >> (TPK.bit_length() - 1)) + 1
    _for_tiles(0, n_far, lambda kt: max_step(kt, False))
    _for_tiles(n_far, n_kt, lambda kt: max_step(kt, True))
    m_ref[...] = jnp.broadcast_to(jnp.max(m_ref[...], axis=-1, keepdims=True), (arows, LANE))

    def sum_step(kt):
        k0 = pl.multiple_of(kt * KT, KT)
        for g in range(NSA_GROUPS):
            ps = []
            for h in range(NSA_HPG * g, NSA_HPG * (g + 1)):
                for half in range(halves):
                    rows = slice(h * QT + half * RB, h * QT + (half + 1) * RB)
                    m = m_ref[rows, :]
                    z = z_ref[rows, pl.ds(k0, KT)]
                    ps.append(jnp.exp2(z - jnp.concatenate([m] * (KT // LANE), axis=1)).astype(BF16))
            grp = slice(g * grows, (g + 1) * grows)
            acc_ref[grp, :] += _dot(jnp.concatenate(ps, axis=0),
                                    kv_ref[pl.ds(k0, KT), KV_VS + g * LANE:KV_VS + (g + 1) * LANE])

    _for_tiles(0, n_kt, sum_step)

    ks0 = pl.multiple_of(jnp.maximum(i - WINDOW // QT, 0) * QT, QT)
    woff = pl.multiple_of(jnp.maximum(WINDOW // QT - i, 0) * QT, LANE)
    s_ref[...] = _dot_nt(q_ref[...], kv_ref[pl.ds(ks0, WIN_KEYS), KV_KW:KV_KW + LANE])
    for h in range(NSA_HEADS):
        for half in range(halves):
            qrows = slice(half * RB, (half + 1) * RB)
            rows = slice(h * QT + half * RB, h * QT + (half + 1) * RB)
            z = s_ref[rows, :] + swin_ref[h, qrows, pl.ds(woff, WIN_KEYS)]
            p_ref[rows, :] = jnp.exp2(z - jnp.max(z, axis=-1, keepdims=True)).astype(BF16)

    lane_ok = lax.broadcasted_iota(I32, (QT, LANE), 1) < HEAD_DIM
    for g in range(NSA_GROUPS):
        grp = slice(g * grows, (g + 1) * grows)
        o_win = _dot(p_ref[grp, :], kv_ref[pl.ds(ks0, WIN_KEYS), KV_VW + g * LANE:KV_VW + (g + 1) * LANE])
        for hp in range(NSA_HPG):
            h = NSA_HPG * g + hp
            rows = slice(h * QT, (h + 1) * QT)
            wrows = slice(hp * QT, (hp + 1) * QT)
            slc = acc_ref[rows, :]
            o_slc = slc / jnp.maximum(slc[:, ONES_LANE:ONES_LANE + 1], 1e-30)
            win = o_win[wrows]
            o_w = win / jnp.maximum(win[:, ONES_LANE:ONES_LANE + 1], 1e-30)
            o = (gn_ref[:, 3 * h:3 * h + 1] * ocmp_ref[rows, :] + gn_ref[:, 3 * h + 1:3 * h + 2] * o_slc
                 + gn_ref[:, 3 * h + 2:3 * h + 3] * o_w)
            o_ref[:, h * LANE:(h + 1) * LANE] = jnp.where(lane_ok, o, 0.0).astype(BF16)


def _nsa(qa, gn, kcmp, vcmp, kv, sslc, swin, bcmp, mapt, eall, b, t):
    n_qt = t // QT
    ncp = t // CMP_STRIDE
    ns = t // SLC_BLOCK
    arows = NSA_HEADS * QT
    assert ncp <= WIN_KEYS

    def full(a):
        return pl.BlockSpec(a.shape, lambda bi, i: (0,) * a.ndim)

    def once(a):
        return pl.BlockSpec(a.shape, lambda bi, i: (0,) * a.ndim, pipeline_mode=pl.Buffered(1))

    return pl.pallas_call(
        functools.partial(_nsa_kernel, t=t),
        grid=(b, n_qt),
        in_specs=[
            pl.BlockSpec((None, QT, NSA_HEADS * LANE), lambda bi, i: (bi, i, 0)),
            pl.BlockSpec((None, QT, LANE), lambda bi, i: (bi, i, 0)),
            pl.BlockSpec((None, ncp, LANE), lambda bi, i: (bi, 0, 0)),
            pl.BlockSpec((None, ncp, NSA_GROUPS * LANE), lambda bi, i: (bi, 0, 0)),
            pl.BlockSpec((None, t, KV_W), lambda bi, i: (bi, 0, 0), pipeline_mode=pl.Buffered(1)),
            once(sslc), once(swin),
            pl.BlockSpec((None, NSA_HEADS, QT, ncp), lambda bi, i: (i, 0, 0, 0)),
            full(mapt), once(eall),
        ],
        out_specs=pl.BlockSpec((None, QT, NSA_HEADS * LANE), lambda bi, i: (bi, i, 0)),
        out_shape=jax.ShapeDtypeStruct((b, t, NSA_HEADS * LANE), BF16),
        scratch_shapes=[
            pltpu.VMEM((arows, LANE), BF16),
            pltpu.VMEM((arows, WIN_KEYS), F32),
            pltpu.VMEM((arows, t), F32),
            pltpu.VMEM((arows, WIN_KEYS), BF16),
            pltpu.VMEM((arows, LANE), F32),
            pltpu.VMEM((arows, LANE), F32),
            pltpu.VMEM((arows, LANE), F32),
            pltpu.VMEM((NSA_GROUPS, QT, LANE), BF16),
            pltpu.VMEM((max(ns, 8), NSA_GROUPS * QT), F32),
        ],
        compiler_params=_cparams(("arbitrary", "arbitrary"), 56),
        name="nsa",
    )(qa, gn, kcmp, vcmp, kv, sslc, swin, bcmp, mapt, eall)


IDX_CHUNK = 512
FAST_PASSES = 10
LANE_SHIFT = LANE.bit_length() - 1


def _dsa_kernel(ql_ref, qi_ref, wi_ref, ki_ref, ckv_ref, strip_ref, wuv_ref, tri_ref, o_ref,
                q_ref, idx_ref, tmp_ref, s_ref, p_ref, m_ref, l_ref, acc_ref, *, t, k_sel):
    i = pl.program_id(1)
    q0 = i * QT
    nch = (i >> 2) + 1
    hrows = DSA_HEADS * QT
    sub = IDX_CHUNK // LANE
    halves = QT // RB

    for h in range(IDX_HEADS):
        q_ref[h * QT:(h + 1) * QT, :] = qi_ref[:, h * LANE:(h + 1) * LANE]
    tq = q0 + lax.broadcasted_iota(I32, (QT, IDX_CHUNK), 0)
    col = lax.broadcasted_iota(I32, (QT, IDX_CHUNK), 1)

    def idx_body(c, carry):
        c0 = pl.multiple_of(c * IDX_CHUNK, IDX_CHUNK)
        d = jnp.maximum(_dot_nt(q_ref[0:IDX_HEADS * QT, :], ki_ref[pl.ds(c0, IDX_CHUNK), :]), 0.0)
        acc = d[0:QT] * wi_ref[:, 0:1]
        for h in range(1, IDX_HEADS):
            acc = acc + d[h * QT:(h + 1) * QT] * wi_ref[:, h:h + 1]
        idx_ref[:, pl.ds(c0, IDX_CHUNK)] = jnp.where(col + c0 <= tq, acc, NEG)
        return carry

    lax.fori_loop(0, nch, idx_body, 0)

    zeros = jnp.zeros((QT, LANE), F32)

    def scan(fn, init):
        def body(c, carry):
            for s in range(sub):
                off = pl.multiple_of(c * IDX_CHUNK + s * LANE, LANE)
                carry = fn(idx_ref[:, pl.ds(off, LANE)], off, carry)
            return carry
        return lax.fori_loop(0, nch, body, init)

    def search(_):
        kf = float(k_sel)
        big = jnp.full((QT, LANE), BIG, F32)

        def count_gt(pivot):
            pb = jnp.broadcast_to(pivot, (QT, LANE))
            cnt = scan(lambda x, off, c: c + jnp.where(x > pb, 1.0, 0.0), zeros)
            return jnp.sum(cnt, axis=-1, keepdims=True)

        def init_fn(x, off, carry):
            lo, hi = carry
            return jnp.minimum(lo, jnp.where(x > 0.5 * NEG, x, BIG)), jnp.maximum(hi, x)

        row_min, row_max = scan(init_fn, (big, -big))
        row_min = jnp.min(row_min, axis=-1, keepdims=True)
        row_max = jnp.max(row_max, axis=-1, keepdims=True)
        n_valid = (q0 + 1 + lax.broadcasted_iota(I32, (QT, 1), 0)).astype(F32)

        def fast_step(_, c):
            lo, hi, g_lo, g_hi, f_lo, f_hi, side, done_i = c
            done = done_i > 0
            mid = lo + (hi - lo) * (g_lo / (g_lo - g_hi))
            mid = jnp.where((mid > lo) & (mid < hi), mid, lo + (hi - lo) * 0.5)
            cnt = count_gt(mid)
            hit = cnt == kf
            up = cnt > kf
            move_lo = up | hit
            move_hi = jnp.logical_not(up)
            g_lo_n = jnp.where(up, cnt - kf, jnp.where(side == 2, g_lo * 0.5, g_lo))
            g_hi_n = jnp.where(up, jnp.where(side == 1, g_hi * 0.5, g_hi), cnt - kf)
            keep_old = done | hit
            return (jnp.where(done, lo, jnp.where(move_lo, mid, lo)), jnp.where(done, hi, jnp.where(move_hi, mid, hi)),
                    jnp.where(keep_old, g_lo, g_lo_n), jnp.where(keep_old, g_hi, g_hi_n),
                    jnp.where(done | jnp.logical_not(up), f_lo, cnt), jnp.where(done | up | hit, f_hi, cnt),
                    jnp.where(up, 1, 2), jnp.where(done | hit, 1, 0))

        lo0 = row_min - (jnp.abs(row_min) + 1.0)
        init = (lo0, row_max, n_valid - kf, jnp.full((QT, 1), -kf, F32), n_valid, jnp.zeros((QT, 1), F32),
                jnp.zeros((QT, 1), I32), jnp.zeros((QT, 1), I32))
        lo, hi, _, _, f_lo, f_hi, _, done_i = lax.fori_loop(0, FAST_PASSES, fast_step, init)
        done = done_i > 0

        def exact(_):
            lob = jnp.broadcast_to(lo, (QT, LANE))
            hib = jnp.broadcast_to(hi, (QT, LANE))

            def snap_fn(x, off, c):
                a, b_ = c
                return (jnp.minimum(a, jnp.where(x > lob, x, BIG)), jnp.maximum(b_, jnp.where(x > hib, -BIG, x)))

            a, b_ = scan(snap_fn, (big, -big))
            lo_d = jnp.where(done, lo, jnp.min(a, axis=-1, keepdims=True))
            hi_d = jnp.where(done, hi, jnp.max(b_, axis=-1, keepdims=True))

            def cond(c):
                return jnp.max(jnp.where(c[0] < c[1], 1, 0)) > 0

            def step(c):
                lo, hi, c_lo, c_hi = c
                mid = lo + (hi - lo) * 0.5
                mid = jnp.where(mid < hi, mid, lo)
                midb = jnp.broadcast_to(mid, (QT, LANE))

                def fn(x, off, cc):
                    cnt, amin, bmax = cc
                    gt = x > midb
                    return (cnt + jnp.where(gt, 1.0, 0.0), jnp.minimum(amin, jnp.where(gt, x, BIG)),
                            jnp.maximum(bmax, jnp.where(gt, -BIG, x)))

                cnt, amin, bmax = scan(fn, (zeros, big, -big))
                cnt = jnp.sum(cnt, axis=-1, keepdims=True)
                amin = jnp.min(amin, axis=-1, keepdims=True)
                bmax = jnp.max(bmax, axis=-1, keepdims=True)
                up = cnt >= kf
                same = lo >= hi
                return (jnp.where(same | jnp.logical_not(up), lo, amin), jnp.where(same | up, hi, bmax),
                        jnp.where(same | jnp.logical_not(up), c_lo, cnt), jnp.where(same | up, c_hi, cnt))

            thr, _, c_ge, c_gt = lax.while_loop(cond, step, (lo_d, hi_d, f_lo, f_hi))
            return thr, c_ge, c_gt

        all_done = jnp.min(done_i) > 0
        thr, c_ge, c_gt = lax.cond(all_done, lambda _: (lo, f_lo, f_hi), exact, 0)
        c_ge = jnp.where(done, kf, c_ge)
        c_gt = jnp.where(done, kf, c_gt)
        thrb = jnp.broadcast_to(thr, (QT, LANE))
        need = k_sel - c_gt

        def tie_search(_):
            lane = lax.broadcasted_iota(I32, (QT, LANE), 1)
            lanef = lane.astype(F32)

            def grp_fn(x, off, g_cnt):
                c = jnp.sum(jnp.where(x == thrb, 1.0, 0.0), axis=-1, keepdims=True)
                return jnp.where(lane == (off >> LANE_SHIFT), c, g_cnt)

            g_cum = _dot(scan(grp_fn, zeros).astype(BF16), tri_ref[...])
            g_star = jnp.sum(jnp.where(g_cum < need, 1.0, 0.0), axis=-1, keepdims=True)
            before = jnp.sum(jnp.where(lanef == g_star - 1.0, g_cum, 0.0), axis=-1, keepdims=True)

            def slab_fn(x, off, slab):
                return jnp.where(g_star == (off >> LANE_SHIFT).astype(F32), x, slab)

            slab = scan(slab_fn, jnp.full((QT, LANE), NEG, F32))
            pre = _dot(jnp.where(slab == thrb, 1.0, 0.0).astype(BF16), tri_ref[...])
            lane_cut = jnp.sum(jnp.where(pre < need - before, 1.0, 0.0), axis=-1, keepdims=True)
            return (g_star * LANE + lane_cut).astype(I32)

        any_tie = jnp.max(jnp.where(c_ge > k_sel, 1, 0)) > 0
        p_cut = lax.cond(any_tie, tie_search, lambda _: jnp.full((QT, 1), t, I32), 0)
        return thr, jnp.where(done, -1, p_cut)

    thr, p_cut = lax.cond(q0 >= k_sel, search,
                          lambda _: (jnp.full((QT, 1), M_FLOOR, F32), jnp.full((QT, 1), t, I32)), 0)
    thrk = jnp.broadcast_to(thr, (QT, KT))
    pcutk = jnp.broadcast_to(p_cut, (QT, KT))
    colk = lax.broadcasted_iota(I32, (QT, KT), 1)

    for h in range(DSA_HEADS):
        q_ref[h * QT:(h + 1) * QT, :] = ql_ref[:, h * LANE:(h + 1) * LANE]
    m_ref[...] = jnp.full((hrows, LANE), M_FLOOR, F32)
    l_ref[...] = jnp.zeros((hrows, LANE), F32)
    acc_ref[...] = jnp.zeros((hrows, LANE), F32)
    blocks = [(h, half) for h in range(DSA_HEADS) for half in range(halves)]

    n_far = _far_tiles(i)
    n_kt = (i >> (TPK.bit_length() - 1)) + 1

    def max_step(kt, near):
        k0 = pl.multiple_of(kt * KT, KT)
        s = _dot_nt(q_ref[...], ckv_ref[pl.ds(k0, KT), :])
        x = idx_ref[:, pl.ds(k0, KT)]
        keep = (x > thrk) | ((x == thrk) & (colk + k0 <= pcutk))
        selm = jnp.where(keep, 0.0, NEG)
        for h, half in blocks:
            qrows = slice(half * RB, (half + 1) * RB)
            rows = slice(h * QT + half * RB, h * QT + (half + 1) * RB)
            z = s[rows] + selm[qrows]
            if near:
                z = z + strip_ref[h, qrows, pl.ds(_strip_offset(i, kt), KT)]
            s_ref[rows, pl.ds(k0, KT)] = z
            m_ref[rows, :] = jnp.maximum(m_ref[rows, :], _lane_groups(jnp.maximum, z))

    _for_tiles(0, n_far, lambda kt: max_step(kt, False))
    _for_tiles(n_far, n_kt, lambda kt: max_step(kt, True))
    m_ref[...] = jnp.broadcast_to(jnp.max(m_ref[...], axis=-1, keepdims=True), (hrows, LANE))

    def sum_step(kt):
        k0 = pl.multiple_of(kt * KT, KT)
        ps = []
        for h, half in blocks:
            rows = slice(h * QT + half * RB, h * QT + (half + 1) * RB)
            m = m_ref[rows, :]
            p = jnp.exp2(s_ref[rows, pl.ds(k0, KT)] - jnp.concatenate([m] * (KT // LANE), axis=1))
            l_ref[rows, :] += _lane_groups(jnp.add, p)
            ps.append(p.astype(BF16))
        acc_ref[...] += _dot(jnp.concatenate(ps, axis=0), ckv_ref[pl.ds(k0, KT), :])

    _for_tiles(0, n_kt, sum_step)
    for h in range(DSA_HEADS):
        rows = slice(h * QT, (h + 1) * QT)
        l = jnp.sum(l_ref[rows, :], axis=-1, keepdims=True)
        o_lat = (acc_ref[rows, :] / jnp.maximum(l, 1e-30)).astype(BF16)
        o_ref[:, h * LANE:(h + 1) * LANE] = _dot(o_lat, wuv_ref[h]).astype(BF16)


def _dsa(qlat, qidx, widx, kidx, ckvn, strip, wuv_pad, b, t):
    n_qt = t // QT
    k_sel = min(IDX_TOPK_MAX, t // 4)
    assert k_sel % QT == 0 and t % IDX_CHUNK == 0 and t // LANE <= LANE
    hrows = DSA_HEADS * QT
    tri = jnp.asarray(np.triu(np.ones((LANE, LANE), np.float32)), BF16)

    def full(a):
        return pl.BlockSpec(a.shape, lambda bi, i: (0,) * a.ndim)

    return pl.pallas_call(
        functools.partial(_dsa_kernel, t=t, k_sel=k_sel),
        grid=(b, n_qt),
        in_specs=[
            pl.BlockSpec((None, QT, DSA_HEADS * LANE), lambda bi, i: (bi, i, 0)),
            pl.BlockSpec((None, QT, IDX_HEADS * LANE), lambda bi, i: (bi, i, 0)),
            pl.BlockSpec((None, QT, LANE), lambda bi, i: (bi, i, 0)),
            pl.BlockSpec((None, t, LANE), lambda bi, i: (bi, 0, 0)),
            pl.BlockSpec((None, t, LANE), lambda bi, i: (bi, 0, 0)),
            pl.BlockSpec(strip.shape, lambda bi, i: (0, 0, 0), pipeline_mode=pl.Buffered(1)),
            full(wuv_pad), full(tri),
        ],
        out_specs=pl.BlockSpec((None, QT, DSA_HEADS * LANE), lambda bi, i: (bi, i, 0)),
        out_shape=jax.ShapeDtypeStruct((b, t, DSA_HEADS * LANE), BF16),
        scratch_shapes=[
            pltpu.VMEM((hrows, LANE), BF16),
            pltpu.VMEM((QT, t), F32),
            pltpu.VMEM((hrows, KT), F32),
            pltpu.VMEM((hrows, t), F32),
            pltpu.VMEM((hrows, KT), BF16),
            pltpu.VMEM((hrows, LANE), F32),
            pltpu.VMEM((hrows, LANE), F32),
            pltpu.VMEM((hrows, LANE), F32),
        ],
        compiler_params=_cparams(("arbitrary", "arbitrary"), 56),
        name="dsa",
    )(qlat, qidx, widx, kidx, ckvn, strip, wuv_pad, tri)


RT_GRP = 0
RT_EXP = 32


def _layer_norm(y, g, b):
    mu = jnp.mean(y, axis=-1, keepdims=True)
    yc = y - mu
    var = jnp.mean(yc * yc, axis=-1, keepdims=True)
    return yc * lax.rsqrt(var + 1e-5) * g + b


def _post_kernel(x_ref, oa_ref, ob_ref, sga_ref, sgb_ref, wa_ref, wb_ref, wo_ref, g_ref, b_ref, wr_ref, br_ref,
                 h_ref, rt_ref, rw_ref, *, alpha):
    rows = x_ref.shape[0]
    merged = (sga_ref[...].astype(F32) * _dot(oa_ref[...], wa_ref[...])
              + sgb_ref[...].astype(F32) * _dot(ob_ref[...], wb_ref[...]))
    y = alpha * x_ref[...] + _dot(merged.astype(BF16), wo_ref[...])
    h = _layer_norm(y, g_ref[...], b_ref[...])
    h_ref[...] = h

    h_hi = h.astype(BF16)
    h_lo = (h - h_hi.astype(F32)).astype(BF16)
    zz = _dot(h_hi, wr_ref[...])
    z = zz[:, 0:LANE] + zz[:, LANE:2 * LANE] + _dot(h_lo, wr_ref[:, 0:LANE]) + br_ref[...]
    lane = lax.broadcasted_iota(I32, (rows, LANE), 1)
    is_g = lane < N_EXPERT_GROUPS
    zg = jnp.where(is_g, z, -BIG)
    gmax = jnp.max(zg, axis=-1, keepdims=True)
    g_sel = jnp.min(jnp.where(is_g & (z == gmax), lane, LANE), axis=-1, keepdims=True)
    p_grp = 1.0 / jnp.sum(jnp.where(is_g, jnp.exp(zg - gmax), 0.0), axis=-1, keepdims=True)
    in_grp = (lane >= RT_EXP) & (lane < RT_EXP + N_EXPERTS) & (((lane - RT_EXP) >> 3) == g_sel)
    ze = jnp.where(in_grp, z, -BIG)
    m1 = jnp.max(ze, axis=-1, keepdims=True)
    i1 = jnp.min(jnp.where(in_grp & (z == m1), lane, LANE), axis=-1, keepdims=True)
    ze2 = jnp.where(lane == i1, -BIG, ze)
    m2 = jnp.max(ze2, axis=-1, keepdims=True)
    i2 = jnp.min(jnp.where(in_grp & (lane != i1) & (z == m2), lane, LANE), axis=-1, keepdims=True)
    e21 = jnp.exp(m2 - m1)
    den = 1.0 + e21
    w1 = p_grp * (1.0 / den)
    w2 = p_grp * (e21 / den)
    rt_ref[...] = jnp.where(lane == 0, i1 - RT_EXP, jnp.where(lane == 1, i2 - RT_EXP, 0))
    rw_ref[...] = jnp.where(lane == 0, w1, jnp.where(lane == 1, w2, 0.0))


def _post(x2, oa, ob, sga, sgb, wa, wb, wo, g, b_, wr, br, alpha, rows=256):
    n, d = x2.shape

    def row(w):
        return pl.BlockSpec((rows, w), lambda i: (i, 0))

    def full(a):
        return pl.BlockSpec(a.shape, lambda i: (0,) * a.ndim)

    return pl.pallas_call(
        functools.partial(_post_kernel, alpha=alpha),
        grid=(n // rows,),
        in_specs=[row(d), row(1024), row(1024), row(1024), row(1024), full(wa), full(wb), full(wo), full(g),
                  full(b_), full(wr), full(br)],
        out_specs=[row(d), row(LANE), row(LANE)],
        out_shape=[jax.ShapeDtypeStruct((n, d), F32), jax.ShapeDtypeStruct((n, LANE), I32),
                   jax.ShapeDtypeStruct((n, LANE), F32)],
        compiler_params=_cparams(("arbitrary",), 48),
        name="post",
    )(x2, oa, ob, sga, sgb, wa, wb, wo, g, b_, wr, br)


def _onehots(rt_ref, rows):
    lane = lax.broadcasted_iota(I32, (rows, LANE), 1)
    oh0 = jnp.where(lane == rt_ref[:, 0:1], 1.0, 0.0)
    oh1 = jnp.where(lane == rt_ref[:, 1:2], 1.0, 0.0)
    return oh0, oh1


def _rank_kernel(rt_ref, tri_ref, rank_ref, cnt_ref, carry_ref):
    rows = rt_ref.shape[0]

    @pl.when(pl.program_id(0) == 0)
    def _():
        carry_ref[...] = jnp.zeros_like(carry_ref)

    oh0, oh1 = _onehots(rt_ref, rows)
    both = oh0 + oh1
    before = _dot(tri_ref[...], both.astype(BF16)) + carry_ref[0:1, :]
    r0 = jnp.sum(oh0 * before, axis=-1, keepdims=True)
    r1 = jnp.sum(oh1 * before, axis=-1, keepdims=True)
    lane = lax.broadcasted_iota(I32, (rows, LANE), 1)
    rank_ref[...] = jnp.where(lane == 0, r0, jnp.where(lane == 1, r1, 0.0))
    carry_ref[...] = carry_ref[...] + jnp.sum(both, axis=0, keepdims=True)
    cnt_ref[...] = carry_ref[...]


def _moe_rank(rt, rows=512):
    n = rt.shape[0]
    tri = jnp.asarray(np.tril(np.ones((rows, rows), np.float32), -1), BF16)
    return pl.pallas_call(
        _rank_kernel,
        grid=(n // rows,),
        in_specs=[pl.BlockSpec((rows, LANE), lambda i: (i, 0)), pl.BlockSpec((rows, rows), lambda i: (0, 0))],
        out_specs=[pl.BlockSpec((rows, LANE), lambda i: (i, 0)), pl.BlockSpec((8, LANE), lambda i: (0, 0))],
        out_shape=[jax.ShapeDtypeStruct((n, LANE), F32), jax.ShapeDtypeStruct((8, LANE), F32)],
        scratch_shapes=[pltpu.VMEM((8, LANE), F32)],
        compiler_params=_cparams(("arbitrary",)),
        name="moe_rank",
    )(rt, tri)


def _lane_cumsum(v):
    lane = lax.broadcasted_iota(I32, v.shape, 1)
    s = 1
    while s < LANE:
        v = v + jnp.where(lane >= s, pltpu.roll(v, s, 1), 0.0)
        s *= 2
    return v


def _dest_kernel(rt_ref, rank_ref, cnt_ref, dest_ref, bexp_ref, *, n_blk_pad):
    rows = rt_ref.shape[0]
    lane8 = lax.broadcasted_iota(I32, (8, LANE), 1)
    cnt = jnp.where(lane8 < N_EXPERTS, cnt_ref[...], 0.0)
    padded = jnp.floor((cnt + (EXPERT_BLOCK - 1)) * (1.0 / EXPERT_BLOCK)) * EXPERT_BLOCK
    pend = _lane_cumsum(padded)
    poff = (pend - padded)[0:1, :]
    oh0, oh1 = _onehots(rt_ref, rows)
    d0 = jnp.sum(oh0 * poff, axis=-1, keepdims=True) + rank_ref[:, 0:1]
    d1 = jnp.sum(oh1 * poff, axis=-1, keepdims=True) + rank_ref[:, 1:2]
    lane = lax.broadcasted_iota(I32, (rows, LANE), 1)
    dest_ref[...] = jnp.where(lane == 0, d0, jnp.where(lane == 1, d1, 0.0)).astype(I32)

    lane_b = lax.broadcasted_iota(I32, (n_blk_pad, LANE), 1)
    start = (lax.broadcasted_iota(I32, (n_blk_pad, LANE), 0) * EXPERT_BLOCK).astype(F32)
    hit = jnp.where((lane_b < N_EXPERTS) & (pend[0:1, :] <= start), 1.0, 0.0)
    e_blk = jnp.minimum(jnp.sum(hit, axis=-1, keepdims=True), N_EXPERTS - 1.0)
    used = jnp.max(pend[0:1, :], axis=-1, keepdims=True) * (1.0 / EXPERT_BLOCK)
    bexp_ref[...] = jnp.where(lane_b == 0, e_blk, jnp.where(lane_b == 1, used, 0.0)).astype(I32)


def _moe_dest(rt, rank, cnt, n_blk, rows=1024):
    n = rt.shape[0]
    n_blk_pad = -(-n_blk // 8) * 8
    return pl.pallas_call(
        functools.partial(_dest_kernel, n_blk_pad=n_blk_pad),
        grid=(n // rows,),
        in_specs=[pl.BlockSpec((rows, LANE), lambda i: (i, 0)), pl.BlockSpec((rows, LANE), lambda i: (i, 0)),
                  pl.BlockSpec((8, LANE), lambda i: (0, 0))],
        out_specs=[pl.BlockSpec((rows, LANE), lambda i: (i, 0)), pl.BlockSpec((n_blk_pad, LANE), lambda i: (0, 0))],
        out_shape=[jax.ShapeDtypeStruct((n, LANE), I32), jax.ShapeDtypeStruct((n_blk_pad, LANE), I32)],
        compiler_params=_cparams(("arbitrary",)),
        name="moe_dest",
    )(rt, rank, cnt)


MOE_ROWS = 512
DMA_UNROLL = 8


def _dispatch_kernel(dest_ref, h_ref, xin_ref, xpad_ref, sem):
    del xin_ref

    def row_copy(r, d):
        return pltpu.make_async_copy(h_ref.at[pl.ds(r, 1)], xpad_ref.at[pl.ds(d, 1)], sem)

    def start(r, carry):
        for j in range(EXPERT_TOPK):
            row_copy(r, dest_ref[EXPERT_TOPK * r + j]).start(priority=j)
        return carry

    lax.fori_loop(0, MOE_ROWS, start, 0, unroll=DMA_UNROLL)

    def wait(r, carry):
        for j in range(EXPERT_TOPK):
            row_copy(r, dest_ref[EXPERT_TOPK * r + j]).wait()
        return carry

    lax.fori_loop(0, MOE_ROWS, wait, 0, unroll=DMA_UNROLL)


def _moe_dispatch(dest_flat, h, n_slots):
    n, d = h.shape
    zeros = jnp.zeros((n_slots, d), h.dtype)
    return pl.pallas_call(
        _dispatch_kernel,
        grid=(n // MOE_ROWS,),
        in_specs=[pl.BlockSpec((EXPERT_TOPK * MOE_ROWS,), lambda i: (i,), memory_space=pltpu.SMEM),
                  pl.BlockSpec((MOE_ROWS, d), lambda i: (i, 0)),
                  pl.BlockSpec(memory_space=pl.ANY)],
        out_specs=pl.BlockSpec(memory_space=pl.ANY),
        out_shape=jax.ShapeDtypeStruct((n_slots, d), h.dtype),
        scratch_shapes=[pltpu.SemaphoreType.DMA(())],
        input_output_aliases={2: 0},
        compiler_params=_cparams(("arbitrary",)),
        name="moe_dispatch",
    )(dest_flat, h, zeros)


def _expert_kernel(bexp_ref, used_ref, x_ref, wg_ref, wu_ref, wd_ref, y_ref):
    blk = pl.program_id(0)

    @pl.when(blk < used_ref[0])
    def _():
        xb = x_ref[...].astype(BF16)
        gate = _dot(xb, wg_ref[...].astype(BF16))
        up = _dot(xb, wu_ref[...].astype(BF16))
        act = (jax.nn.silu(gate) * up).astype(BF16)
        y_ref[...] = _dot(act, wd_ref[...].astype(BF16))

    @pl.when(blk >= used_ref[0])
    def _():
        y_ref[...] = jnp.zeros_like(y_ref)


def _moe_experts(bexp, used, xpad, w_gate, w_up, w_down):
    n_slots, d = xpad.shape
    n_blk = n_slots // EXPERT_BLOCK
    de = w_gate.shape[-1]
    grid_spec = pltpu.PrefetchScalarGridSpec(
        num_scalar_prefetch=2,
        grid=(n_blk,),
        in_specs=[
            pl.BlockSpec((EXPERT_BLOCK, d), lambda i, be, us: (i, 0)),
            pl.BlockSpec((None, d, de), lambda i, be, us: (be[i], 0, 0)),
            pl.BlockSpec((None, d, de), lambda i, be, us: (be[i], 0, 0)),
            pl.BlockSpec((None, de, d), lambda i, be, us: (be[i], 0, 0)),
        ],
        out_specs=pl.BlockSpec((EXPERT_BLOCK, d), lambda i, be, us: (i, 0)),
    )
    return pl.pallas_call(
        _expert_kernel,
        grid_spec=grid_spec,
        out_shape=jax.ShapeDtypeStruct((n_slots, d), F32),
        compiler_params=_cparams(("arbitrary",), 48),
        name="moe_experts",
    )(bexp, used, xpad, w_gate, w_up, w_down)


def _combine_kernel(dest_ref, h_ref, rw_ref, g_ref, b_ref, y_ref, o_ref, buf_ref, sem, *, alpha):
    def row_copy(r, j):
        return pltpu.make_async_copy(y_ref.at[pl.ds(dest_ref[EXPERT_TOPK * r + j], 1)],
                                     buf_ref.at[j, pl.ds(r, 1)], sem)

    def start(r, carry):
        for j in range(EXPERT_TOPK):
            row_copy(r, j).start(priority=j)
        return carry

    lax.fori_loop(0, MOE_ROWS, start, 0, unroll=DMA_UNROLL)

    def wait(r, carry):
        for j in range(EXPERT_TOPK):
            row_copy(r, j).wait()
        return carry

    lax.fori_loop(0, MOE_ROWS, wait, 0, unroll=DMA_UNROLL)
    moe = buf_ref[0] * rw_ref[:, 0:1] + buf_ref[1] * rw_ref[:, 1:2]
    o_ref[...] = _layer_norm(alpha * h_ref[...] + moe, g_ref[...], b_ref[...])


def _moe_combine(dest_flat, h, rw, g, b_, ypad, alpha):
    n, d = h.shape
    return pl.pallas_call(
        functools.partial(_combine_kernel, alpha=alpha),
        grid=(n // MOE_ROWS,),
        in_specs=[pl.BlockSpec((EXPERT_TOPK * MOE_ROWS,), lambda i: (i,), memory_space=pltpu.SMEM),
                  pl.BlockSpec((MOE_ROWS, d), lambda i: (i, 0)),
                  pl.BlockSpec((MOE_ROWS, LANE), lambda i: (i, 0)),
                  pl.BlockSpec((1, d), lambda i: (0, 0)),
                  pl.BlockSpec((1, d), lambda i: (0, 0)),
                  pl.BlockSpec(memory_space=pl.ANY)],
        out_specs=pl.BlockSpec((MOE_ROWS, d), lambda i: (i, 0)),
        out_shape=jax.ShapeDtypeStruct((n, d), F32),
        scratch_shapes=[pltpu.VMEM((EXPERT_TOPK, MOE_ROWS, d), F32), pltpu.SemaphoreType.DMA(())],
        compiler_params=_cparams(("arbitrary",), 48),
        name="moe_combine",
    )(dest_flat, h, rw, g, b_, ypad)


def _cmp_map_t(t):
    nc = (t - CMP_BLOCK) // CMP_STRIDE + 1
    ns = t // SLC_BLOCK
    ncp = t // CMP_STRIDE
    cs = CMP_STRIDE * np.arange(nc)[:, None]
    ss = SLC_BLOCK * np.arange(ns)[None, :]
    ov = np.minimum(cs + CMP_BLOCK, ss + SLC_BLOCK) - np.maximum(cs, ss)
    m = np.clip(ov, 0, None).astype(np.float32) / CMP_STRIDE
    out = np.zeros((ns, ncp), np.float32)
    out[:, :nc] = m.T
    return jnp.asarray(out, BF16)


def _block_expand(t):
    ns = t // SLC_BLOCK
    rows = max(ns, LANE)
    e = np.zeros((rows, t), np.float32)
    e[np.arange(t) // SLC_BLOCK, np.arange(t)] = 1.0
    return jnp.asarray(e, BF16)


def _pad_head_rows(w, n_heads):
    wh = w.reshape(n_heads, HEAD_DIM, w.shape[-1])
    return jnp.concatenate([wh, jnp.zeros_like(wh)], axis=1).reshape(n_heads * LANE, w.shape[-1]).astype(BF16)


def kernel(x, w_in, cmp_pe_k, cmp_pe_v, cmp_w1_k, cmp_w2_k, cmp_w1_v, cmp_w2_v, ckv_norm_g, w_uk, w_uv, rel_bias,
           w_branch_a, w_branch_b, w_out, ln1_g, ln1_b, w_grp, b_grp, w_rtr, b_rtr, w_gate, w_up, w_down, ln2_g,
           ln2_b):
    b, t, d = x.shape
    n = b * t
    depth = w_in.shape[0]
    alpha = (2.0 * depth) ** 0.25
    assert t % 512 == 0 and t >= WIN_KEYS and n % MOE_ROWS == 0

    sslc = _bias_strip(rel_bias, NSA_HEADS, 0, STRIP_A, STRIP_W, None, True)
    sdsa = _bias_strip(rel_bias, DSA_HEADS, NSA_HEADS, STRIP_A, STRIP_W, None, True)
    swin = _bias_strip(rel_bias, NSA_HEADS, 0, WIN_A, WIN_W, WINDOW, False)
    bcmp = _cmp_bias(rel_bias, t)
    mapt = _cmp_map_t(t)
    eall = _block_expand(t)

    n_a = n * EXPERT_TOPK
    n_blk = -(-n_a // EXPERT_BLOCK) + N_EXPERTS
    n_slots = n_blk * EXPERT_BLOCK

    h = x.reshape(n, d)
    for l in range(depth):
        w_pad = _proj_weights(w_in[l])
        wuk = w_uk[l]
        z = jnp.zeros_like(wuk[0])
        wuk_pairs = jnp.stack([
            jnp.concatenate([jnp.concatenate([wuk[2 * k], z], axis=1), jnp.concatenate([z, wuk[2 * k + 1]], axis=1)],
                            axis=0) for k in range(DSA_HEADS // 2)]).astype(BF16)
        wuv_pad = jnp.pad(w_uv[l], ((0, 0), (0, 0), (0, LANE - HEAD_DIM))).astype(BF16)
        wk = _cmp_weights(cmp_w1_k[l], cmp_w2_k[l], cmp_pe_k[l], HEAD_DIM)
        wv = _cmp_weights(cmp_w1_v[l], cmp_w2_v[l], cmp_pe_v[l], LANE)
        wa_pad = _pad_head_rows(w_branch_a[l], NSA_HEADS)
        wb_pad = _pad_head_rows(w_branch_b[l], DSA_HEADS)
        wr = jnp.zeros((d, LANE), F32).at[:, RT_GRP:RT_GRP + N_EXPERT_GROUPS].set(w_grp[l])
        wr = wr.at[:, RT_EXP:RT_EXP + N_EXPERTS].set(w_rtr[l])
        wr_hi = wr.astype(BF16)
        wr = jnp.concatenate([wr_hi, (wr - wr_hi.astype(F32)).astype(BF16)], axis=1)
        br = jnp.zeros((1, LANE), F32).at[0, RT_GRP:RT_GRP + N_EXPERT_GROUPS].set(b_grp[l])
        br = br.at[0, RT_EXP:RT_EXP + N_EXPERTS].set(b_rtr[l])

        (qa, kc, vc, kv, gn, qlat, ckvn, qidx, kidx, widx, sga, sgb) = _proj(
            h, w_pad, wuk_pairs, ckv_norm_g[l].reshape(1, KV_RANK))
        kcmp, vcmp = _compress(kc, vc, wk, wv, b, t)

        def b3(a):
            return a.reshape(b, t, a.shape[-1])

        oa = _nsa(b3(qa), b3(gn), kcmp, vcmp, b3(kv), sslc, swin, bcmp, mapt, eall, b, t)
        ob = _dsa(b3(qlat), b3(qidx), b3(widx), b3(kidx), b3(ckvn), sdsa, wuv_pad, b, t)

        h1, rt, rw = _post(h, oa.reshape(n, -1), ob.reshape(n, -1), sga, sgb, wa_pad, wb_pad, w_out[l].astype(BF16),
                           ln1_g[l].reshape(1, d), ln1_b[l].reshape(1, d), wr, br, alpha)

        rank, cnt = _moe_rank(rt)
        dest, bexp = _moe_dest(rt, rank, cnt, n_blk)
        dest_flat = dest[:, :EXPERT_TOPK].reshape(n_a)
        xpad = _moe_dispatch(dest_flat, h1, n_slots)
        ypad = _moe_experts(bexp[:n_blk, 0], bexp[:1, 1], xpad, w_gate[l], w_up[l], w_down[l])
        h = _moe_combine(dest_flat, h1, rw, ln2_g[l].reshape(1, d), ln2_b[l].reshape(1, d), ypad, alpha)
    return h.reshape(b, t, d)
```

```python
import functools
import math

import numpy as np
import jax
import jax.numpy as jnp
from jax import lax
from jax.experimental import pallas as pl
from jax.experimental.pallas import tpu as pltpu

F32 = jnp.float32
BF16 = jnp.bfloat16
I32 = jnp.int32

D_MODEL = 1024
HEAD_DIM = 64
NSA_HEADS = 8
NSA_GROUPS = 2
NSA_HPG = NSA_HEADS // NSA_GROUPS
CMP_BLOCK = 32
CMP_STRIDE = 16
CMP_HIDDEN = 256
SLC_BLOCK = 64
SLC_TOPN = 16
SLC_FORCE = 1e4
WINDOW = 512
DSA_HEADS = 8
KV_RANK = 128
IDX_HEADS = 4
IDX_DIM = 64
IDX_TOPK_MAX = 256
REL_BUCKETS = 32
REL_EXACT = 16
REL_MAX_DIST = 1024
N_EXPERT_GROUPS = 4
EXPERTS_PER_GROUP = 8
N_EXPERTS = N_EXPERT_GROUPS * EXPERTS_PER_GROUP
EXPERT_TOPK = 2
D_EXPERT = 256
EXPERT_BLOCK = 256
NSA_WIDTH = NSA_HEADS * HEAD_DIM
DSA_WIDTH = DSA_HEADS * HEAD_DIM
NEG = -1e30
SPLIT_SIZES = (NSA_WIDTH,) + (NSA_GROUPS * HEAD_DIM,) * 6 + (
    NSA_HEADS * 3, DSA_WIDTH, KV_RANK, IDX_HEADS * IDX_DIM, IDX_DIM, IDX_HEADS, D_MODEL, D_MODEL)
SPLIT_POINTS = tuple(int(v) for v in np.cumsum(SPLIT_SIZES)[:-1])

LANE = 128
QT = 128
KT = 512
TPK = KT // QT
RB = 64
M_FLOOR = -1e29
BIG = 3e38
LOG2E = math.log2(math.e)

FAR_TILES = 11
STRIP_A = FAR_TILES * QT
STRIP_W = STRIP_A + KT
assert REL_EXACT + int(math.log((STRIP_A - KT + 1) / REL_EXACT) / math.log(REL_MAX_DIST / REL_EXACT)
                       * (REL_BUCKETS - REL_EXACT)) >= REL_BUCKETS - 1
WIN_A = WINDOW
WIN_KEYS = WINDOW + QT
WIN_W = WIN_A + WIN_KEYS

O_QA, O_KC, O_VC, O_KV, O_GN, O_QB, O_CKV, O_QIDX, O_KIDX, O_WIDX, O_GA, O_GB, PROJ_W = (
    int(v) for v in np.cumsum([0, 1024, 128, 128, 768, 128, 512, 128, 512, 128, 128, 1024, 1024]))
KV_KS, KV_VS, KV_KW, KV_VW, KV_W = 0, 128, 384, 512, 768
ONES_LANE = HEAD_DIM

_NT = (((1,), (1,)), ((), ()))


def _dot(a, b):
    return jnp.dot(a, b, preferred_element_type=F32)


def _dot_nt(a, b):
    return lax.dot_general(a, b, _NT, preferred_element_type=F32)


def _cparams(sem, vmem_mb=None):
    kw = dict(dimension_semantics=sem)
    if vmem_mb is not None:
        kw["vmem_limit_bytes"] = vmem_mb * 1024 * 1024
    return pltpu.CompilerParams(**kw)


def _proj_weights(w_in):
    (q_a, kc, vc, ks, vs, kw, vw, g_nsa, q_b, ckv, q_idx, k_idx, w_idx, ga, gb) = jnp.split(w_in, SPLIT_POINTS, axis=1)
    d = w_in.shape[0]
    scale = HEAD_DIM ** -0.5 * LOG2E
    z64 = jnp.zeros((d, NSA_HPG, HEAD_DIM), F32)
    qa = (q_a * scale).reshape(d, NSA_GROUPS, NSA_HPG, HEAD_DIM)
    qa_pad = jnp.concatenate([
        jnp.concatenate([qa[:, 0], z64], axis=-1).reshape(d, NSA_HPG * LANE),
        jnp.concatenate([z64, qa[:, 1]], axis=-1).reshape(d, NSA_HPG * LANE)], axis=1)
    qi = q_idx.reshape(d, IDX_HEADS, IDX_DIM)
    qi_pad = jnp.concatenate([qi, jnp.zeros_like(qi)], axis=-1).reshape(d, IDX_HEADS * LANE)

    def pad(a):
        return jnp.pad(a, ((0, 0), (0, LANE - a.shape[1])))

    def per_group(v):
        return jnp.concatenate([pad(v[:, :HEAD_DIM]), pad(v[:, HEAD_DIM:])], axis=1)

    w_idx_s = w_idx * (IDX_HEADS ** -0.5 * IDX_DIM ** -0.5)
    cols = [qa_pad, kc, vc, ks, per_group(vs), kw, per_group(vw), pad(g_nsa), q_b * scale, ckv, qi_pad, pad(k_idx),
            pad(w_idx_s), ga, gb]
    w = jnp.concatenate(cols, axis=1)
    assert w.shape[1] == PROJ_W
    return w.astype(BF16)


def _proj_kernel(x_ref, w_ref, wuk_ref, g_ref, qa_o, kc_o, vc_o, kv_o, gn_o, qlat_o, ckv_o, qidx_o, kidx_o,
                 widx_o, sga_o, sgb_o):
    xb = x_ref[...].astype(BF16)

    def mm(lo, n):
        return _dot(xb, w_ref[:, lo:lo + n])

    qa_o[...] = mm(O_QA, 1024).astype(BF16)
    kc_o[...] = mm(O_KC, 128).astype(BF16)
    vc_o[...] = mm(O_VC, 128).astype(BF16)
    lane = lax.broadcasted_iota(I32, (x_ref.shape[0], KV_W), 1)
    is_one = ((lane & (LANE - 1)) == ONES_LANE) & (((lane >= KV_VS) & (lane < KV_KW)) | (lane >= KV_VW))
    kv_o[...] = jnp.where(is_one, 1.0, mm(O_KV, KV_W)).astype(BF16)
    gn_o[...] = jax.nn.sigmoid(mm(O_GN, 128))
    qb = mm(O_QB, 512).astype(BF16)
    for k in range(DSA_HEADS // 2):
        qlat_o[:, 256 * k:256 * (k + 1)] = _dot(qb[:, 128 * k:128 * (k + 1)], wuk_ref[k]).astype(BF16)
    c = mm(O_CKV, 128)
    ms = jnp.mean(c * c, axis=-1, keepdims=True)
    ckv_o[...] = (c * lax.rsqrt(ms + 1e-6) * g_ref[...]).astype(BF16)
    qidx_o[...] = mm(O_QIDX, 512).astype(BF16)
    kidx_o[...] = mm(O_KIDX, 128).astype(BF16)
    widx_o[...] = mm(O_WIDX, 128)
    sga_o[...] = jax.nn.sigmoid(mm(O_GA, 1024)).astype(BF16)
    sgb_o[...] = jax.nn.sigmoid(mm(O_GB, 1024)).astype(BF16)


def _proj(x2, w_pad, wuk_pairs, ckv_g, rows=512):
    n, d = x2.shape
    widths = [(1024, BF16), (128, BF16), (128, BF16), (KV_W, BF16), (128, F32), (1024, BF16), (128, BF16),
              (512, BF16), (128, BF16), (128, F32), (1024, BF16), (1024, BF16)]
    return pl.pallas_call(
        _proj_kernel,
        grid=(n // rows,),
        in_specs=[
            pl.BlockSpec((rows, d), lambda i: (i, 0)),
            pl.BlockSpec((d, PROJ_W), lambda i: (0, 0)),
            pl.BlockSpec((DSA_HEADS // 2, 128, 256), lambda i: (0, 0, 0)),
            pl.BlockSpec((1, KV_RANK), lambda i: (0, 0)),
        ],
        out_specs=[pl.BlockSpec((rows, w), lambda i: (i, 0)) for w, _ in widths],
        out_shape=[jax.ShapeDtypeStruct((n, w), dt) for w, dt in widths],
        compiler_params=_cparams(("arbitrary",), 56),
        name="proj",
    )(x2, w_pad, wuk_pairs, ckv_g)


def _cmp_weights(w1, w2, pe, out_lanes):
    half = CMP_BLOCK // 2
    w1r = w1.reshape(CMP_BLOCK, HEAD_DIM, CMP_HIDDEN)
    eye = jnp.eye(NSA_GROUPS, dtype=F32)

    def expand(wl):
        return jnp.einsum('ldj,gh->lgdhj', wl, eye).reshape(half * NSA_GROUPS * HEAD_DIM, NSA_GROUPS * CMP_HIDDEN)

    top, bot = expand(w1r[:half]), expand(w1r[half:])

    def pe_rows(p):
        return jnp.broadcast_to(p[:, None, :], (half, NSA_GROUPS, HEAD_DIM)).reshape(1, -1)

    w2p = w2 if out_lanes == HEAD_DIM else jnp.pad(w2, ((0, 0), (0, out_lanes - HEAD_DIM)))
    w2bd = jnp.einsum('jd,gh->gjhd', w2p, eye).reshape(NSA_GROUPS * CMP_HIDDEN, NSA_GROUPS * out_lanes)
    return (top.astype(BF16), bot.astype(BF16), pe_rows(pe[:half]).astype(BF16), pe_rows(pe[half:]).astype(BF16),
            w2bd.astype(BF16))


def _compress_kernel(hk_ref, hv_ref, kt_ref, kb_ref, kpt_ref, kpb_ref, k2_ref, vt_ref, vb_ref, vpt_ref, vpb_ref,
                     v2_ref, ko_ref, vo_ref):
    ncp = hk_ref.shape[0]

    def one(h_ref, top_ref, bot_ref, pt_ref, pb_ref, w2_ref, o_ref):
        h = h_ref[...]
        a = _dot(h, top_ref[...])
        b = _dot(h, bot_ref[...])
        pe8t = jnp.broadcast_to(pt_ref[...], (8, pt_ref.shape[1]))
        pe8b = jnp.broadcast_to(pb_ref[...], (8, pb_ref.shape[1]))
        pe_term = (_dot(pe8t, top_ref[...]) + _dot(pe8b, bot_ref[...]))[0:1]
        pre = a + pltpu.roll(b, ncp - 1, 0) + pe_term
        hid = jax.nn.gelu(pre, approximate=True)
        o_ref[...] = _dot(hid.astype(BF16), w2_ref[...]).astype(BF16)

    one(hk_ref, kt_ref, kb_ref, kpt_ref, kpb_ref, k2_ref, ko_ref)
    one(hv_ref, vt_ref, vb_ref, vpt_ref, vpb_ref, v2_ref, vo_ref)


def _compress(kc, vc, wk, wv, b, t):
    ncp = t // CMP_STRIDE
    hw = CMP_STRIDE * NSA_GROUPS * HEAD_DIM
    hk = kc.reshape(b, ncp, hw)
    hv = vc.reshape(b, ncp, hw)
    hspec = pl.BlockSpec((None, ncp, hw), lambda i: (i, 0, 0))

    def full(a):
        return pl.BlockSpec(a.shape, lambda i: (0,) * a.ndim)

    widths = (wk[-1].shape[1], wv[-1].shape[1])
    return pl.pallas_call(
        _compress_kernel,
        grid=(b,),
        in_specs=[hspec, hspec] + [full(a) for a in wk] + [full(a) for a in wv],
        out_specs=[pl.BlockSpec((None, ncp, w), lambda i: (i, 0, 0)) for w in widths],
        out_shape=[jax.ShapeDtypeStruct((b, ncp, w), BF16) for w in widths],
        compiler_params=_cparams(("arbitrary",), 48),
        name="compress",
    )(hk, hv, *wk, *wv)


def _rel_bucket(dist):
    n = jnp.maximum(dist, 0)
    nf = jnp.maximum(n, 1).astype(F32)
    large = REL_EXACT + (jnp.log(nf / REL_EXACT) / math.log(REL_MAX_DIST / REL_EXACT)
                         * (REL_BUCKETS - REL_EXACT)).astype(I32)
    return jnp.where(n < REL_EXACT, n, jnp.minimum(large, REL_BUCKETS - 1))


def _rel_lookup(tab_ref, bucket, col):
    bits = [(bucket & (1 << k)) != 0 for k in range(5)]
    vals = [jnp.where(bits[0], tab_ref[2 * k + 1, col], tab_ref[2 * k, col]) for k in range(REL_BUCKETS // 2)]
    for lvl in range(1, 5):
        vals = [jnp.where(bits[lvl], vals[2 * k + 1], vals[2 * k]) for k in range(len(vals) // 2)]
    return vals[0]


def _strip_kernel(tab_ref, o_ref, *, a, window, head0, rel_far):
    h = pl.program_id(0)
    shape = o_ref.shape
    r = lax.broadcasted_iota(I32, shape, 0)
    j = lax.broadcasted_iota(I32, shape, 1)
    dist = r + a - j
    valid = dist >= 0
    if window is not None:
        valid = valid & (dist < window)
    val = _rel_lookup(tab_ref, _rel_bucket(dist), h + head0)
    if rel_far:
        val = val - tab_ref[REL_BUCKETS - 1, h + head0]
    o_ref[...] = jnp.where(valid, val * LOG2E, NEG)


def _bias_strip(rel_bias, n_heads, head0, a, width, window, rel_far):
    return pl.pallas_call(
        functools.partial(_strip_kernel, a=a, window=window, head0=head0, rel_far=rel_far),
        grid=(n_heads,),
        in_specs=[pl.BlockSpec(memory_space=pltpu.SMEM)],
        out_specs=pl.BlockSpec((None, QT, width), lambda h: (h, 0, 0)),
        out_shape=jax.ShapeDtypeStruct((n_heads, QT, width), F32),
        compiler_params=_cparams(("arbitrary",)),
        name="bias_strip",
    )(rel_bias)


def _cmp_bias_kernel(tab_ref, o_ref, *, nc):
    i = pl.program_id(0)
    shape = o_ref.shape[1:]
    r = lax.broadcasted_iota(I32, shape, 0)
    c = lax.broadcasted_iota(I32, shape, 1)
    dist = i * QT + r - (CMP_STRIDE * c + CMP_BLOCK - 1)
    valid = (dist >= 0) & (c < nc)
    bucket = _rel_bucket(dist)
    for h in range(NSA_HEADS):
        o_ref[h] = jnp.where(valid, _rel_lookup(tab_ref, bucket, h) * LOG2E, NEG)


def _cmp_bias(rel_bias, t):
    n_qt = t // QT
    ncp = t // CMP_STRIDE
    nc = (t - CMP_BLOCK) // CMP_STRIDE + 1
    return pl.pallas_call(
        functools.partial(_cmp_bias_kernel, nc=nc),
        grid=(n_qt,),
        in_specs=[pl.BlockSpec(memory_space=pltpu.SMEM)],
        out_specs=pl.BlockSpec((None, NSA_HEADS, QT, ncp), lambda i: (i, 0, 0, 0)),
        out_shape=jax.ShapeDtypeStruct((n_qt, NSA_HEADS, QT, ncp), F32),
        compiler_params=_cparams(("arbitrary",)),
        name="cmp_bias",
    )(rel_bias)


def _lane_groups(op, x):
    parts = [x[:, c * LANE:(c + 1) * LANE] for c in range(x.shape[1] // LANE)]
    while len(parts) > 1:
        parts = [op(parts[j], parts[j + 1]) for j in range(0, len(parts), 2)]
    return parts[0]


def _far_tiles(i):
    return jnp.maximum((i - (FAR_TILES - TPK)) >> (TPK.bit_length() - 1), 0)


def _strip_offset(i, kt):
    return pl.multiple_of(QT * jnp.maximum(FAR_TILES - (i - TPK * kt), 0), LANE)


def _for_tiles(lo, hi, step):
    n = hi - lo

    def pair(j, c):
        step(lo + 2 * j)
        step(lo + 2 * j + 1)
        return c

    lax.fori_loop(0, n >> 1, pair, 0)

    @pl.when((n & 1) == 1)
    def _():
        step(hi - 1)


def _nsa_kernel(qa_ref, gn_ref, kc_ref, vc_ref, kv_ref, sslc_ref, swin_ref, bcmp_ref, mapt_ref, eall_ref, o_ref,
                q_ref, s_ref, z_ref, p_ref, m_ref, acc_ref, ocmp_ref, sel_ref, st_ref, *, t):
    i = pl.program_id(1)
    q0 = i * QT
    ns = t // SLC_BLOCK
    ncp = t // CMP_STRIDE
    n_top = min(SLC_TOPN, ns)
    grows = NSA_HPG * QT
    arows = NSA_HEADS * QT
    shift = SLC_BLOCK.bit_length() - 1
    halves = QT // RB

    for h in range(NSA_HEADS):
        q_ref[h * QT:(h + 1) * QT, :] = qa_ref[:, h * LANE:(h + 1) * LANE]

    s_ref[:, 0:ncp] = _dot_nt(q_ref[...], kc_ref[...])
    imp = []
    for g in range(NSA_GROUPS):
        psum = None
        for h in range(NSA_HPG * g, NSA_HPG * (g + 1)):
            rows = slice(h * QT, (h + 1) * QT)
            z = s_ref[rows, 0:ncp] + bcmp_ref[h]
            m = jnp.maximum(jnp.max(z, axis=-1, keepdims=True), M_FLOOR)
            e = jnp.exp2(z - m)
            p = e / jnp.maximum(jnp.sum(e, axis=-1, keepdims=True), 1e-30)
            p_ref[rows, 0:ncp] = p.astype(BF16)
            psum = p if psum is None else psum + p
        grp = slice(g * grows, (g + 1) * grows)
        ocmp_ref[grp, :] = _dot(p_ref[grp, 0:ncp], vc_ref[:, g * LANE:(g + 1) * LANE])
        imp.append(_dot_nt(mapt_ref[...], psum.astype(BF16)))
    imp_t = jnp.concatenate(imp, axis=1)

    blk = lax.broadcasted_iota(I32, (ns, NSA_GROUPS * QT), 0)
    tq = q0 + (lax.broadcasted_iota(I32, (ns, NSA_GROUPS * QT), 1) & (QT - 1))
    cur = tq >> shift
    forced = (blk == 0) | (blk == cur) | (blk == cur - 1)
    avail = (blk << shift) <= tq
    score = jnp.where(avail, imp_t + jnp.where(forced, SLC_FORCE, 0.0), NEG)
    st_ref[0:ns, :] = score

    sub8 = 8
    groups = [score[v * sub8:(v + 1) * sub8] for v in range(ns // sub8)]
    blk8 = lax.broadcasted_iota(I32, (sub8, NSA_GROUPS * QT), 0)
    ranks = [jnp.zeros((sub8, NSA_GROUPS * QT), F32) for _ in groups]
    for jp in range(ns):
        rowb = jnp.broadcast_to(st_ref[jp:jp + 1, :], (sub8, NSA_GROUPS * QT))
        for v, sv in enumerate(groups):
            ge = jnp.where(rowb >= sv, 1.0, 0.0)
            gt = jnp.where(rowb > sv, 1.0, 0.0)
            if v * sub8 > jp:
                inc = ge
            elif v * sub8 + sub8 - 1 <= jp:
                inc = gt
            else:
                inc = jnp.where(blk8 + v * sub8 > jp, ge, gt)
            ranks[v] = ranks[v] + inc
    rank = jnp.concatenate(ranks, axis=0)
    sel_t = jnp.where((rank < n_top) & avail, 1.0, 0.0)
    if ns < LANE:
        sel_t = jnp.concatenate([sel_t, jnp.zeros((LANE - ns, NSA_GROUPS * QT), F32)], axis=0)
    for g in range(NSA_GROUPS):
        sel_ref[g] = sel_t[:, g * QT:(g + 1) * QT].T.astype(BF16)

    m_ref[...] = jnp.full((arows, LANE), M_FLOOR, F32)
    acc_ref[...] = jnp.zeros((arows, LANE), F32)

    def max_step(kt, near):
        k0 = pl.multiple_of(kt * KT, KT)
        s = _dot_nt(q_ref[...], kv_ref[pl.ds(k0, KT), KV_KS:KV_KS + LANE])
        for g in range(NSA_GROUPS):
            madd = (_dot(sel_ref[g], eall_ref[:, pl.ds(k0, KT)]) - 1.0) * (-NEG)
            for h in range(NSA_HPG * g, NSA_HPG * (g + 1)):
                for half in range(halves):
                    qrows = slice(half * RB, (half + 1) * RB)
                    rows = slice(h * QT + half * RB, h * QT + (half + 1) * RB)
                    z = s[rows] + madd[qrows]
                    if near:
                        z = z + sslc_ref[h, qrows, pl.ds(_strip_offset(i, kt), KT)]
                    z_ref[rows, pl.ds(k0, KT)] = z
                    m_ref[rows, :] = jnp.maximum(m_ref[rows, :], _lane_groups(jnp.maximum, z))

    n_far = _far_tiles(i)
    n_kt = (i >> (TPK.bit_length() - 1)) + 1
    _for_tiles(0, n_far, lambda kt: max_step(kt, False))
    _for_tiles(n_far, n_kt, lambda kt: max_step(kt, True))
    m_ref[...] = jnp.broadcast_to(jnp.max(m_ref[...], axis=-1, keepdims=True), (arows, LANE))

    def sum_step(kt):
        k0 = pl.multiple_of(kt * KT, KT)
        for g in range(NSA_GROUPS):
            ps = []
            for h in range(NSA_HPG * g, NSA_HPG * (g + 1)):
                for half in range(halves):
                    rows = slice(h * QT + half * RB, h * QT + (half + 1) * RB)
                    m = m_ref[rows, :]
                    z = z_ref[rows, pl.ds(k0, KT)]
                    ps.append(jnp.exp2(z - jnp.concatenate([m] * (KT // LANE), axis=1)).astype(BF16))
            grp = slice(g * grows, (g + 1) * grows)
            acc_ref[grp, :] += _dot(jnp.concatenate(ps, axis=0),
                                    kv_ref[pl.ds(k0, KT), KV_VS + g * LANE:KV_VS + (g + 1) * LANE])

    _for_tiles(0, n_kt, sum_step)

    ks0 = pl.multiple_of(jnp.maximum(i - WINDOW // QT, 0) * QT, QT)
    woff = pl.multiple_of(jnp.maximum(WINDOW // QT - i, 0) * QT, LANE)
    s_ref[...] = _dot_nt(q_ref[...], kv_ref[pl.ds(ks0, WIN_KEYS), KV_KW:KV_KW + LANE])
    for h in range(NSA_HEADS):
        for half in range(halves):
            qrows = slice(half * RB, (half + 1) * RB)
            rows = slice(h * QT + half * RB, h * QT + (half + 1) * RB)
            z = s_ref[rows, :] + swin_ref[h, qrows, pl.ds(woff, WIN_KEYS)]
            p_ref[rows, :] = jnp.exp2(z - jnp.max(z, axis=-1, keepdims=True)).astype(BF16)

    lane_ok = lax.broadcasted_iota(I32, (QT, LANE), 1) < HEAD_DIM
    for g in range(NSA_GROUPS):
        grp = slice(g * grows, (g + 1) * grows)
        o_win = _dot(p_ref[grp, :], kv_ref[pl.ds(ks0, WIN_KEYS), KV_VW + g * LANE:KV_VW + (g + 1) * LANE])
        for hp in range(NSA_HPG):
            h = NSA_HPG * g + hp
            rows = slice(h * QT, (h + 1) * QT)
            wrows = slice(hp * QT, (hp + 1) * QT)
            slc = acc_ref[rows, :]
            o_slc = slc / jnp.maximum(slc[:, ONES_LANE:ONES_LANE + 1], 1e-30)
            win = o_win[wrows]
            o_w = win / jnp.maximum(win[:, ONES_LANE:ONES_LANE + 1], 1e-30)
            o = (gn_ref[:, 3 * h:3 * h + 1] * ocmp_ref[rows, :] + gn_ref[:, 3 * h + 1:3 * h + 2] * o_slc
                 + gn_ref[:, 3 * h + 2:3 * h + 3] * o_w)
            o_ref[:, h * LANE:(h + 1) * LANE] = jnp.where(lane_ok, o, 0.0).astype(BF16)


def _nsa(qa, gn, kcmp, vcmp, kv, sslc, swin, bcmp, mapt, eall, b, t):
    n_qt = t // QT
    ncp = t // CMP_STRIDE
    ns = t // SLC_BLOCK
    arows = NSA_HEADS * QT
    assert ncp <= WIN_KEYS

    def full(a):
        return pl.BlockSpec(a.shape, lambda bi, i: (0,) * a.ndim)

    def once(a):
        return pl.BlockSpec(a.shape, lambda bi, i: (0,) * a.ndim, pipeline_mode=pl.Buffered(1))

    return pl.pallas_call(
        functools.partial(_nsa_kernel, t=t),
        grid=(b, n_qt),
        in_specs=[
            pl.BlockSpec((None, QT, NSA_HEADS * LANE), lambda bi, i: (bi, i, 0)),
            pl.BlockSpec((None, QT, LANE), lambda bi, i: (bi, i, 0)),
            pl.BlockSpec((None, ncp, LANE), lambda bi, i: (bi, 0, 0)),
            pl.BlockSpec((None, ncp, NSA_GROUPS * LANE), lambda bi, i: (bi, 0, 0)),
            pl.BlockSpec((None, t, KV_W), lambda bi, i: (bi, 0, 0), pipeline_mode=pl.Buffered(1)),
            once(sslc), once(swin),
            pl.BlockSpec((None, NSA_HEADS, QT, ncp), lambda bi, i: (i, 0, 0, 0)),
            full(mapt), once(eall),
        ],
        out_specs=pl.BlockSpec((None, QT, NSA_HEADS * LANE), lambda bi, i: (bi, i, 0)),
        out_shape=jax.ShapeDtypeStruct((b, t, NSA_HEADS * LANE), BF16),
        scratch_shapes=[
            pltpu.VMEM((arows, LANE), BF16),
            pltpu.VMEM((arows, WIN_KEYS), F32),
            pltpu.VMEM((arows, t), F32),
            pltpu.VMEM((arows, WIN_KEYS), BF16),
            pltpu.VMEM((arows, LANE), F32),
            pltpu.VMEM((arows, LANE), F32),
            pltpu.VMEM((arows, LANE), F32),
            pltpu.VMEM((NSA_GROUPS, QT, LANE), BF16),
            pltpu.VMEM((max(ns, 8), NSA_GROUPS * QT), F32),
        ],
        compiler_params=_cparams(("arbitrary", "arbitrary"), 56),
        name="nsa",
    )(qa, gn, kcmp, vcmp, kv, sslc, swin, bcmp, mapt, eall)


IDX_CHUNK = 512
FAST_PASSES = 10
LANE_SHIFT = LANE.bit_length() - 1


def _dsa_kernel(ql_ref, qi_ref, wi_ref, ki_ref, ckv_ref, strip_ref, wuv_ref, tri_ref, o_ref,
                q_ref, idx_ref, tmp_ref, s_ref, p_ref, m_ref, l_ref, acc_ref, *, t, k_sel):
    i = pl.program_id(1)
    q0 = i * QT
    nch = (i >> 2) + 1
    hrows = DSA_HEADS * QT
    sub = IDX_CHUNK // LANE
    halves = QT // RB

    for h in range(IDX_HEADS):
        q_ref[h * QT:(h + 1) * QT, :] = qi_ref[:, h * LANE:(h + 1) * LANE]
    tq = q0 + lax.broadcasted_iota(I32, (QT, IDX_CHUNK), 0)
    col = lax.broadcasted_iota(I32, (QT, IDX_CHUNK), 1)

    def idx_body(c, carry):
        c0 = pl.multiple_of(c * IDX_CHUNK, IDX_CHUNK)
        d = jnp.maximum(_dot_nt(q_ref[0:IDX_HEADS * QT, :], ki_ref[pl.ds(c0, IDX_CHUNK), :]), 0.0)
        acc = d[0:QT] * wi_ref[:, 0:1]
        for h in range(1, IDX_HEADS):
            acc = acc + d[h * QT:(h + 1) * QT] * wi_ref[:, h:h + 1]
        idx_ref[:, pl.ds(c0, IDX_CHUNK)] = jnp.where(col + c0 <= tq, acc, NEG)
        return carry

    lax.fori_loop(0, nch, idx_body, 0)

    zeros = jnp.zeros((QT, LANE), F32)

    def scan(fn, init):
        def body(c, carry):
            for s in range(sub):
                off = pl.multiple_of(c * IDX_CHUNK + s * LANE, LANE)
                carry = fn(idx_ref[:, pl.ds(off, LANE)], off, carry)
            return carry
        return lax.fori_loop(0, nch, body, init)

    def search(_):
        kf = float(k_sel)
        big = jnp.full((QT, LANE), BIG, F32)

        def count_gt(pivot):
            pb = jnp.broadcast_to(pivot, (QT, LANE))
            cnt = scan(lambda x, off, c: c + jnp.where(x > pb, 1.0, 0.0), zeros)
            return jnp.sum(cnt, axis=-1, keepdims=True)

        def init_fn(x, off, carry):
            lo, hi, s1, s2 = carry
            ok = x > 0.5 * NEG
            xv = jnp.where(ok, x, 0.0)
            return jnp.minimum(lo, jnp.where(ok, x, BIG)), jnp.maximum(hi, x), s1 + xv, s2 + xv * xv

        row_min, row_max, s1, s2 = scan(init_fn, (big, -big, zeros, zeros))
        row_min = jnp.min(row_min, axis=-1, keepdims=True)
        row_max = jnp.max(row_max, axis=-1, keepdims=True)
        n_valid = (q0 + 1 + lax.broadcasted_iota(I32, (QT, 1), 0)).astype(F32)

        mean = jnp.sum(s1, axis=-1, keepdims=True) / n_valid
        var = jnp.maximum(jnp.sum(s2, axis=-1, keepdims=True) / n_valid - mean * mean, 0.0)
        q_tail = kf / n_valid
        upper = q_tail < 0.5
        tt = jnp.sqrt(-2.0 * jnp.log(jnp.where(upper, q_tail, 1.0 - q_tail)))
        z_abs = tt - ((0.010328 * tt + 0.802853) * tt + 2.515517) / (((0.001308 * tt + 0.189269) * tt + 1.432788) * tt + 1.0)
        pivot0 = mean + jnp.where(upper, z_abs, -z_abs) * jnp.sqrt(var)

        def fast_step(it, c):
            lo, hi, g_lo, g_hi, f_lo, f_hi, side, done_i = c
            done = done_i > 0
            mid = jnp.where(it == 0, pivot0, lo + (hi - lo) * (g_lo / (g_lo - g_hi)))
            mid = jnp.where((mid > lo) & (mid < hi), mid, lo + (hi - lo) * 0.5)
            cnt = count_gt(mid)
            hit = cnt == kf
            up = cnt > kf
            move_lo = up | hit
            move_hi = jnp.logical_not(up)
            g_lo_n = jnp.where(up, cnt - kf, jnp.where(side == 2, g_lo * 0.5, g_lo))
            g_hi_n = jnp.where(up, jnp.where(side == 1, g_hi * 0.5, g_hi), cnt - kf)
            keep_old = done | hit
            return (jnp.where(done, lo, jnp.where(move_lo, mid, lo)), jnp.where(done, hi, jnp.where(move_hi, mid, hi)),
                    jnp.where(keep_old, g_lo, g_lo_n), jnp.where(keep_old, g_hi, g_hi_n),
                    jnp.where(done | jnp.logical_not(up), f_lo, cnt), jnp.where(done | up | hit, f_hi, cnt),
                    jnp.where(up, 1, 2), jnp.where(done | hit, 1, 0))

        lo0 = row_min - (jnp.abs(row_min) + 1.0)
        init = (lo0, row_max, n_valid - kf, jnp.full((QT, 1), -kf, F32), n_valid, jnp.zeros((QT, 1), F32),
                jnp.zeros((QT, 1), I32), jnp.zeros((QT, 1), I32))
        lo, hi, _, _, f_lo, f_hi, _, done_i = lax.fori_loop(0, FAST_PASSES, fast_step, init)
        done = done_i > 0

        def exact(_):
            lob = jnp.broadcast_to(lo, (QT, LANE))
            hib = jnp.broadcast_to(hi, (QT, LANE))

            def snap_fn(x, off, c):
                a, b_ = c
                return (jnp.minimum(a, jnp.where(x > lob, x, BIG)), jnp.maximum(b_, jnp.where(x > hib, -BIG, x)))

            a, b_ = scan(snap_fn, (big, -big))
            lo_d = jnp.where(done, lo, jnp.min(a, axis=-1, keepdims=True))
            hi_d = jnp.where(done, hi, jnp.max(b_, axis=-1, keepdims=True))

            def cond(c):
                return jnp.max(jnp.where(c[0] < c[1], 1, 0)) > 0

            def step(c):
                lo, hi, c_lo, c_hi = c
                mid = lo + (hi - lo) * 0.5
                mid = jnp.where(mid < hi, mid, lo)
                midb = jnp.broadcast_to(mid, (QT, LANE))

                def fn(x, off, cc):
                    cnt, amin, bmax = cc
                    gt = x > midb
                    return (cnt + jnp.where(gt, 1.0, 0.0), jnp.minimum(amin, jnp.where(gt, x, BIG)),
                            jnp.maximum(bmax, jnp.where(gt, -BIG, x)))

                cnt, amin, bmax = scan(fn, (zeros, big, -big))
                cnt = jnp.sum(cnt, axis=-1, keepdims=True)
                amin = jnp.min(amin, axis=-1, keepdims=True)
                bmax = jnp.max(bmax, axis=-1, keepdims=True)
                up = cnt >= kf
                same = lo >= hi
                return (jnp.where(same | jnp.logical_not(up), lo, amin), jnp.where(same | up, hi, bmax),
                        jnp.where(same | jnp.logical_not(up), c_lo, cnt), jnp.where(same | up, c_hi, cnt))

            thr, _, c_ge, c_gt = lax.while_loop(cond, step, (lo_d, hi_d, f_lo, f_hi))
            return thr, c_ge, c_gt

        all_done = jnp.min(done_i) > 0
        thr, c_ge, c_gt = lax.cond(all_done, lambda _: (lo, f_lo, f_hi), exact, 0)
        c_ge = jnp.where(done, kf, c_ge)
        c_gt = jnp.where(done, kf, c_gt)
        thrb = jnp.broadcast_to(thr, (QT, LANE))
        need = k_sel - c_gt

        def tie_search(_):
            lane = lax.broadcasted_iota(I32, (QT, LANE), 1)
            lanef = lane.astype(F32)

            def grp_fn(x, off, g_cnt):
                c = jnp.sum(jnp.where(x == thrb, 1.0, 0.0), axis=-1, keepdims=True)
                return jnp.where(lane == (off >> LANE_SHIFT), c, g_cnt)

            g_cum = _dot(scan(grp_fn, zeros).astype(BF16), tri_ref[...])
            g_star = jnp.sum(jnp.where(g_cum < need, 1.0, 0.0), axis=-1, keepdims=True)
            before = jnp.sum(jnp.where(lanef == g_star - 1.0, g_cum, 0.0), axis=-1, keepdims=True)

            def slab_fn(x, off, slab):
                return jnp.where(g_star == (off >> LANE_SHIFT).astype(F32), x, slab)

            slab = scan(slab_fn, jnp.full((QT, LANE), NEG, F32))
            pre = _dot(jnp.where(slab == thrb, 1.0, 0.0).astype(BF16), tri_ref[...])
            lane_cut = jnp.sum(jnp.where(pre < need - before, 1.0, 0.0), axis=-1, keepdims=True)
            return (g_star * LANE + lane_cut).astype(I32)

        any_tie = jnp.max(jnp.where(c_ge > k_sel, 1, 0)) > 0
        p_cut = lax.cond(any_tie, tie_search, lambda _: jnp.full((QT, 1), t, I32), 0)
        return thr, jnp.where(done, -1, p_cut)

    thr, p_cut = lax.cond(q0 >= k_sel, search,
                          lambda _: (jnp.full((QT, 1), M_FLOOR, F32), jnp.full((QT, 1), t, I32)), 0)
    thrk = jnp.broadcast_to(thr, (QT, KT))
    pcutk = jnp.broadcast_to(p_cut, (QT, KT))
    colk = lax.broadcasted_iota(I32, (QT, KT), 1)

    for h in range(DSA_HEADS):
        q_ref[h * QT:(h + 1) * QT, :] = ql_ref[:, h * LANE:(h + 1) * LANE]
    m_ref[...] = jnp.full((hrows, LANE), M_FLOOR, F32)
    l_ref[...] = jnp.zeros((hrows, LANE), F32)
    acc_ref[...] = jnp.zeros((hrows, LANE), F32)
    blocks = [(h, half) for h in range(DSA_HEADS) for half in range(halves)]

    n_far = _far_tiles(i)
    n_kt = (i >> (TPK.bit_length() - 1)) + 1

    def max_step(kt, near):
        k0 = pl.multiple_of(kt * KT, KT)
        s = _dot_nt(q_ref[...], ckv_ref[pl.ds(k0, KT), :])
        x = idx_ref[:, pl.ds(k0, KT)]
        keep = (x > thrk) | ((x == thrk) & (colk + k0 <= pcutk))
        selm = jnp.where(keep, 0.0, NEG)
        for h, half in blocks:
            qrows = slice(half * RB, (half + 1) * RB)
            rows = slice(h * QT + half * RB, h * QT + (half + 1) * RB)
            z = s[rows] + selm[qrows]
            if near:
                z = z + strip_ref[h, qrows, pl.ds(_strip_offset(i, kt), KT)]
            s_ref[rows, pl.ds(k0, KT)] = z
            m_ref[rows, :] = jnp.maximum(m_ref[rows, :], _lane_groups(jnp.maximum, z))

    _for_tiles(0, n_far, lambda kt: max_step(kt, False))
    _for_tiles(n_far, n_kt, lambda kt: max_step(kt, True))
    m_ref[...] = jnp.broadcast_to(jnp.max(m_ref[...], axis=-1, keepdims=True), (hrows, LANE))

    def sum_step(kt):
        k0 = pl.multiple_of(kt * KT, KT)
        ps = []
        for h, half in blocks:
            rows = slice(h * QT + half * RB, h * QT + (half + 1) * RB)
            m = m_ref[rows, :]
            p = jnp.exp2(s_ref[rows, pl.ds(k0, KT)] - jnp.concatenate([m] * (KT // LANE), axis=1))
            l_ref[rows, :] += _lane_groups(jnp.add, p)
            ps.append(p.astype(BF16))
        acc_ref[...] += _dot(jnp.concatenate(ps, axis=0), ckv_ref[pl.ds(k0, KT), :])

    _for_tiles(0, n_kt, sum_step)
    for h in range(DSA_HEADS):
        rows = slice(h * QT, (h + 1) * QT)
        l = jnp.sum(l_ref[rows, :], axis=-1, keepdims=True)
        o_lat = (acc_ref[rows, :] / jnp.maximum(l, 1e-30)).astype(BF16)
        o_ref[:, h * LANE:(h + 1) * LANE] = _dot(o_lat, wuv_ref[h]).astype(BF16)


def _dsa(qlat, qidx, widx, kidx, ckvn, strip, wuv_pad, b, t):
    n_qt = t // QT
    k_sel = min(IDX_TOPK_MAX, t // 4)
    assert k_sel % QT == 0 and t % IDX_CHUNK == 0 and t // LANE <= LANE
    hrows = DSA_HEADS * QT
    tri = jnp.asarray(np.triu(np.ones((LANE, LANE), np.float32)), BF16)

    def full(a):
        return pl.BlockSpec(a.shape, lambda bi, i: (0,) * a.ndim)

    return pl.pallas_call(
        functools.partial(_dsa_kernel, t=t, k_sel=k_sel),
        grid=(b, n_qt),
        in_specs=[
            pl.BlockSpec((None, QT, DSA_HEADS * LANE), lambda bi, i: (bi, i, 0)),
            pl.BlockSpec((None, QT, IDX_HEADS * LANE), lambda bi, i: (bi, i, 0)),
            pl.BlockSpec((None, QT, LANE), lambda bi, i: (bi, i, 0)),
            pl.BlockSpec((None, t, LANE), lambda bi, i: (bi, 0, 0)),
            pl.BlockSpec((None, t, LANE), lambda bi, i: (bi, 0, 0)),
            pl.BlockSpec(strip.shape, lambda bi, i: (0, 0, 0), pipeline_mode=pl.Buffered(1)),
            full(wuv_pad), full(tri),
        ],
        out_specs=pl.BlockSpec((None, QT, DSA_HEADS * LANE), lambda bi, i: (bi, i, 0)),
        out_shape=jax.ShapeDtypeStruct((b, t, DSA_HEADS * LANE), BF16),
        scratch_shapes=[
            pltpu.VMEM((hrows, LANE), BF16),
            pltpu.VMEM((QT, t), F32),
            pltpu.VMEM((hrows, KT), F32),
            pltpu.VMEM((hrows, t), F32),
            pltpu.VMEM((hrows, KT), BF16),
            pltpu.VMEM((hrows, LANE), F32),
            pltpu.VMEM((hrows, LANE), F32),
            pltpu.VMEM((hrows, LANE), F32),
        ],
        compiler_params=_cparams(("arbitrary", "arbitrary"), 56),
        name="dsa",
    )(qlat, qidx, widx, kidx, ckvn, strip, wuv_pad, tri)


RT_GRP = 0
RT_EXP = 32


def _layer_norm(y, g, b):
    mu = jnp.mean(y, axis=-1, keepdims=True)
    yc = y - mu
    var = jnp.mean(yc * yc, axis=-1, keepdims=True)
    return yc * lax.rsqrt(var + 1e-5) * g + b


def _post_kernel(x_ref, oa_ref, ob_ref, sga_ref, sgb_ref, wa_ref, wb_ref, wo_ref, g_ref, b_ref, wr_ref, br_ref,
                 h_ref, rt_ref, rw_ref, *, alpha):
    rows = x_ref.shape[0]
    merged = (sga_ref[...].astype(F32) * _dot(oa_ref[...], wa_ref[...])
              + sgb_ref[...].astype(F32) * _dot(ob_ref[...], wb_ref[...]))
    y = alpha * x_ref[...] + _dot(merged.astype(BF16), wo_ref[...])
    h = _layer_norm(y, g_ref[...], b_ref[...])
    h_ref[...] = h

    h_hi = h.astype(BF16)
    h_lo = (h - h_hi.astype(F32)).astype(BF16)
    zz = _dot(h_hi, wr_ref[...])
    z = zz[:, 0:LANE] + zz[:, LANE:2 * LANE] + _dot(h_lo, wr_ref[:, 0:LANE]) + br_ref[...]
    lane = lax.broadcasted_iota(I32, (rows, LANE), 1)
    is_g = lane < N_EXPERT_GROUPS
    zg = jnp.where(is_g, z, -BIG)
    gmax = jnp.max(zg, axis=-1, keepdims=True)
    g_sel = jnp.min(jnp.where(is_g & (z == gmax), lane, LANE), axis=-1, keepdims=True)
    p_grp = 1.0 / jnp.sum(jnp.where(is_g, jnp.exp(zg - gmax), 0.0), axis=-1, keepdims=True)
    in_grp = (lane >= RT_EXP) & (lane < RT_EXP + N_EXPERTS) & (((lane - RT_EXP) >> 3) == g_sel)
    ze = jnp.where(in_grp, z, -BIG)
    m1 = jnp.max(ze, axis=-1, keepdims=True)
    i1 = jnp.min(jnp.where(in_grp & (z == m1), lane, LANE), axis=-1, keepdims=True)
    ze2 = jnp.where(lane == i1, -BIG, ze)
    m2 = jnp.max(ze2, axis=-1, keepdims=True)
    i2 = jnp.min(jnp.where(in_grp & (lane != i1) & (z == m2), lane, LANE), axis=-1, keepdims=True)
    e21 = jnp.exp(m2 - m1)
    den = 1.0 + e21
    w1 = p_grp * (1.0 / den)
    w2 = p_grp * (e21 / den)
    rt_ref[...] = jnp.where(lane == 0, i1 - RT_EXP, jnp.where(lane == 1, i2 - RT_EXP, 0))
    rw_ref[...] = jnp.where(lane == 0, w1, jnp.where(lane == 1, w2, 0.0))


def _post(x2, oa, ob, sga, sgb, wa, wb, wo, g, b_, wr, br, alpha, rows=256):
    n, d = x2.shape

    def row(w):
        return pl.BlockSpec((rows, w), lambda i: (i, 0))

    def full(a):
        return pl.BlockSpec(a.shape, lambda i: (0,) * a.ndim)

    return pl.pallas_call(
        functools.partial(_post_kernel, alpha=alpha),
        grid=(n // rows,),
        in_specs=[row(d), row(1024), row(1024), row(1024), row(1024), full(wa), full(wb), full(wo), full(g),
                  full(b_), full(wr), full(br)],
        out_specs=[row(d), row(LANE), row(LANE)],
        out_shape=[jax.ShapeDtypeStruct((n, d), F32), jax.ShapeDtypeStruct((n, LANE), I32),
                   jax.ShapeDtypeStruct((n, LANE), F32)],
        compiler_params=_cparams(("arbitrary",), 48),
        name="post",
    )(x2, oa, ob, sga, sgb, wa, wb, wo, g, b_, wr, br)


def _onehots(rt_ref, rows):
    lane = lax.broadcasted_iota(I32, (rows, LANE), 1)
    oh0 = jnp.where(lane == rt_ref[:, 0:1], 1.0, 0.0)
    oh1 = jnp.where(lane == rt_ref[:, 1:2], 1.0, 0.0)
    return oh0, oh1


def _rank_kernel(rt_ref, tri_ref, rank_ref, cnt_ref, carry_ref):
    rows = rt_ref.shape[0]

    @pl.when(pl.program_id(0) == 0)
    def _():
        carry_ref[...] = jnp.zeros_like(carry_ref)

    oh0, oh1 = _onehots(rt_ref, rows)
    both = oh0 + oh1
    before = _dot(tri_ref[...], both.astype(BF16)) + carry_ref[0:1, :]
    r0 = jnp.sum(oh0 * before, axis=-1, keepdims=True)
    r1 = jnp.sum(oh1 * before, axis=-1, keepdims=True)
    lane = lax.broadcasted_iota(I32, (rows, LANE), 1)
    rank_ref[...] = jnp.where(lane == 0, r0, jnp.where(lane == 1, r1, 0.0))
    carry_ref[...] = carry_ref[...] + jnp.sum(both, axis=0, keepdims=True)
    cnt_ref[...] = carry_ref[...]


def _moe_rank(rt, rows=512):
    n = rt.shape[0]
    tri = jnp.asarray(np.tril(np.ones((rows, rows), np.float32), -1), BF16)
    return pl.pallas_call(
        _rank_kernel,
        grid=(n // rows,),
        in_specs=[pl.BlockSpec((rows, LANE), lambda i: (i, 0)), pl.BlockSpec((rows, rows), lambda i: (0, 0))],
        out_specs=[pl.BlockSpec((rows, LANE), lambda i: (i, 0)), pl.BlockSpec((8, LANE), lambda i: (0, 0))],
        out_shape=[jax.ShapeDtypeStruct((n, LANE), F32), jax.ShapeDtypeStruct((8, LANE), F32)],
        scratch_shapes=[pltpu.VMEM((8, LANE), F32)],
        compiler_params=_cparams(("arbitrary",)),
        name="moe_rank",
    )(rt, tri)


def _lane_cumsum(v):
    lane = lax.broadcasted_iota(I32, v.shape, 1)
    s = 1
    while s < LANE:
        v = v + jnp.where(lane >= s, pltpu.roll(v, s, 1), 0.0)
        s *= 2
    return v


def _dest_kernel(rt_ref, rank_ref, cnt_ref, dest_ref, bexp_ref, *, n_blk_pad):
    rows = rt_ref.shape[0]
    lane8 = lax.broadcasted_iota(I32, (8, LANE), 1)
    cnt = jnp.where(lane8 < N_EXPERTS, cnt_ref[...], 0.0)
    padded = jnp.floor((cnt + (EXPERT_BLOCK - 1)) * (1.0 / EXPERT_BLOCK)) * EXPERT_BLOCK
    pend = _lane_cumsum(padded)
    poff = (pend - padded)[0:1, :]
    oh0, oh1 = _onehots(rt_ref, rows)
    d0 = jnp.sum(oh0 * poff, axis=-1, keepdims=True) + rank_ref[:, 0:1]
    d1 = jnp.sum(oh1 * poff, axis=-1, keepdims=True) + rank_ref[:, 1:2]
    lane = lax.broadcasted_iota(I32, (rows, LANE), 1)
    dest_ref[...] = jnp.where(lane == 0, d0, jnp.where(lane == 1, d1, 0.0)).astype(I32)

    lane_b = lax.broadcasted_iota(I32, (n_blk_pad, LANE), 1)
    start = (lax.broadcasted_iota(I32, (n_blk_pad, LANE), 0) * EXPERT_BLOCK).astype(F32)
    hit = jnp.where((lane_b < N_EXPERTS) & (pend[0:1, :] <= start), 1.0, 0.0)
    e_blk = jnp.minimum(jnp.sum(hit, axis=-1, keepdims=True), N_EXPERTS - 1.0)
    used = jnp.max(pend[0:1, :], axis=-1, keepdims=True) * (1.0 / EXPERT_BLOCK)
    bexp_ref[...] = jnp.where(lane_b == 0, e_blk, jnp.where(lane_b == 1, used, 0.0)).astype(I32)


def _moe_dest(rt, rank, cnt, n_blk, rows=1024):
    n = rt.shape[0]
    n_blk_pad = -(-n_blk // 8) * 8
    return pl.pallas_call(
        functools.partial(_dest_kernel, n_blk_pad=n_blk_pad),
        grid=(n // rows,),
        in_specs=[pl.BlockSpec((rows, LANE), lambda i: (i, 0)), pl.BlockSpec((rows, LANE), lambda i: (i, 0)),
                  pl.BlockSpec((8, LANE), lambda i: (0, 0))],
        out_specs=[pl.BlockSpec((rows, LANE), lambda i: (i, 0)), pl.BlockSpec((n_blk_pad, LANE), lambda i: (0, 0))],
        out_shape=[jax.ShapeDtypeStruct((n, LANE), I32), jax.ShapeDtypeStruct((n_blk_pad, LANE), I32)],
        compiler_params=_cparams(("arbitrary",)),
        name="moe_dest",
    )(rt, rank, cnt)


MOE_ROWS = 512
DMA_UNROLL = 8


def _dispatch_kernel(dest_ref, h_ref, xin_ref, xpad_ref, sem):
    del xin_ref

    def row_copy(r, d):
        return pltpu.make_async_copy(h_ref.at[pl.ds(r, 1)], xpad_ref.at[pl.ds(d, 1)], sem)

    def start(r, carry):
        for j in range(EXPERT_TOPK):
            row_copy(r, dest_ref[EXPERT_TOPK * r + j]).start(priority=j)
        return carry

    lax.fori_loop(0, MOE_ROWS, start, 0, unroll=DMA_UNROLL)

    def wait(r, carry):
        for j in range(EXPERT_TOPK):
            row_copy(r, dest_ref[EXPERT_TOPK * r + j]).wait()
        return carry

    lax.fori_loop(0, MOE_ROWS, wait, 0, unroll=DMA_UNROLL)


def _moe_dispatch(dest_flat, h, n_slots):
    n, d = h.shape
    zeros = jnp.zeros((n_slots, d), h.dtype)
    return pl.pallas_call(
        _dispatch_kernel,
        grid=(n // MOE_ROWS,),
        in_specs=[pl.BlockSpec((EXPERT_TOPK * MOE_ROWS,), lambda i: (i,), memory_space=pltpu.SMEM),
                  pl.BlockSpec((MOE_ROWS, d), lambda i: (i, 0)),
                  pl.BlockSpec(memory_space=pl.ANY)],
        out_specs=pl.BlockSpec(memory_space=pl.ANY),
        out_shape=jax.ShapeDtypeStruct((n_slots, d), h.dtype),
        scratch_shapes=[pltpu.SemaphoreType.DMA(())],
        input_output_aliases={2: 0},
        compiler_params=_cparams(("arbitrary",)),
        name="moe_dispatch",
    )(dest_flat, h, zeros)


def _expert_kernel(bexp_ref, used_ref, x_ref, wg_ref, wu_ref, wd_ref, y_ref):
    blk = pl.program_id(0)

    @pl.when(blk < used_ref[0])
    def _():
        xb = x_ref[...].astype(BF16)
        gate = _dot(xb, wg_ref[...].astype(BF16))
        up = _dot(xb, wu_ref[...].astype(BF16))
        act = (jax.nn.silu(gate) * up).astype(BF16)
        y_ref[...] = _dot(act, wd_ref[...].astype(BF16))

    @pl.when(blk >= used_ref[0])
    def _():
        y_ref[...] = jnp.zeros_like(y_ref)


def _moe_experts(bexp, used, xpad, w_gate, w_up, w_down):
    n_slots, d = xpad.shape
    n_blk = n_slots // EXPERT_BLOCK
    de = w_gate.shape[-1]
    grid_spec = pltpu.PrefetchScalarGridSpec(
        num_scalar_prefetch=2,
        grid=(n_blk,),
        in_specs=[
            pl.BlockSpec((EXPERT_BLOCK, d), lambda i, be, us: (i, 0)),
            pl.BlockSpec((None, d, de), lambda i, be, us: (be[i], 0, 0)),
            pl.BlockSpec((None, d, de), lambda i, be, us: (be[i], 0, 0)),
            pl.BlockSpec((None, de, d), lambda i, be, us: (be[i], 0, 0)),
        ],
        out_specs=pl.BlockSpec((EXPERT_BLOCK, d), lambda i, be, us: (i, 0)),
    )
    return pl.pallas_call(
        _expert_kernel,
        grid_spec=grid_spec,
        out_shape=jax.ShapeDtypeStruct((n_slots, d), F32),
        compiler_params=_cparams(("arbitrary",), 48),
        name="moe_experts",
    )(bexp, used, xpad, w_gate, w_up, w_down)


def _combine_kernel(dest_ref, h_ref, rw_ref, g_ref, b_ref, y_ref, o_ref, buf_ref, sem, *, alpha):
    def row_copy(r, j):
        return pltpu.make_async_copy(y_ref.at[pl.ds(dest_ref[EXPERT_TOPK * r + j], 1)],
                                     buf_ref.at[j, pl.ds(r, 1)], sem)

    def start(r, carry):
        for j in range(EXPERT_TOPK):
            row_copy(r, j).start(priority=j)
        return carry

    lax.fori_loop(0, MOE_ROWS, start, 0, unroll=DMA_UNROLL)

    def wait(r, carry):
        for j in range(EXPERT_TOPK):
            row_copy(r, j).wait()
        return carry

    lax.fori_loop(0, MOE_ROWS, wait, 0, unroll=DMA_UNROLL)
    moe = buf_ref[0] * rw_ref[:, 0:1] + buf_ref[1] * rw_ref[:, 1:2]
    o_ref[...] = _layer_norm(alpha * h_ref[...] + moe, g_ref[...], b_ref[...])


def _moe_combine(dest_flat, h, rw, g, b_, ypad, alpha):
    n, d = h.shape
    return pl.pallas_call(
        functools.partial(_combine_kernel, alpha=alpha),
        grid=(n // MOE_ROWS,),
        in_specs=[pl.BlockSpec((EXPERT_TOPK * MOE_ROWS,), lambda i: (i,), memory_space=pltpu.SMEM),
                  pl.BlockSpec((MOE_ROWS, d), lambda i: (i, 0)),
                  pl.BlockSpec((MOE_ROWS, LANE), lambda i: (i, 0)),
                  pl.BlockSpec((1, d), lambda i: (0, 0)),
                  pl.BlockSpec((1, d), lambda i: (0, 0)),
                  pl.BlockSpec(memory_space=pl.ANY)],
        out_specs=pl.BlockSpec((MOE_ROWS, d), lambda i: (i, 0)),
        out_shape=jax.ShapeDtypeStruct((n, d), F32),
        scratch_shapes=[pltpu.VMEM((EXPERT_TOPK, MOE_ROWS, d), F32), pltpu.SemaphoreType.DMA(())],
        compiler_params=_cparams(("arbitrary",), 48),
        name="moe_combine",
    )(dest_flat, h, rw, g, b_, ypad)


def _cmp_map_t(t):
    nc = (t - CMP_BLOCK) // CMP_STRIDE + 1
    ns = t // SLC_BLOCK
    ncp = t // CMP_STRIDE
    cs = CMP_STRIDE * np.arange(nc)[:, None]
    ss = SLC_BLOCK * np.arange(ns)[None, :]
    ov = np.minimum(cs + CMP_BLOCK, ss + SLC_BLOCK) - np.maximum(cs, ss)
    m = np.clip(ov, 0, None).astype(np.float32) / CMP_STRIDE
    out = np.zeros((ns, ncp), np.float32)
    out[:, :nc] = m.T
    return jnp.asarray(out, BF16)


def _block_expand(t):
    ns = t // SLC_BLOCK
    rows = max(ns, LANE)
    e = np.zeros((rows, t), np.float32)
    e[np.arange(t) // SLC_BLOCK, np.arange(t)] = 1.0
    return jnp.asarray(e, BF16)


def _pad_head_rows(w, n_heads):
    wh = w.reshape(n_heads, HEAD_DIM, w.shape[-1])
    return jnp.concatenate([wh, jnp.zeros_like(wh)], axis=1).reshape(n_heads * LANE, w.shape[-1]).astype(BF16)


def kernel(x, w_in, cmp_pe_k, cmp_pe_v, cmp_w1_k, cmp_w2_k, cmp_w1_v, cmp_w2_v, ckv_norm_g, w_uk, w_uv, rel_bias,
           w_branch_a, w_branch_b, w_out, ln1_g, ln1_b, w_grp, b_grp, w_rtr, b_rtr, w_gate, w_up, w_down, ln2_g,
           ln2_b):
    b, t, d = x.shape
    n = b * t
    depth = w_in.shape[0]
    alpha = (2.0 * depth) ** 0.25
    assert t % 512 == 0 and t >= WIN_KEYS and n % MOE_ROWS == 0

    sslc = _bias_strip(rel_bias, NSA_HEADS, 0, STRIP_A, STRIP_W, None, True)
    sdsa = _bias_strip(rel_bias, DSA_HEADS, NSA_HEADS, STRIP_A, STRIP_W, None, True)
    swin = _bias_strip(rel_bias, NSA_HEADS, 0, WIN_A, WIN_W, WINDOW, False)
    bcmp = _cmp_bias(rel_bias, t)
    mapt = _cmp_map_t(t)
    eall = _block_expand(t)

    n_a = n * EXPERT_TOPK
    n_blk = -(-n_a // EXPERT_BLOCK) + N_EXPERTS
    n_slots = n_blk * EXPERT_BLOCK

    h = x.reshape(n, d)
    for l in range(depth):
        w_pad = _proj_weights(w_in[l])
        wuk = w_uk[l]
        z = jnp.zeros_like(wuk[0])
        wuk_pairs = jnp.stack([
            jnp.concatenate([jnp.concatenate([wuk[2 * k], z], axis=1), jnp.concatenate([z, wuk[2 * k + 1]], axis=1)],
                            axis=0) for k in range(DSA_HEADS // 2)]).astype(BF16)
        wuv_pad = jnp.pad(w_uv[l], ((0, 0), (0, 0), (0, LANE - HEAD_DIM))).astype(BF16)
        wk = _cmp_weights(cmp_w1_k[l], cmp_w2_k[l], cmp_pe_k[l], HEAD_DIM)
        wv = _cmp_weights(cmp_w1_v[l], cmp_w2_v[l], cmp_pe_v[l], LANE)
        wa_pad = _pad_head_rows(w_branch_a[l], NSA_HEADS)
        wb_pad = _pad_head_rows(w_branch_b[l], DSA_HEADS)
        wr = jnp.zeros((d, LANE), F32).at[:, RT_GRP:RT_GRP + N_EXPERT_GROUPS].set(w_grp[l])
        wr = wr.at[:, RT_EXP:RT_EXP + N_EXPERTS].set(w_rtr[l])
        wr_hi = wr.astype(BF16)
        wr = jnp.concatenate([wr_hi, (wr - wr_hi.astype(F32)).astype(BF16)], axis=1)
        br = jnp.zeros((1, LANE), F32).at[0, RT_GRP:RT_GRP + N_EXPERT_GROUPS].set(b_grp[l])
        br = br.at[0, RT_EXP:RT_EXP + N_EXPERTS].set(b_rtr[l])

        (qa, kc, vc, kv, gn, qlat, ckvn, qidx, kidx, widx, sga, sgb) = _proj(
            h, w_pad, wuk_pairs, ckv_norm_g[l].reshape(1, KV_RANK))
        kcmp, vcmp = _compress(kc, vc, wk, wv, b, t)

        def b3(a):
            return a.reshape(b, t, a.shape[-1])

        oa = _nsa(b3(qa), b3(gn), kcmp, vcmp, b3(kv), sslc, swin, bcmp, mapt, eall, b, t)
        ob = _dsa(b3(qlat), b3(qidx), b3(widx), b3(kidx), b3(ckvn), sdsa, wuv_pad, b, t)

        h1, rt, rw = _post(h, oa.reshape(n, -1), ob.reshape(n, -1), sga, sgb, wa_pad, wb_pad, w_out[l].astype(BF16),
                           ln1_g[l].reshape(1, d), ln1_b[l].reshape(1, d), wr, br, alpha)

        rank, cnt = _moe_rank(rt)
        dest, bexp = _moe_dest(rt, rank, cnt, n_blk)
        dest_flat = dest[:, :EXPERT_TOPK].reshape(n_a)
        xpad = _moe_dispatch(dest_flat, h1, n_slots)
        ypad = _moe_experts(bexp[:n_blk, 0], bexp[:1, 1], xpad, w_gate[l], w_up[l], w_down[l])
        h = _moe_combine(dest_flat, h1, rw, ln2_g[l].reshape(1, d), ln2_b[l].reshape(1, d), ypad, alpha)
    return h.reshape(b, t, d)
```

```python
import functools
import math

import numpy as np
import jax
import jax.numpy as jnp
from jax import lax
from jax.experimental import pallas as pl
from jax.experimental.pallas import tpu as pltpu

F32 = jnp.float32
BF16 = jnp.bfloat16
I32 = jnp.int32

D_MODEL = 1024
HEAD_DIM = 64
NSA_HEADS = 8
NSA_GROUPS = 2
NSA_HPG = NSA_HEADS // NSA_GROUPS
CMP_BLOCK = 32
CMP_STRIDE = 16
CMP_HIDDEN = 256
SLC_BLOCK = 64
SLC_TOPN = 16
SLC_FORCE = 1e4
WINDOW = 512
DSA_HEADS = 8
KV_RANK = 128
IDX_HEADS = 4
IDX_DIM = 64
IDX_TOPK_MAX = 256
REL_BUCKETS = 32
REL_EXACT = 16
REL_MAX_DIST = 1024
N_EXPERT_GROUPS = 4
EXPERTS_PER_GROUP = 8
N_EXPERTS = N_EXPERT_GROUPS * EXPERTS_PER_GROUP
EXPERT_TOPK = 2
D_EXPERT = 256
EXPERT_BLOCK = 256
NSA_WIDTH = NSA_HEADS * HEAD_DIM
DSA_WIDTH = DSA_HEADS * HEAD_DIM
NEG = -1e30
SPLIT_SIZES = (NSA_WIDTH,) + (NSA_GROUPS * HEAD_DIM,) * 6 + (
    NSA_HEADS * 3, DSA_WIDTH, KV_RANK, IDX_HEADS * IDX_DIM, IDX_DIM, IDX_HEADS, D_MODEL, D_MODEL)
SPLIT_POINTS = tuple(int(v) for v in np.cumsum(SPLIT_SIZES)[:-1])

LANE = 128
QT = 128
KT = 512
TPK = KT // QT
RB = 64
M_FLOOR = -1e29
BIG = 3e38
LOG2E = math.log2(math.e)

FAR_TILES = 11
STRIP_A = FAR_TILES * QT
STRIP_W = STRIP_A + KT
assert REL_EXACT + int(math.log((STRIP_A - KT + 1) / REL_EXACT) / math.log(REL_MAX_DIST / REL_EXACT)
                       * (REL_BUCKETS - REL_EXACT)) >= REL_BUCKETS - 1
WIN_A = WINDOW
WIN_KEYS = WINDOW + QT
WIN_W = WIN_A + WIN_KEYS

O_QA, O_KC, O_VC, O_KV, O_GN, O_QB, O_CKV, O_QIDX, O_KIDX, O_WIDX, O_GA, O_GB, PROJ_W = (
    int(v) for v in np.cumsum([0, 1024, 128, 128, 768, 128, 512, 128, 512, 128, 128, 1024, 1024]))
KV_KS, KV_VS, KV_KW, KV_VW, KV_W = 0, 128, 384, 512, 768
ONES_LANE = HEAD_DIM

_NT = (((1,), (1,)), ((), ()))


def _dot(a, b):
    return jnp.dot(a, b, preferred_element_type=F32)


def _dot_nt(a, b):
    return lax.dot_general(a, b, _NT, preferred_element_type=F32)


def _cparams(sem, vmem_mb=None):
    kw = dict(dimension_semantics=sem)
    if vmem_mb is not None:
        kw["vmem_limit_bytes"] = vmem_mb * 1024 * 1024
    return pltpu.CompilerParams(**kw)


def _proj_weights(w_in):
    (q_a, kc, vc, ks, vs, kw, vw, g_nsa, q_b, ckv, q_idx, k_idx, w_idx, ga, gb) = jnp.split(w_in, SPLIT_POINTS, axis=1)
    d = w_in.shape[0]
    scale = HEAD_DIM ** -0.5 * LOG2E
    z64 = jnp.zeros((d, NSA_HPG, HEAD_DIM), F32)
    qa = (q_a * scale).reshape(d, NSA_GROUPS, NSA_HPG, HEAD_DIM)
    qa_pad = jnp.concatenate([
        jnp.concatenate([qa[:, 0], z64], axis=-1).reshape(d, NSA_HPG * LANE),
        jnp.concatenate([z64, qa[:, 1]], axis=-1).reshape(d, NSA_HPG * LANE)], axis=1)
    qi = q_idx.reshape(d, IDX_HEADS, IDX_DIM)
    qi_pad = jnp.concatenate([qi, jnp.zeros_like(qi)], axis=-1).reshape(d, IDX_HEADS * LANE)

    def pad(a):
        return jnp.pad(a, ((0, 0), (0, LANE - a.shape[1])))

    def per_group(v):
        return jnp.concatenate([pad(v[:, :HEAD_DIM]), pad(v[:, HEAD_DIM:])], axis=1)

    w_idx_s = w_idx * (IDX_HEADS ** -0.5 * IDX_DIM ** -0.5)
    cols = [qa_pad, kc, vc, ks, per_group(vs), kw, per_group(vw), pad(g_nsa), q_b * scale, ckv, qi_pad, pad(k_idx),
            pad(w_idx_s), ga, gb]
    w = jnp.concatenate(cols, axis=1)
    assert w.shape[1] == PROJ_W
    return w.astype(BF16)


def _proj_kernel(x_ref, w_ref, wuk_ref, g_ref, qa_o, kc_o, vc_o, kv_o, gn_o, qlat_o, ckv_o, qidx_o, kidx_o,
                 widx_o, sga_o, sgb_o):
    xb = x_ref[...].astype(BF16)

    def mm(lo, n):
        return _dot(xb, w_ref[:, lo:lo + n])

    qa_o[...] = mm(O_QA, 1024).astype(BF16)
    kc_o[...] = mm(O_KC, 128).astype(BF16)
    vc_o[...] = mm(O_VC, 128).astype(BF16)
    lane = lax.broadcasted_iota(I32, (x_ref.shape[0], KV_W), 1)
    is_one = ((lane & (LANE - 1)) == ONES_LANE) & (((lane >= KV_VS) & (lane < KV_KW)) | (lane >= KV_VW))
    kv_o[...] = jnp.where(is_one, 1.0, mm(O_KV, KV_W)).astype(BF16)
    gn_o[...] = jax.nn.sigmoid(mm(O_GN, 128))
    qb = mm(O_QB, 512).astype(BF16)
    for k in range(DSA_HEADS // 2):
        qlat_o[:, 256 * k:256 * (k + 1)] = _dot(qb[:, 128 * k:128 * (k + 1)], wuk_ref[k]).astype(BF16)
    c = mm(O_CKV, 128)
    ms = jnp.mean(c * c, axis=-1, keepdims=True)
    ckv_o[...] = (c * lax.rsqrt(ms + 1e-6) * g_ref[...]).astype(BF16)
    qidx_o[...] = mm(O_QIDX, 512).astype(BF16)
    kidx_o[...] = mm(O_KIDX, 128).astype(BF16)
    widx_o[...] = mm(O_WIDX, 128)
    sga_o[...] = jax.nn.sigmoid(mm(O_GA, 1024)).astype(BF16)
    sgb_o[...] = jax.nn.sigmoid(mm(O_GB, 1024)).astype(BF16)


def _proj(x2, w_pad, wuk_pairs, ckv_g, rows=512):
    n, d = x2.shape
    widths = [(1024, BF16), (128, BF16), (128, BF16), (KV_W, BF16), (128, F32), (1024, BF16), (128, BF16),
              (512, BF16), (128, BF16), (128, F32), (1024, BF16), (1024, BF16)]
    return pl.pallas_call(
        _proj_kernel,
        grid=(n // rows,),
        in_specs=[
            pl.BlockSpec((rows, d), lambda i: (i, 0)),
            pl.BlockSpec((d, PROJ_W), lambda i: (0, 0)),
            pl.BlockSpec((DSA_HEADS // 2, 128, 256), lambda i: (0, 0, 0)),
            pl.BlockSpec((1, KV_RANK), lambda i: (0, 0)),
        ],
        out_specs=[pl.BlockSpec((rows, w), lambda i: (i, 0)) for w, _ in widths],
        out_shape=[jax.ShapeDtypeStruct((n, w), dt) for w, dt in widths],
        compiler_params=_cparams(("arbitrary",), 56),
        name="proj",
    )(x2, w_pad, wuk_pairs, ckv_g)


def _cmp_weights(w1, w2, pe, out_lanes):
    half = CMP_BLOCK // 2
    w1r = w1.reshape(CMP_BLOCK, HEAD_DIM, CMP_HIDDEN)
    eye = jnp.eye(NSA_GROUPS, dtype=F32)

    def expand(wl):
        return jnp.einsum('ldj,gh->lgdhj', wl, eye).reshape(half * NSA_GROUPS * HEAD_DIM, NSA_GROUPS * CMP_HIDDEN)

    top, bot = expand(w1r[:half]), expand(w1r[half:])

    def pe_rows(p):
        return jnp.broadcast_to(p[:, None, :], (half, NSA_GROUPS, HEAD_DIM)).reshape(1, -1)

    w2p = w2 if out_lanes == HEAD_DIM else jnp.pad(w2, ((0, 0), (0, out_lanes - HEAD_DIM)))
    w2bd = jnp.einsum('jd,gh->gjhd', w2p, eye).reshape(NSA_GROUPS * CMP_HIDDEN, NSA_GROUPS * out_lanes)
    return (top.astype(BF16), bot.astype(BF16), pe_rows(pe[:half]).astype(BF16), pe_rows(pe[half:]).astype(BF16),
            w2bd.astype(BF16))


def _compress_kernel(hk_ref, hv_ref, kt_ref, kb_ref, kpt_ref, kpb_ref, k2_ref, vt_ref, vb_ref, vpt_ref, vpb_ref,
                     v2_ref, ko_ref, vo_ref):
    ncp = hk_ref.shape[0]

    def one(h_ref, top_ref, bot_ref, pt_ref, pb_ref, w2_ref, o_ref):
        h = h_ref[...]
        a = _dot(h, top_ref[...])
        b = _dot(h, bot_ref[...])
        pe8t = jnp.broadcast_to(pt_ref[...], (8, pt_ref.shape[1]))
        pe8b = jnp.broadcast_to(pb_ref[...], (8, pb_ref.shape[1]))
        pe_term = (_dot(pe8t, top_ref[...]) + _dot(pe8b, bot_ref[...]))[0:1]
        pre = a + pltpu.roll(b, ncp - 1, 0) + pe_term
        hid = jax.nn.gelu(pre, approximate=True)
        o_ref[...] = _dot(hid.astype(BF16), w2_ref[...]).astype(BF16)

    one(hk_ref, kt_ref, kb_ref, kpt_ref, kpb_ref, k2_ref, ko_ref)
    one(hv_ref, vt_ref, vb_ref, vpt_ref, vpb_ref, v2_ref, vo_ref)


def _compress(kc, vc, wk, wv, b, t):
    ncp = t // CMP_STRIDE
    hw = CMP_STRIDE * NSA_GROUPS * HEAD_DIM
    hk = kc.reshape(b, ncp, hw)
    hv = vc.reshape(b, ncp, hw)
    hspec = pl.BlockSpec((None, ncp, hw), lambda i: (i, 0, 0))

    def full(a):
        return pl.BlockSpec(a.shape, lambda i: (0,) * a.ndim)

    widths = (wk[-1].shape[1], wv[-1].shape[1])
    return pl.pallas_call(
        _compress_kernel,
        grid=(b,),
        in_specs=[hspec, hspec] + [full(a) for a in wk] + [full(a) for a in wv],
        out_specs=[pl.BlockSpec((None, ncp, w), lambda i: (i, 0, 0)) for w in widths],
        out_shape=[jax.ShapeDtypeStruct((b, ncp, w), BF16) for w in widths],
        compiler_params=_cparams(("arbitrary",), 48),
        name="compress",
    )(hk, hv, *wk, *wv)


def _rel_bucket(dist):
    n = jnp.maximum(dist, 0)
    nf = jnp.maximum(n, 1).astype(F32)
    large = REL_EXACT + (jnp.log(nf / REL_EXACT) / math.log(REL_MAX_DIST / REL_EXACT)
                         * (REL_BUCKETS - REL_EXACT)).astype(I32)
    return jnp.where(n < REL_EXACT, n, jnp.minimum(large, REL_BUCKETS - 1))


def _rel_lookup(tab_ref, bucket, col):
    bits = [(bucket & (1 << k)) != 0 for k in range(5)]
    vals = [jnp.where(bits[0], tab_ref[2 * k + 1, col], tab_ref[2 * k, col]) for k in range(REL_BUCKETS // 2)]
    for lvl in range(1, 5):
        vals = [jnp.where(bits[lvl], vals[2 * k + 1], vals[2 * k]) for k in range(len(vals) // 2)]
    return vals[0]


def _strip_kernel(tab_ref, o_ref, *, a, window, head0, rel_far):
    h = pl.program_id(0)
    shape = o_ref.shape
    r = lax.broadcasted_iota(I32, shape, 0)
    j = lax.broadcasted_iota(I32, shape, 1)
    dist = r + a - j
    valid = dist >= 0
    if window is not None:
        valid = valid & (dist < window)
    val = _rel_lookup(tab_ref, _rel_bucket(dist), h + head0)
    if rel_far:
        val = val - tab_ref[REL_BUCKETS - 1, h + head0]
    o_ref[...] = jnp.where(valid, val * LOG2E, NEG)


def _bias_strip(rel_bias, n_heads, head0, a, width, window, rel_far):
    return pl.pallas_call(
        functools.partial(_strip_kernel, a=a, window=window, head0=head0, rel_far=rel_far),
        grid=(n_heads,),
        in_specs=[pl.BlockSpec(memory_space=pltpu.SMEM)],
        out_specs=pl.BlockSpec((None, QT, width), lambda h: (h, 0, 0)),
        out_shape=jax.ShapeDtypeStruct((n_heads, QT, width), F32),
        compiler_params=_cparams(("arbitrary",)),
        name="bias_strip",
    )(rel_bias)


def _cmp_bias_kernel(tab_ref, o_ref, *, nc):
    i = pl.program_id(0)
    shape = o_ref.shape[1:]
    r = lax.broadcasted_iota(I32, shape, 0)
    c = lax.broadcasted_iota(I32, shape, 1)
    dist = i * QT + r - (CMP_STRIDE * c + CMP_BLOCK - 1)
    valid = (dist >= 0) & (c < nc)
    bucket = _rel_bucket(dist)
    for h in range(NSA_HEADS):
        o_ref[h] = jnp.where(valid, _rel_lookup(tab_ref, bucket, h) * LOG2E, NEG)


def _cmp_bias(rel_bias, t):
    n_qt = t // QT
    ncp = t // CMP_STRIDE
    nc = (t - CMP_BLOCK) // CMP_STRIDE + 1
    return pl.pallas_call(
        functools.partial(_cmp_bias_kernel, nc=nc),
        grid=(n_qt,),
        in_specs=[pl.BlockSpec(memory_space=pltpu.SMEM)],
        out_specs=pl.BlockSpec((None, NSA_HEADS, QT, ncp), lambda i: (i, 0, 0, 0)),
        out_shape=jax.ShapeDtypeStruct((n_qt, NSA_HEADS, QT, ncp), F32),
        compiler_params=_cparams(("arbitrary",)),
        name="cmp_bias",
    )(rel_bias)


def _lane_groups(op, x):
    parts = [x[:, c * LANE:(c + 1) * LANE] for c in range(x.shape[1] // LANE)]
    while len(parts) > 1:
        parts = [op(parts[j], parts[j + 1]) for j in range(0, len(parts), 2)]
    return parts[0]


def _far_tiles(i):
    return jnp.maximum((i - (FAR_TILES - TPK)) >> (TPK.bit_length() - 1), 0)


def _strip_offset(i, kt):
    return pl.multiple_of(QT * jnp.maximum(FAR_TILES - (i - TPK * kt), 0), LANE)


def _for_tiles(lo, hi, step):
    n = hi - lo

    def pair(j, c):
        step(lo + 2 * j)
        step(lo + 2 * j + 1)
        return c

    lax.fori_loop(0, n >> 1, pair, 0)

    @pl.when((n & 1) == 1)
    def _():
        step(hi - 1)


def _nsa_kernel(qa_ref, gn_ref, kc_ref, vc_ref, kv_ref, sslc_ref, swin_ref, bcmp_ref, mapt_ref, eall_ref, o_ref,
                q_ref, s_ref, z_ref, p_ref, m_ref, acc_ref, ocmp_ref, sel_ref, st_ref, *, t):
    i = pl.program_id(1)
    q0 = i * QT
    ns = t // SLC_BLOCK
    ncp = t // CMP_STRIDE
    n_top = min(SLC_TOPN, ns)
    grows = NSA_HPG * QT
    arows = NSA_HEADS * QT
    shift = SLC_BLOCK.bit_length() - 1
    halves = QT // RB

    for h in range(NSA_HEADS):
        q_ref[h * QT:(h + 1) * QT, :] = qa_ref[:, h * LANE:(h + 1) * LANE]

    s_ref[:, 0:ncp] = _dot_nt(q_ref[...], kc_ref[...])
    imp = []
    for g in range(NSA_GROUPS):
        psum = None
        for h in range(NSA_HPG * g, NSA_HPG * (g + 1)):
            rows = slice(h * QT, (h + 1) * QT)
            z = s_ref[rows, 0:ncp] + bcmp_ref[h]
            m = jnp.maximum(jnp.max(z, axis=-1, keepdims=True), M_FLOOR)
            e = jnp.exp2(z - m)
            p = e / jnp.maximum(jnp.sum(e, axis=-1, keepdims=True), 1e-30)
            p_ref[rows, 0:ncp] = p.astype(BF16)
            psum = p if psum is None else psum + p
        grp = slice(g * grows, (g + 1) * grows)
        ocmp_ref[grp, :] = _dot(p_ref[grp, 0:ncp], vc_ref[:, g * LANE:(g + 1) * LANE])
        imp.append(_dot_nt(mapt_ref[...], psum.astype(BF16)))
    imp_t = jnp.concatenate(imp, axis=1)

    blk = lax.broadcasted_iota(I32, (ns, NSA_GROUPS * QT), 0)
    tq = q0 + (lax.broadcasted_iota(I32, (ns, NSA_GROUPS * QT), 1) & (QT - 1))
    cur = tq >> shift
    forced = (blk == 0) | (blk == cur) | (blk == cur - 1)
    avail = (blk << shift) <= tq
    score = jnp.where(avail, imp_t + jnp.where(forced, SLC_FORCE, 0.0), NEG)
    st_ref[0:ns, :] = score

    sub8 = 8
    groups = [score[v * sub8:(v + 1) * sub8] for v in range(ns // sub8)]
    blk8 = lax.broadcasted_iota(I32, (sub8, NSA_GROUPS * QT), 0)
    ranks = [jnp.zeros((sub8, NSA_GROUPS * QT), F32) for _ in groups]
    for jp in range(ns):
        rowb = jnp.broadcast_to(st_ref[jp:jp + 1, :], (sub8, NSA_GROUPS * QT))
        for v, sv in enumerate(groups):
            ge = jnp.where(rowb >= sv, 1.0, 0.0)
            gt = jnp.where(rowb > sv, 1.0, 0.0)
            if v * sub8 > jp:
                inc = ge
            elif v * sub8 + sub8 - 1 <= jp:
                inc = gt
            else:
                inc = jnp.where(blk8 + v * sub8 > jp, ge, gt)
            ranks[v] = ranks[v] + inc
    rank = jnp.concatenate(ranks, axis=0)
    sel_t = jnp.where((rank < n_top) & avail, 1.0, 0.0)
    if ns < LANE:
        sel_t = jnp.concatenate([sel_t, jnp.zeros((LANE - ns, NSA_GROUPS * QT), F32)], axis=0)
    for g in range(NSA_GROUPS):
        sel_ref[g] = sel_t[:, g * QT:(g + 1) * QT].T.astype(BF16)

    m_ref[...] = jnp.full((arows, LANE), M_FLOOR, F32)
    acc_ref[...] = jnp.zeros((arows, LANE), F32)

    def max_step(kt, near):
        k0 = pl.multiple_of(kt * KT, KT)
        s = _dot_nt(q_ref[...], kv_ref[pl.ds(k0, KT), KV_KS:KV_KS + LANE])
        for g in range(NSA_GROUPS):
            madd = (_dot(sel_ref[g], eall_ref[:, pl.ds(k0, KT)]) - 1.0) * (-NEG)
            for h in range(NSA_HPG * g, NSA_HPG * (g + 1)):
                for half in range(halves):
                    qrows = slice(half * RB, (half + 1) * RB)
                    rows = slice(h * QT + half * RB, h * QT + (half + 1) * RB)
                    z = s[rows] + madd[qrows]
                    if near:
                        z = z + sslc_ref[h, qrows, pl.ds(_strip_offset(i, kt), KT)]
                    z_ref[rows, pl.ds(k0, KT)] = z
                    m_ref[rows, :] = jnp.maximum(m_ref[rows, :], _lane_groups(jnp.maximum, z))

    n_far = _far_tiles(i)
    n_kt = (i >> (TPK.bit_length() - 1)) + 1
    _for_tiles(0, n_far, lambda kt: max_step(kt, False))
    _for_tiles(n_far, n_kt, lambda kt: max_step(kt, True))
    m_ref[...] = jnp.broadcast_to(jnp.max(m_ref[...], axis=-1, keepdims=True), (arows, LANE))

    def sum_step(kt):
        k0 = pl.multiple_of(kt * KT, KT)
        for g in range(NSA_GROUPS):
            ps = []
            for h in range(NSA_HPG * g, NSA_HPG * (g + 1)):
                for half in range(halves):
                    rows = slice(h * QT + half * RB, h * QT + (half + 1) * RB)
                    m = m_ref[rows, :]
                    z = z_ref[rows, pl.ds(k0, KT)]
                    ps.append(jnp.exp2(z - jnp.concatenate([m] * (KT // LANE), axis=1)).astype(BF16))
            grp = slice(g * grows, (g + 1) * grows)
            acc_ref[grp, :] += _dot(jnp.concatenate(ps, axis=0),
                                    kv_ref[pl.ds(k0, KT), KV_VS + g * LANE:KV_VS + (g + 1) * LANE])

    _for_tiles(0, n_kt, sum_step)

    ks0 = pl.multiple_of(jnp.maximum(i - WINDOW // QT, 0) * QT, QT)
    woff = pl.multiple_of(jnp.maximum(WINDOW // QT - i, 0) * QT, LANE)
    s_ref[...] = _dot_nt(q_ref[...], kv_ref[pl.ds(ks0, WIN_KEYS), KV_KW:KV_KW + LANE])
    for h in range(NSA_HEADS):
        for half in range(halves):
            qrows = slice(half * RB, (half + 1) * RB)
            rows = slice(h * QT + half * RB, h * QT + (half + 1) * RB)
            z = s_ref[rows, :] + swin_ref[h, qrows, pl.ds(woff, WIN_KEYS)]
            p_ref[rows, :] = jnp.exp2(z - jnp.max(z, axis=-1, keepdims=True)).astype(BF16)

    lane_ok = lax.broadcasted_iota(I32, (QT, LANE), 1) < HEAD_DIM
    for g in range(NSA_GROUPS):
        grp = slice(g * grows, (g + 1) * grows)
        o_win = _dot(p_ref[grp, :], kv_ref[pl.ds(ks0, WIN_KEYS), KV_VW + g * LANE:KV_VW + (g + 1) * LANE])
        for hp in range(NSA_HPG):
            h = NSA_HPG * g + hp
            rows = slice(h * QT, (h + 1) * QT)
            wrows = slice(hp * QT, (hp + 1) * QT)
            slc = acc_ref[rows, :]
            o_slc = slc / jnp.maximum(slc[:, ONES_LANE:ONES_LANE + 1], 1e-30)
            win = o_win[wrows]
            o_w = win / jnp.maximum(win[:, ONES_LANE:ONES_LANE + 1], 1e-30)
            o = (gn_ref[:, 3 * h:3 * h + 1] * ocmp_ref[rows, :] + gn_ref[:, 3 * h + 1:3 * h + 2] * o_slc
                 + gn_ref[:, 3 * h + 2:3 * h + 3] * o_w)
            o_ref[:, h * LANE:(h + 1) * LANE] = jnp.where(lane_ok, o, 0.0).astype(BF16)


def _nsa(qa, gn, kcmp, vcmp, kv, sslc, swin, bcmp, mapt, eall, b, t):
    n_qt = t // QT
    ncp = t // CMP_STRIDE
    ns = t // SLC_BLOCK
    arows = NSA_HEADS * QT
    assert ncp <= WIN_KEYS

    def full(a):
        return pl.BlockSpec(a.shape, lambda bi, i: (0,) * a.ndim)

    def once(a):
        return pl.BlockSpec(a.shape, lambda bi, i: (0,) * a.ndim, pipeline_mode=pl.Buffered(1))

    return pl.pallas_call(
        functools.partial(_nsa_kernel, t=t),
        grid=(b, n_qt),
        in_specs=[
            pl.BlockSpec((None, QT, NSA_HEADS * LANE), lambda bi, i: (bi, i, 0)),
            pl.BlockSpec((None, QT, LANE), lambda bi, i: (bi, i, 0)),
            pl.BlockSpec((None, ncp, LANE), lambda bi, i: (bi, 0, 0)),
            pl.BlockSpec((None, ncp, NSA_GROUPS * LANE), lambda bi, i: (bi, 0, 0)),
            pl.BlockSpec((None, t, KV_W), lambda bi, i: (bi, 0, 0), pipeline_mode=pl.Buffered(1)),
            once(sslc), once(swin),
            pl.BlockSpec((None, NSA_HEADS, QT, ncp), lambda bi, i: (i, 0, 0, 0)),
            full(mapt), once(eall),
        ],
        out_specs=pl.BlockSpec((None, QT, NSA_HEADS * LANE), lambda bi, i: (bi, i, 0)),
        out_shape=jax.ShapeDtypeStruct((b, t, NSA_HEADS * LANE), BF16),
        scratch_shapes=[
            pltpu.VMEM((arows, LANE), BF16),
            pltpu.VMEM((arows, WIN_KEYS), F32),
            pltpu.VMEM((arows, t), F32),
            pltpu.VMEM((arows, WIN_KEYS), BF16),
            pltpu.VMEM((arows, LANE), F32),
            pltpu.VMEM((arows, LANE), F32),
            pltpu.VMEM((arows, LANE), F32),
            pltpu.VMEM((NSA_GROUPS, QT, LANE), BF16),
            pltpu.VMEM((max(ns, 8), NSA_GROUPS * QT), F32),
        ],
        compiler_params=_cparams(("arbitrary", "arbitrary"), 56),
        name="nsa",
    )(qa, gn, kcmp, vcmp, kv, sslc, swin, bcmp, mapt, eall)


IDX_CHUNK = 512
FAST_PASSES = 12
LANE_SHIFT = LANE.bit_length() - 1


def _dsa_kernel(ql_ref, qi_ref, wi_ref, ki_ref, ckv_ref, strip_ref, wuv_ref, tri_ref, o_ref,
                q_ref, idx_ref, tmp_ref, s_ref, p_ref, m_ref, l_ref, acc_ref, *, t, k_sel):
    i = pl.program_id(1)
    q0 = i * QT
    nch = (i >> 2) + 1
    hrows = DSA_HEADS * QT
    sub = IDX_CHUNK // LANE
    halves = QT // RB

    for h in range(IDX_HEADS):
        q_ref[h * QT:(h + 1) * QT, :] = qi_ref[:, h * LANE:(h + 1) * LANE]
    tq = q0 + lax.broadcasted_iota(I32, (QT, IDX_CHUNK), 0)
    col = lax.broadcasted_iota(I32, (QT, IDX_CHUNK), 1)

    def idx_body(c, carry):
        c0 = pl.multiple_of(c * IDX_CHUNK, IDX_CHUNK)
        d = jnp.maximum(_dot_nt(q_ref[0:IDX_HEADS * QT, :], ki_ref[pl.ds(c0, IDX_CHUNK), :]), 0.0)
        acc = d[0:QT] * wi_ref[:, 0:1]
        for h in range(1, IDX_HEADS):
            acc = acc + d[h * QT:(h + 1) * QT] * wi_ref[:, h:h + 1]
        idx_ref[:, pl.ds(c0, IDX_CHUNK)] = jnp.where(col + c0 <= tq, acc, NEG)
        return carry

    lax.fori_loop(0, nch, idx_body, 0)

    zeros = jnp.zeros((QT, LANE), F32)

    def scan(fn, init):
        def body(c, carry):
            for s in range(sub):
                off = pl.multiple_of(c * IDX_CHUNK + s * LANE, LANE)
                carry = fn(idx_ref[:, pl.ds(off, LANE)], off, carry)
            return carry
        return lax.fori_loop(0, nch, body, init)

    def search(_):
        kf = float(k_sel)
        big = jnp.full((QT, LANE), BIG, F32)

        def count_gt(pivot):
            pb = jnp.broadcast_to(pivot, (QT, LANE))
            cnt = scan(lambda x, off, c: c + jnp.where(x > pb, 1.0, 0.0), zeros)
            return jnp.sum(cnt, axis=-1, keepdims=True)

        def init_fn(x, off, carry):
            lo, hi = carry
            return jnp.minimum(lo, jnp.where(x > 0.5 * NEG, x, BIG)), jnp.maximum(hi, x)

        row_min, row_max = scan(init_fn, (big, -big))
        row_min = jnp.min(row_min, axis=-1, keepdims=True)
        row_max = jnp.max(row_max, axis=-1, keepdims=True)
        n_valid = (q0 + 1 + lax.broadcasted_iota(I32, (QT, 1), 0)).astype(F32)

        def fast_step(_, c):
            lo, hi, g_lo, g_hi, side = c
            done = lo >= hi
            mid = lo + (hi - lo) * (g_lo / (g_lo - g_hi))
            mid = jnp.where((mid > lo) & (mid < hi), mid, lo + (hi - lo) * 0.5)
            cnt = count_gt(mid)
            up = cnt > kf
            above = cnt >= kf
            return (jnp.where(done | jnp.logical_not(above), lo, mid), jnp.where(done | up, hi, mid),
                    jnp.where(up, cnt - kf, jnp.where(side == 2, g_lo * 0.5, g_lo)),
                    jnp.where(up, jnp.where(side == 1, g_hi * 0.5, g_hi), cnt - kf),
                    jnp.where(up, 1, 2))

        lo0 = row_min - (jnp.abs(row_min) + 1.0)
        init = (lo0, row_max, n_valid - kf, jnp.full((QT, 1), -kf, F32), jnp.zeros((QT, 1), I32))
        lo, hi, _, _, _ = lax.fori_loop(0, FAST_PASSES, fast_step, init)
        done = lo >= hi

        def exact(_):
            lob = jnp.broadcast_to(lo, (QT, LANE))
            hib = jnp.broadcast_to(hi, (QT, LANE))

            def snap_fn(x, off, c):
                a, b_, n_lo, n_hi = c
                above_lo = x > lob
                above_hi = x > hib
                return (jnp.minimum(a, jnp.where(above_lo, x, BIG)), jnp.maximum(b_, jnp.where(above_hi, -BIG, x)),
                        n_lo + jnp.where(above_lo, 1.0, 0.0), n_hi + jnp.where(above_hi, 1.0, 0.0))

            a, b_, n_lo, n_hi = scan(snap_fn, (big, -big, zeros, zeros))
            lo_d = jnp.where(done, lo, jnp.min(a, axis=-1, keepdims=True))
            hi_d = jnp.where(done, hi, jnp.max(b_, axis=-1, keepdims=True))
            f_lo = jnp.sum(n_lo, axis=-1, keepdims=True)
            f_hi = jnp.sum(n_hi, axis=-1, keepdims=True)

            def cond(c):
                return jnp.max(jnp.where(c[0] < c[1], 1, 0)) > 0

            def step(c):
                lo, hi, c_lo, c_hi = c
                mid = lo + (hi - lo) * 0.5
                mid = jnp.where(mid < hi, mid, lo)
                midb = jnp.broadcast_to(mid, (QT, LANE))

                def fn(x, off, cc):
                    cnt, amin, bmax = cc
                    gt = x > midb
                    return (cnt + jnp.where(gt, 1.0, 0.0), jnp.minimum(amin, jnp.where(gt, x, BIG)),
                            jnp.maximum(bmax, jnp.where(gt, -BIG, x)))

                cnt, amin, bmax = scan(fn, (zeros, big, -big))
                cnt = jnp.sum(cnt, axis=-1, keepdims=True)
                amin = jnp.min(amin, axis=-1, keepdims=True)
                bmax = jnp.max(bmax, axis=-1, keepdims=True)
                up = cnt >= kf
                same = lo >= hi
                return (jnp.where(same | jnp.logical_not(up), lo, amin), jnp.where(same | up, hi, bmax),
                        jnp.where(same | jnp.logical_not(up), c_lo, cnt), jnp.where(same | up, c_hi, cnt))

            thr, _, c_ge, c_gt = lax.while_loop(cond, step, (lo_d, hi_d, f_lo, f_hi))
            return thr, c_ge, c_gt

        all_done = jnp.min(jnp.where(done, 1, 0)) > 0
        k_col = jnp.full((QT, 1), kf, F32)
        thr, c_ge, c_gt = lax.cond(all_done, lambda _: (lo, k_col, k_col), exact, 0)
        c_ge = jnp.where(done, kf, c_ge)
        c_gt = jnp.where(done, kf, c_gt)
        thrb = jnp.broadcast_to(thr, (QT, LANE))
        need = k_sel - c_gt

        def tie_search(_):
            lane = lax.broadcasted_iota(I32, (QT, LANE), 1)
            lanef = lane.astype(F32)

            def grp_fn(x, off, g_cnt):
                c = jnp.sum(jnp.where(x == thrb, 1.0, 0.0), axis=-1, keepdims=True)
                return jnp.where(lane == (off >> LANE_SHIFT), c, g_cnt)

            g_cum = _dot(scan(grp_fn, zeros).astype(BF16), tri_ref[...])
            g_star = jnp.sum(jnp.where(g_cum < need, 1.0, 0.0), axis=-1, keepdims=True)
            before = jnp.sum(jnp.where(lanef == g_star - 1.0, g_cum, 0.0), axis=-1, keepdims=True)

            def slab_fn(x, off, slab):
                return jnp.where(g_star == (off >> LANE_SHIFT).astype(F32), x, slab)

            slab = scan(slab_fn, jnp.full((QT, LANE), NEG, F32))
            pre = _dot(jnp.where(slab == thrb, 1.0, 0.0).astype(BF16), tri_ref[...])
            lane_cut = jnp.sum(jnp.where(pre < need - before, 1.0, 0.0), axis=-1, keepdims=True)
            return (g_star * LANE + lane_cut).astype(I32)

        any_tie = jnp.max(jnp.where(c_ge > k_sel, 1, 0)) > 0
        p_cut = lax.cond(any_tie, tie_search, lambda _: jnp.full((QT, 1), t, I32), 0)
        return thr, jnp.where(done, -1, p_cut)

    thr, p_cut = lax.cond(q0 >= k_sel, search,
                          lambda _: (jnp.full((QT, 1), M_FLOOR, F32), jnp.full((QT, 1), t, I32)), 0)
    thrk = jnp.broadcast_to(thr, (QT, KT))
    pcutk = jnp.broadcast_to(p_cut, (QT, KT))
    colk = lax.broadcasted_iota(I32, (QT, KT), 1)

    for h in range(DSA_HEADS):
        q_ref[h * QT:(h + 1) * QT, :] = ql_ref[:, h * LANE:(h + 1) * LANE]
    m_ref[...] = jnp.full((hrows, LANE), M_FLOOR, F32)
    l_ref[...] = jnp.zeros((hrows, LANE), F32)
    acc_ref[...] = jnp.zeros((hrows, LANE), F32)
    blocks = [(h, half) for h in range(DSA_HEADS) for half in range(halves)]

    n_far = _far_tiles(i)
    n_kt = (i >> (TPK.bit_length() - 1)) + 1

    def max_step(kt, near):
        k0 = pl.multiple_of(kt * KT, KT)
        s = _dot_nt(q_ref[...], ckv_ref[pl.ds(k0, KT), :])
        x = idx_ref[:, pl.ds(k0, KT)]
        keep = (x > thrk) | ((x == thrk) & (colk + k0 <= pcutk))
        selm = jnp.where(keep, 0.0, NEG)
        for h, half in blocks:
            qrows = slice(half * RB, (half + 1) * RB)
            rows = slice(h * QT + half * RB, h * QT + (half + 1) * RB)
            z = s[rows] + selm[qrows]
            if near:
                z = z + strip_ref[h, qrows, pl.ds(_strip_offset(i, kt), KT)]
            s_ref[rows, pl.ds(k0, KT)] = z
            m_ref[rows, :] = jnp.maximum(m_ref[rows, :], _lane_groups(jnp.maximum, z))

    _for_tiles(0, n_far, lambda kt: max_step(kt, False))
    _for_tiles(n_far, n_kt, lambda kt: max_step(kt, True))
    m_ref[...] = jnp.broadcast_to(jnp.max(m_ref[...], axis=-1, keepdims=True), (hrows, LANE))

    def sum_step(kt):
        k0 = pl.multiple_of(kt * KT, KT)
        ps = []
        for h, half in blocks:
            rows = slice(h * QT + half * RB, h * QT + (half + 1) * RB)
            m = m_ref[rows, :]
            p = jnp.exp2(s_ref[rows, pl.ds(k0, KT)] - jnp.concatenate([m] * (KT // LANE), axis=1))
            l_ref[rows, :] += _lane_groups(jnp.add, p)
            ps.append(p.astype(BF16))
        acc_ref[...] += _dot(jnp.concatenate(ps, axis=0), ckv_ref[pl.ds(k0, KT), :])

    _for_tiles(0, n_kt, sum_step)
    for h in range(DSA_HEADS):
        rows = slice(h * QT, (h + 1) * QT)
        l = jnp.sum(l_ref[rows, :], axis=-1, keepdims=True)
        o_lat = (acc_ref[rows, :] / jnp.maximum(l, 1e-30)).astype(BF16)
        o_ref[:, h * LANE:(h + 1) * LANE] = _dot(o_lat, wuv_ref[h]).astype(BF16)


def _dsa(qlat, qidx, widx, kidx, ckvn, strip, wuv_pad, b, t):
    n_qt = t // QT
    k_sel = min(IDX_TOPK_MAX, t // 4)
    assert k_sel % QT == 0 and t % IDX_CHUNK == 0 and t // LANE <= LANE
    hrows = DSA_HEADS * QT
    tri = jnp.asarray(np.triu(np.ones((LANE, LANE), np.float32)), BF16)

    def full(a):
        return pl.BlockSpec(a.shape, lambda bi, i: (0,) * a.ndim)

    return pl.pallas_call(
        functools.partial(_dsa_kernel, t=t, k_sel=k_sel),
        grid=(b, n_qt),
        in_specs=[
            pl.BlockSpec((None, QT, DSA_HEADS * LANE), lambda bi, i: (bi, i, 0)),
            pl.BlockSpec((None, QT, IDX_HEADS * LANE), lambda bi, i: (bi, i, 0)),
            pl.BlockSpec((None, QT, LANE), lambda bi, i: (bi, i, 0)),
            pl.BlockSpec((None, t, LANE), lambda bi, i: (bi, 0, 0)),
            pl.BlockSpec((None, t, LANE), lambda bi, i: (bi, 0, 0)),
            pl.BlockSpec(strip.shape, lambda bi, i: (0, 0, 0), pipeline_mode=pl.Buffered(1)),
            full(wuv_pad), full(tri),
        ],
        out_specs=pl.BlockSpec((None, QT, DSA_HEADS * LANE), lambda bi, i: (bi, i, 0)),
        out_shape=jax.ShapeDtypeStruct((b, t, DSA_HEADS * LANE), BF16),
        scratch_shapes=[
            pltpu.VMEM((hrows, LANE), BF16),
            pltpu.VMEM((QT, t), F32),
            pltpu.VMEM((hrows, KT), F32),
            pltpu.VMEM((hrows, t), F32),
            pltpu.VMEM((hrows, KT), BF16),
            pltpu.VMEM((hrows, LANE), F32),
            pltpu.VMEM((hrows, LANE), F32),
            pltpu.VMEM((hrows, LANE), F32),
        ],
        compiler_params=_cparams(("arbitrary", "arbitrary"), 56),
        name="dsa",
    )(qlat, qidx, widx, kidx, ckvn, strip, wuv_pad, tri)


RT_GRP = 0
RT_EXP = 32


def _layer_norm(y, g, b):
    mu = jnp.mean(y, axis=-1, keepdims=True)
    yc = y - mu
    var = jnp.mean(yc * yc, axis=-1, keepdims=True)
    return yc * lax.rsqrt(var + 1e-5) * g + b


def _post_kernel(x_ref, oa_ref, ob_ref, sga_ref, sgb_ref, wa_ref, wb_ref, wo_ref, g_ref, b_ref, wr_ref, br_ref,
                 h_ref, rt_ref, rw_ref, *, alpha):
    rows = x_ref.shape[0]
    merged = (sga_ref[...].astype(F32) * _dot(oa_ref[...], wa_ref[...])
              + sgb_ref[...].astype(F32) * _dot(ob_ref[...], wb_ref[...]))
    y = alpha * x_ref[...] + _dot(merged.astype(BF16), wo_ref[...])
    h = _layer_norm(y, g_ref[...], b_ref[...])
    h_ref[...] = h

    h_hi = h.astype(BF16)
    h_lo = (h - h_hi.astype(F32)).astype(BF16)
    zz = _dot(h_hi, wr_ref[...])
    z = zz[:, 0:LANE] + zz[:, LANE:2 * LANE] + _dot(h_lo, wr_ref[:, 0:LANE]) + br_ref[...]
    lane = lax.broadcasted_iota(I32, (rows, LANE), 1)
    is_g = lane < N_EXPERT_GROUPS
    zg = jnp.where(is_g, z, -BIG)
    gmax = jnp.max(zg, axis=-1, keepdims=True)
    g_sel = jnp.min(jnp.where(is_g & (z == gmax), lane, LANE), axis=-1, keepdims=True)
    p_grp = 1.0 / jnp.sum(jnp.where(is_g, jnp.exp(zg - gmax), 0.0), axis=-1, keepdims=True)
    in_grp = (lane >= RT_EXP) & (lane < RT_EXP + N_EXPERTS) & (((lane - RT_EXP) >> 3) == g_sel)
    ze = jnp.where(in_grp, z, -BIG)
    m1 = jnp.max(ze, axis=-1, keepdims=True)
    i1 = jnp.min(jnp.where(in_grp & (z == m1), lane, LANE), axis=-1, keepdims=True)
    ze2 = jnp.where(lane == i1, -BIG, ze)
    m2 = jnp.max(ze2, axis=-1, keepdims=True)
    i2 = jnp.min(jnp.where(in_grp & (lane != i1) & (z == m2), lane, LANE), axis=-1, keepdims=True)
    e21 = jnp.exp(m2 - m1)
    den = 1.0 + e21
    w1 = p_grp * (1.0 / den)
    w2 = p_grp * (e21 / den)
    rt_ref[...] = jnp.where(lane == 0, i1 - RT_EXP, jnp.where(lane == 1, i2 - RT_EXP, 0))
    rw_ref[...] = jnp.where(lane == 0, w1, jnp.where(lane == 1, w2, 0.0))


def _post(x2, oa, ob, sga, sgb, wa, wb, wo, g, b_, wr, br, alpha, rows=256):
    n, d = x2.shape

    def row(w):
        return pl.BlockSpec((rows, w), lambda i: (i, 0))

    def full(a):
        return pl.BlockSpec(a.shape, lambda i: (0,) * a.ndim)

    return pl.pallas_call(
        functools.partial(_post_kernel, alpha=alpha),
        grid=(n // rows,),
        in_specs=[row(d), row(1024), row(1024), row(1024), row(1024), full(wa), full(wb), full(wo), full(g),
                  full(b_), full(wr), full(br)],
        out_specs=[row(d), row(LANE), row(LANE)],
        out_shape=[jax.ShapeDtypeStruct((n, d), F32), jax.ShapeDtypeStruct((n, LANE), I32),
                   jax.ShapeDtypeStruct((n, LANE), F32)],
        compiler_params=_cparams(("arbitrary",), 48),
        name="post",
    )(x2, oa, ob, sga, sgb, wa, wb, wo, g, b_, wr, br)


def _onehots(rt_ref, rows):
    lane = lax.broadcasted_iota(I32, (rows, LANE), 1)
    oh0 = jnp.where(lane == rt_ref[:, 0:1], 1.0, 0.0)
    oh1 = jnp.where(lane == rt_ref[:, 1:2], 1.0, 0.0)
    return oh0, oh1


def _rank_kernel(rt_ref, tri_ref, rank_ref, cnt_ref, carry_ref):
    rows = rt_ref.shape[0]

    @pl.when(pl.program_id(0) == 0)
    def _():
        carry_ref[...] = jnp.zeros_like(carry_ref)

    oh0, oh1 = _onehots(rt_ref, rows)
    both = oh0 + oh1
    before = _dot(tri_ref[...], both.astype(BF16)) + carry_ref[0:1, :]
    r0 = jnp.sum(oh0 * before, axis=-1, keepdims=True)
    r1 = jnp.sum(oh1 * before, axis=-1, keepdims=True)
    lane = lax.broadcasted_iota(I32, (rows, LANE), 1)
    rank_ref[...] = jnp.where(lane == 0, r0, jnp.where(lane == 1, r1, 0.0))
    carry_ref[...] = carry_ref[...] + jnp.sum(both, axis=0, keepdims=True)
    cnt_ref[...] = carry_ref[...]


def _moe_rank(rt, rows=512):
    n = rt.shape[0]
    tri = jnp.asarray(np.tril(np.ones((rows, rows), np.float32), -1), BF16)
    return pl.pallas_call(
        _rank_kernel,
        grid=(n // rows,),
        in_specs=[pl.BlockSpec((rows, LANE), lambda i: (i, 0)), pl.BlockSpec((rows, rows), lambda i: (0, 0))],
        out_specs=[pl.BlockSpec((rows, LANE), lambda i: (i, 0)), pl.BlockSpec((8, LANE), lambda i: (0, 0))],
        out_shape=[jax.ShapeDtypeStruct((n, LANE), F32), jax.ShapeDtypeStruct((8, LANE), F32)],
        scratch_shapes=[pltpu.VMEM((8, LANE), F32)],
        compiler_params=_cparams(("arbitrary",)),
        name="moe_rank",
    )(rt, tri)


def _lane_cumsum(v):
    lane = lax.broadcasted_iota(I32, v.shape, 1)
    s = 1
    while s < LANE:
        v = v + jnp.where(lane >= s, pltpu.roll(v, s, 1), 0.0)
        s *= 2
    return v


def _dest_kernel(rt_ref, rank_ref, cnt_ref, dest_ref, bexp_ref, *, n_blk_pad):
    rows = rt_ref.shape[0]
    lane8 = lax.broadcasted_iota(I32, (8, LANE), 1)
    cnt = jnp.where(lane8 < N_EXPERTS, cnt_ref[...], 0.0)
    padded = jnp.floor((cnt + (EXPERT_BLOCK - 1)) * (1.0 / EXPERT_BLOCK)) * EXPERT_BLOCK
    pend = _lane_cumsum(padded)
    poff = (pend - padded)[0:1, :]
    oh0, oh1 = _onehots(rt_ref, rows)
    d0 = jnp.sum(oh0 * poff, axis=-1, keepdims=True) + rank_ref[:, 0:1]
    d1 = jnp.sum(oh1 * poff, axis=-1, keepdims=True) + rank_ref[:, 1:2]
    lane = lax.broadcasted_iota(I32, (rows, LANE), 1)
    dest_ref[...] = jnp.where(lane == 0, d0, jnp.where(lane == 1, d1, 0.0)).astype(I32)

    lane_b = lax.broadcasted_iota(I32, (n_blk_pad, LANE), 1)
    start = (lax.broadcasted_iota(I32, (n_blk_pad, LANE), 0) * EXPERT_BLOCK).astype(F32)
    hit = jnp.where((lane_b < N_EXPERTS) & (pend[0:1, :] <= start), 1.0, 0.0)
    e_blk = jnp.minimum(jnp.sum(hit, axis=-1, keepdims=True), N_EXPERTS - 1.0)
    used = jnp.max(pend[0:1, :], axis=-1, keepdims=True) * (1.0 / EXPERT_BLOCK)
    bexp_ref[...] = jnp.where(lane_b == 0, e_blk, jnp.where(lane_b == 1, used, 0.0)).astype(I32)


def _moe_dest(rt, rank, cnt, n_blk, rows=1024):
    n = rt.shape[0]
    n_blk_pad = -(-n_blk // 8) * 8
    return pl.pallas_call(
        functools.partial(_dest_kernel, n_blk_pad=n_blk_pad),
        grid=(n // rows,),
        in_specs=[pl.BlockSpec((rows, LANE), lambda i: (i, 0)), pl.BlockSpec((rows, LANE), lambda i: (i, 0)),
                  pl.BlockSpec((8, LANE), lambda i: (0, 0))],
        out_specs=[pl.BlockSpec((rows, LANE), lambda i: (i, 0)), pl.BlockSpec((n_blk_pad, LANE), lambda i: (0, 0))],
        out_shape=[jax.ShapeDtypeStruct((n, LANE), I32), jax.ShapeDtypeStruct((n_blk_pad, LANE), I32)],
        compiler_params=_cparams(("arbitrary",)),
        name="moe_dest",
    )(rt, rank, cnt)


MOE_ROWS = 512
DMA_UNROLL = 8


def _dispatch_kernel(dest_ref, h_ref, xin_ref, xpad_ref, sem):
    del xin_ref

    def row_copy(r, d):
        return pltpu.make_async_copy(h_ref.at[pl.ds(r, 1)], xpad_ref.at[pl.ds(d, 1)], sem)

    def start(r, carry):
        for j in range(EXPERT_TOPK):
            row_copy(r, dest_ref[EXPERT_TOPK * r + j]).start(priority=j)
        return carry

    lax.fori_loop(0, MOE_ROWS, start, 0, unroll=DMA_UNROLL)

    def wait(r, carry):
        for j in range(EXPERT_TOPK):
            row_copy(r, dest_ref[EXPERT_TOPK * r + j]).wait()
        return carry

    lax.fori_loop(0, MOE_ROWS, wait, 0, unroll=DMA_UNROLL)


def _moe_dispatch(dest_flat, h, n_slots):
    n, d = h.shape
    zeros = jnp.zeros((n_slots, d), h.dtype)
    return pl.pallas_call(
        _dispatch_kernel,
        grid=(n // MOE_ROWS,),
        in_specs=[pl.BlockSpec((EXPERT_TOPK * MOE_ROWS,), lambda i: (i,), memory_space=pltpu.SMEM),
                  pl.BlockSpec((MOE_ROWS, d), lambda i: (i, 0)),
                  pl.BlockSpec(memory_space=pl.ANY)],
        out_specs=pl.BlockSpec(memory_space=pl.ANY),
        out_shape=jax.ShapeDtypeStruct((n_slots, d), h.dtype),
        scratch_shapes=[pltpu.SemaphoreType.DMA(())],
        input_output_aliases={2: 0},
        compiler_params=_cparams(("arbitrary",)),
        name="moe_dispatch",
    )(dest_flat, h, zeros)


def _expert_kernel(bexp_ref, used_ref, x_ref, wg_ref, wu_ref, wd_ref, y_ref):
    blk = pl.program_id(0)

    @pl.when(blk < used_ref[0])
    def _():
        xb = x_ref[...].astype(BF16)
        gate = _dot(xb, wg_ref[...].astype(BF16))
        up = _dot(xb, wu_ref[...].astype(BF16))
        act = (jax.nn.silu(gate) * up).astype(BF16)
        y_ref[...] = _dot(act, wd_ref[...].astype(BF16))

    @pl.when(blk >= used_ref[0])
    def _():
        y_ref[...] = jnp.zeros_like(y_ref)


def _moe_experts(bexp, used, xpad, w_gate, w_up, w_down):
    n_slots, d = xpad.shape
    n_blk = n_slots // EXPERT_BLOCK
    de = w_gate.shape[-1]
    grid_spec = pltpu.PrefetchScalarGridSpec(
        num_scalar_prefetch=2,
        grid=(n_blk,),
        in_specs=[
            pl.BlockSpec((EXPERT_BLOCK, d), lambda i, be, us: (i, 0)),
            pl.BlockSpec((None, d, de), lambda i, be, us: (be[i], 0, 0)),
            pl.BlockSpec((None, d, de), lambda i, be, us: (be[i], 0, 0)),
            pl.BlockSpec((None, de, d), lambda i, be, us: (be[i], 0, 0)),
        ],
        out_specs=pl.BlockSpec((EXPERT_BLOCK, d), lambda i, be, us: (i, 0)),
    )
    return pl.pallas_call(
        _expert_kernel,
        grid_spec=grid_spec,
        out_shape=jax.ShapeDtypeStruct((n_slots, d), F32),
        compiler_params=_cparams(("arbitrary",), 48),
        name="moe_experts",
    )(bexp, used, xpad, w_gate, w_up, w_down)


def _combine_kernel(dest_ref, h_ref, rw_ref, g_ref, b_ref, y_ref, o_ref, buf_ref, sem, *, alpha):
    def row_copy(r, j):
        return pltpu.make_async_copy(y_ref.at[pl.ds(dest_ref[EXPERT_TOPK * r + j], 1)],
                                     buf_ref.at[j, pl.ds(r, 1)], sem)

    def start(r, carry):
        for j in range(EXPERT_TOPK):
            row_copy(r, j).start(priority=j)
        return carry

    lax.fori_loop(0, MOE_ROWS, start, 0, unroll=DMA_UNROLL)

    def wait(r, carry):
        for j in range(EXPERT_TOPK):
            row_copy(r, j).wait()
        return carry

    lax.fori_loop(0, MOE_ROWS, wait, 0, unroll=DMA_UNROLL)
    moe = buf_ref[0] * rw_ref[:, 0:1] + buf_ref[1] * rw_ref[:, 1:2]
    o_ref[...] = _layer_norm(alpha * h_ref[...] + moe, g_ref[...], b_ref[...])


def _moe_combine(dest_flat, h, rw, g, b_, ypad, alpha):
    n, d = h.shape
    return pl.pallas_call(
        functools.partial(_combine_kernel, alpha=alpha),
        grid=(n // MOE_ROWS,),
        in_specs=[pl.BlockSpec((EXPERT_TOPK * MOE_ROWS,), lambda i: (i,), memory_space=pltpu.SMEM),
                  pl.BlockSpec((MOE_ROWS, d), lambda i: (i, 0)),
                  pl.BlockSpec((MOE_ROWS, LANE), lambda i: (i, 0)),
                  pl.BlockSpec((1, d), lambda i: (0, 0)),
                  pl.BlockSpec((1, d), lambda i: (0, 0)),
                  pl.BlockSpec(memory_space=pl.ANY)],
        out_specs=pl.BlockSpec((MOE_ROWS, d), lambda i: (i, 0)),
        out_shape=jax.ShapeDtypeStruct((n, d), F32),
        scratch_shapes=[pltpu.VMEM((EXPERT_TOPK, MOE_ROWS, d), F32), pltpu.SemaphoreType.DMA(())],
        compiler_params=_cparams(("arbitrary",), 48),
        name="moe_combine",
    )(dest_flat, h, rw, g, b_, ypad)


def _cmp_map_t(t):
    nc = (t - CMP_BLOCK) // CMP_STRIDE + 1
    ns = t // SLC_BLOCK
    ncp = t // CMP_STRIDE
    cs = CMP_STRIDE * np.arange(nc)[:, None]
    ss = SLC_BLOCK * np.arange(ns)[None, :]
    ov = np.minimum(cs + CMP_BLOCK, ss + SLC_BLOCK) - np.maximum(cs, ss)
    m = np.clip(ov, 0, None).astype(np.float32) / CMP_STRIDE
    out = np.zeros((ns, ncp), np.float32)
    out[:, :nc] = m.T
    return jnp.asarray(out, BF16)


def _block_expand(t):
    ns = t // SLC_BLOCK
    rows = max(ns, LANE)
    e = np.zeros((rows, t), np.float32)
    e[np.arange(t) // SLC_BLOCK, np.arange(t)] = 1.0
    return jnp.asarray(e, BF16)


def _pad_head_rows(w, n_heads):
    wh = w.reshape(n_heads, HEAD_DIM, w.shape[-1])
    return jnp.concatenate([wh, jnp.zeros_like(wh)], axis=1).reshape(n_heads * LANE, w.shape[-1]).astype(BF16)


def kernel(x, w_in, cmp_pe_k, cmp_pe_v, cmp_w1_k, cmp_w2_k, cmp_w1_v, cmp_w2_v, ckv_norm_g, w_uk, w_uv, rel_bias,
           w_branch_a, w_branch_b, w_out, ln1_g, ln1_b, w_grp, b_grp, w_rtr, b_rtr, w_gate, w_up, w_down, ln2_g,
           ln2_b):
    b, t, d = x.shape
    n = b * t
    depth = w_in.shape[0]
    alpha = (2.0 * depth) ** 0.25
    assert t % 512 == 0 and t >= WIN_KEYS and n % MOE_ROWS == 0

    sslc = _bias_strip(rel_bias, NSA_HEADS, 0, STRIP_A, STRIP_W, None, True)
    sdsa = _bias_strip(rel_bias, DSA_HEADS, NSA_HEADS, STRIP_A, STRIP_W, None, True)
    swin = _bias_strip(rel_bias, NSA_HEADS, 0, WIN_A, WIN_W, WINDOW, False)
    bcmp = _cmp_bias(rel_bias, t)
    mapt = _cmp_map_t(t)
    eall = _block_expand(t)

    n_a = n * EXPERT_TOPK
    n_blk = -(-n_a // EXPERT_BLOCK) + N_EXPERTS
    n_slots = n_blk * EXPERT_BLOCK

    h = x.reshape(n, d)
    for l in range(depth):
        w_pad = _proj_weights(w_in[l])
        wuk = w_uk[l]
        z = jnp.zeros_like(wuk[0])
        wuk_pairs = jnp.stack([
            jnp.concatenate([jnp.concatenate([wuk[2 * k], z], axis=1), jnp.concatenate([z, wuk[2 * k + 1]], axis=1)],
                            axis=0) for k in range(DSA_HEADS // 2)]).astype(BF16)
        wuv_pad = jnp.pad(w_uv[l], ((0, 0), (0, 0), (0, LANE - HEAD_DIM))).astype(BF16)
        wk = _cmp_weights(cmp_w1_k[l], cmp_w2_k[l], cmp_pe_k[l], HEAD_DIM)
        wv = _cmp_weights(cmp_w1_v[l], cmp_w2_v[l], cmp_pe_v[l], LANE)
        wa_pad = _pad_head_rows(w_branch_a[l], NSA_HEADS)
        wb_pad = _pad_head_rows(w_branch_b[l], DSA_HEADS)
        wr = jnp.zeros((d, LANE), F32).at[:, RT_GRP:RT_GRP + N_EXPERT_GROUPS].set(w_grp[l])
        wr = wr.at[:, RT_EXP:RT_EXP + N_EXPERTS].set(w_rtr[l])
        wr_hi = wr.astype(BF16)
        wr = jnp.concatenate([wr_hi, (wr - wr_hi.astype(F32)).astype(BF16)], axis=1)
        br = jnp.zeros((1, LANE), F32).at[0, RT_GRP:RT_GRP + N_EXPERT_GROUPS].set(b_grp[l])
        br = br.at[0, RT_EXP:RT_EXP + N_EXPERTS].set(b_rtr[l])

        (qa, kc, vc, kv, gn, qlat, ckvn, qidx, kidx, widx, sga, sgb) = _proj(
            h, w_pad, wuk_pairs, ckv_norm_g[l].reshape(1, KV_RANK))
        kcmp, vcmp = _compress(kc, vc, wk, wv, b, t)

        def b3(a):
            return a.reshape(b, t, a.shape[-1])

        oa = _nsa(b3(qa), b3(gn), kcmp, vcmp, b3(kv), sslc, swin, bcmp, mapt, eall, b, t)
        ob = _dsa(b3(qlat), b3(qidx), b3(widx), b3(kidx), b3(ckvn), sdsa, wuv_pad, b, t)

        h1, rt, rw = _post(h, oa.reshape(n, -1), ob.reshape(n, -1), sga, sgb, wa_pad, wb_pad, w_out[l].astype(BF16),
                           ln1_g[l].reshape(1, d), ln1_b[l].reshape(1, d), wr, br, alpha)

        rank, cnt = _moe_rank(rt)
        dest, bexp = _moe_dest(rt, rank, cnt, n_blk)
        dest_flat = dest[:, :EXPERT_TOPK].reshape(n_a)
        xpad = _moe_dispatch(dest_flat, h1, n_slots)
        ypad = _moe_experts(bexp[:n_blk, 0], bexp[:1, 1], xpad, w_gate[l], w_up[l], w_down[l])
        h = _moe_combine(dest_flat, h1, rw, ln2_g[l].reshape(1, d), ln2_b[l].reshape(1, d), ypad, alpha)
    return h.reshape(b, t, d)
```

```python
import functools
import math

import numpy as np
import jax
import jax.numpy as jnp
from jax import lax
from jax.experimental import pallas as pl
from jax.experimental.pallas import tpu as pltpu

F32 = jnp.float32
BF16 = jnp.bfloat16
I32 = jnp.int32

D_MODEL = 1024
HEAD_DIM = 64
NSA_HEADS = 8
NSA_GROUPS = 2
NSA_HPG = NSA_HEADS // NSA_GROUPS
CMP_BLOCK = 32
CMP_STRIDE = 16
CMP_HIDDEN = 256
SLC_BLOCK = 64
SLC_TOPN = 16
SLC_FORCE = 1e4
WINDOW = 512
DSA_HEADS = 8
KV_RANK = 128
IDX_HEADS = 4
IDX_DIM = 64
IDX_TOPK_MAX = 256
REL_BUCKETS = 32
REL_EXACT = 16
REL_MAX_DIST = 1024
N_EXPERT_GROUPS = 4
EXPERTS_PER_GROUP = 8
N_EXPERTS = N_EXPERT_GROUPS * EXPERTS_PER_GROUP
EXPERT_TOPK = 2
D_EXPERT = 256
EXPERT_BLOCK = 256
NSA_WIDTH = NSA_HEADS * HEAD_DIM
DSA_WIDTH = DSA_HEADS * HEAD_DIM
NEG = -1e30
SPLIT_SIZES = (NSA_WIDTH,) + (NSA_GROUPS * HEAD_DIM,) * 6 + (
    NSA_HEADS * 3, DSA_WIDTH, KV_RANK, IDX_HEADS * IDX_DIM, IDX_DIM, IDX_HEADS, D_MODEL, D_MODEL)
SPLIT_POINTS = tuple(int(v) for v in np.cumsum(SPLIT_SIZES)[:-1])

LANE = 128
QT = 128
KT = 512
TPK = KT // QT
RB = 64
M_FLOOR = -1e29
BIG = 3e38
LOG2E = math.log2(math.e)

FAR_TILES = 11
STRIP_A = FAR_TILES * QT
STRIP_W = STRIP_A + KT
assert REL_EXACT + int(math.log((STRIP_A - KT + 1) / REL_EXACT) / math.log(REL_MAX_DIST / REL_EXACT)
                       * (REL_BUCKETS - REL_EXACT)) >= REL_BUCKETS - 1
WIN_A = WINDOW
WIN_KEYS = WINDOW + QT
WIN_W = WIN_A + WIN_KEYS

O_QA, O_KC, O_VC, O_KV, O_GN, O_QB, O_CKV, O_QIDX, O_KIDX, O_WIDX, O_GA, O_GB, PROJ_W = (
    int(v) for v in np.cumsum([0, 1024, 128, 128, 768, 128, 512, 128, 512, 128, 128, 1024, 1024]))
KV_KS, KV_VS, KV_KW, KV_VW, KV_W = 0, 128, 384, 512, 768
ONES_LANE = HEAD_DIM

_NT = (((1,), (1,)), ((), ()))


def _dot(a, b):
    return jnp.dot(a, b, preferred_element_type=F32)


def _dot_nt(a, b):
    return lax.dot_general(a, b, _NT, preferred_element_type=F32)


def _cparams(sem, vmem_mb=None):
    kw = dict(dimension_semantics=sem)
    if vmem_mb is not None:
        kw["vmem_limit_bytes"] = vmem_mb * 1024 * 1024
    return pltpu.CompilerParams(**kw)


def _proj_weights(w_in):
    (q_a, kc, vc, ks, vs, kw, vw, g_nsa, q_b, ckv, q_idx, k_idx, w_idx, ga, gb) = jnp.split(w_in, SPLIT_POINTS, axis=1)
    d = w_in.shape[0]
    scale = HEAD_DIM ** -0.5 * LOG2E
    z64 = jnp.zeros((d, NSA_HPG, HEAD_DIM), F32)
    qa = (q_a * scale).reshape(d, NSA_GROUPS, NSA_HPG, HEAD_DIM)
    qa_pad = jnp.concatenate([
        jnp.concatenate([qa[:, 0], z64], axis=-1).reshape(d, NSA_HPG * LANE),
        jnp.concatenate([z64, qa[:, 1]], axis=-1).reshape(d, NSA_HPG * LANE)], axis=1)
    qi = q_idx.reshape(d, IDX_HEADS, IDX_DIM)
    qi_pad = jnp.concatenate([qi, jnp.zeros_like(qi)], axis=-1).reshape(d, IDX_HEADS * LANE)

    def pad(a):
        return jnp.pad(a, ((0, 0), (0, LANE - a.shape[1])))

    def per_group(v):
        return jnp.concatenate([pad(v[:, :HEAD_DIM]), pad(v[:, HEAD_DIM:])], axis=1)

    w_idx_s = w_idx * (IDX_HEADS ** -0.5 * IDX_DIM ** -0.5)
    cols = [qa_pad, kc, vc, ks, per_group(vs), kw, per_group(vw), pad(g_nsa), q_b * scale, ckv, qi_pad, pad(k_idx),
            pad(w_idx_s), ga, gb]
    w = jnp.concatenate(cols, axis=1)
    assert w.shape[1] == PROJ_W
    return w.astype(BF16)


def _proj_kernel(x_ref, w_ref, wuk_ref, g_ref, qa_o, kc_o, vc_o, kv_o, gn_o, qlat_o, ckv_o, qidx_o, kidx_o,
                 widx_o, sga_o, sgb_o):
    xb = x_ref[...].astype(BF16)

    def mm(lo, n):
        return _dot(xb, w_ref[:, lo:lo + n])

    qa_o[...] = mm(O_QA, 1024).astype(BF16)
    kc_o[...] = mm(O_KC, 128).astype(BF16)
    vc_o[...] = mm(O_VC, 128).astype(BF16)
    lane = lax.broadcasted_iota(I32, (x_ref.shape[0], KV_W), 1)
    is_one = ((lane & (LANE - 1)) == ONES_LANE) & (((lane >= KV_VS) & (lane < KV_KW)) | (lane >= KV_VW))
    kv_o[...] = jnp.where(is_one, 1.0, mm(O_KV, KV_W)).astype(BF16)
    gn_o[...] = jax.nn.sigmoid(mm(O_GN, 128))
    qb = mm(O_QB, 512).astype(BF16)
    for k in range(DSA_HEADS // 2):
        qlat_o[:, 256 * k:256 * (k + 1)] = _dot(qb[:, 128 * k:128 * (k + 1)], wuk_ref[k]).astype(BF16)
    c = mm(O_CKV, 128)
    ms = jnp.mean(c * c, axis=-1, keepdims=True)
    ckv_o[...] = (c * lax.rsqrt(ms + 1e-6) * g_ref[...]).astype(BF16)
    qidx_o[...] = mm(O_QIDX, 512).astype(BF16)
    kidx_o[...] = mm(O_KIDX, 128).astype(BF16)
    widx_o[...] = mm(O_WIDX, 128)
    sga_o[...] = jax.nn.sigmoid(mm(O_GA, 1024)).astype(BF16)
    sgb_o[...] = jax.nn.sigmoid(mm(O_GB, 1024)).astype(BF16)


def _proj(x2, w_pad, wuk_pairs, ckv_g, rows=512):
    n, d = x2.shape
    widths = [(1024, BF16), (128, BF16), (128, BF16), (KV_W, BF16), (128, F32), (1024, BF16), (128, BF16),
              (512, BF16), (128, BF16), (128, F32), (1024, BF16), (1024, BF16)]
    return pl.pallas_call(
        _proj_kernel,
        grid=(n // rows,),
        in_specs=[
            pl.BlockSpec((rows, d), lambda i: (i, 0)),
            pl.BlockSpec((d, PROJ_W), lambda i: (0, 0)),
            pl.BlockSpec((DSA_HEADS // 2, 128, 256), lambda i: (0, 0, 0)),
            pl.BlockSpec((1, KV_RANK), lambda i: (0, 0)),
        ],
        out_specs=[pl.BlockSpec((rows, w), lambda i: (i, 0)) for w, _ in widths],
        out_shape=[jax.ShapeDtypeStruct((n, w), dt) for w, dt in widths],
        compiler_params=_cparams(("arbitrary",), 56),
        name="proj",
    )(x2, w_pad, wuk_pairs, ckv_g)


def _cmp_weights(w1, w2, pe, out_lanes):
    half = CMP_BLOCK // 2
    w1r = w1.reshape(CMP_BLOCK, HEAD_DIM, CMP_HIDDEN)
    eye = jnp.eye(NSA_GROUPS, dtype=F32)

    def expand(wl):
        return jnp.einsum('ldj,gh->lgdhj', wl, eye).reshape(half * NSA_GROUPS * HEAD_DIM, NSA_GROUPS * CMP_HIDDEN)

    top, bot = expand(w1r[:half]), expand(w1r[half:])

    def pe_rows(p):
        return jnp.broadcast_to(p[:, None, :], (half, NSA_GROUPS, HEAD_DIM)).reshape(1, -1)

    w2p = w2 if out_lanes == HEAD_DIM else jnp.pad(w2, ((0, 0), (0, out_lanes - HEAD_DIM)))
    w2bd = jnp.einsum('jd,gh->gjhd', w2p, eye).reshape(NSA_GROUPS * CMP_HIDDEN, NSA_GROUPS * out_lanes)
    return (top.astype(BF16), bot.astype(BF16), pe_rows(pe[:half]).astype(BF16), pe_rows(pe[half:]).astype(BF16),
            w2bd.astype(BF16))


def _compress_kernel(hk_ref, hv_ref, kt_ref, kb_ref, kpt_ref, kpb_ref, k2_ref, vt_ref, vb_ref, vpt_ref, vpb_ref,
                     v2_ref, ko_ref, vo_ref):
    ncp = hk_ref.shape[0]

    def one(h_ref, top_ref, bot_ref, pt_ref, pb_ref, w2_ref, o_ref):
        h = h_ref[...]
        a = _dot(h, top_ref[...])
        b = _dot(h, bot_ref[...])
        pe8t = jnp.broadcast_to(pt_ref[...], (8, pt_ref.shape[1]))
        pe8b = jnp.broadcast_to(pb_ref[...], (8, pb_ref.shape[1]))
        pe_term = (_dot(pe8t, top_ref[...]) + _dot(pe8b, bot_ref[...]))[0:1]
        pre = a + pltpu.roll(b, ncp - 1, 0) + pe_term
        hid = jax.nn.gelu(pre, approximate=True)
        o_ref[...] = _dot(hid.astype(BF16), w2_ref[...]).astype(BF16)

    one(hk_ref, kt_ref, kb_ref, kpt_ref, kpb_ref, k2_ref, ko_ref)
    one(hv_ref, vt_ref, vb_ref, vpt_ref, vpb_ref, v2_ref, vo_ref)


def _compress(kc, vc, wk, wv, b, t):
    ncp = t // CMP_STRIDE
    hw = CMP_STRIDE * NSA_GROUPS * HEAD_DIM
    hk = kc.reshape(b, ncp, hw)
    hv = vc.reshape(b, ncp, hw)
    hspec = pl.BlockSpec((None, ncp, hw), lambda i: (i, 0, 0))

    def full(a):
        return pl.BlockSpec(a.shape, lambda i: (0,) * a.ndim)

    widths = (wk[-1].shape[1], wv[-1].shape[1])
    return pl.pallas_call(
        _compress_kernel,
        grid=(b,),
        in_specs=[hspec, hspec] + [full(a) for a in wk] + [full(a) for a in wv],
        out_specs=[pl.BlockSpec((None, ncp, w), lambda i: (i, 0, 0)) for w in widths],
        out_shape=[jax.ShapeDtypeStruct((b, ncp, w), BF16) for w in widths],
        compiler_params=_cparams(("arbitrary",), 48),
        name="compress",
    )(hk, hv, *wk, *wv)


def _rel_bucket(dist):
    n = jnp.maximum(dist, 0)
    nf = jnp.maximum(n, 1).astype(F32)
    large = REL_EXACT + (jnp.log(nf / REL_EXACT) / math.log(REL_MAX_DIST / REL_EXACT)
                         * (REL_BUCKETS - REL_EXACT)).astype(I32)
    return jnp.where(n < REL_EXACT, n, jnp.minimum(large, REL_BUCKETS - 1))


def _rel_lookup(tab_ref, bucket, col):
    bits = [(bucket & (1 << k)) != 0 for k in range(5)]
    vals = [jnp.where(bits[0], tab_ref[2 * k + 1, col], tab_ref[2 * k, col]) for k in range(REL_BUCKETS // 2)]
    for lvl in range(1, 5):
        vals = [jnp.where(bits[lvl], vals[2 * k + 1], vals[2 * k]) for k in range(len(vals) // 2)]
    return vals[0]


def _strip_kernel(tab_ref, o_ref, *, a, window, head0, rel_far):
    h = pl.program_id(0)
    shape = o_ref.shape
    r = lax.broadcasted_iota(I32, shape, 0)
    j = lax.broadcasted_iota(I32, shape, 1)
    dist = r + a - j
    valid = dist >= 0
    if window is not None:
        valid = valid & (dist < window)
    val = _rel_lookup(tab_ref, _rel_bucket(dist), h + head0)
    if rel_far:
        val = val - tab_ref[REL_BUCKETS - 1, h + head0]
    o_ref[...] = jnp.where(valid, val * LOG2E, NEG)


def _bias_strip(rel_bias, n_heads, head0, a, width, window, rel_far):
    return pl.pallas_call(
        functools.partial(_strip_kernel, a=a, window=window, head0=head0, rel_far=rel_far),
        grid=(n_heads,),
        in_specs=[pl.BlockSpec(memory_space=pltpu.SMEM)],
        out_specs=pl.BlockSpec((None, QT, width), lambda h: (h, 0, 0)),
        out_shape=jax.ShapeDtypeStruct((n_heads, QT, width), F32),
        compiler_params=_cparams(("arbitrary",)),
        name="bias_strip",
    )(rel_bias)


def _cmp_bias_kernel(tab_ref, o_ref, *, nc):
    i = pl.program_id(0)
    shape = o_ref.shape[1:]
    r = lax.broadcasted_iota(I32, shape, 0)
    c = lax.broadcasted_iota(I32, shape, 1)
    dist = i * QT + r - (CMP_STRIDE * c + CMP_BLOCK - 1)
    valid = (dist >= 0) & (c < nc)
    bucket = _rel_bucket(dist)
    for h in range(NSA_HEADS):
        o_ref[h] = jnp.where(valid, _rel_lookup(tab_ref, bucket, h) * LOG2E, NEG)


def _cmp_bias(rel_bias, t):
    n_qt = t // QT
    ncp = t // CMP_STRIDE
    nc = (t - CMP_BLOCK) // CMP_STRIDE + 1
    return pl.pallas_call(
        functools.partial(_cmp_bias_kernel, nc=nc),
        grid=(n_qt,),
        in_specs=[pl.BlockSpec(memory_space=pltpu.SMEM)],
        out_specs=pl.BlockSpec((None, NSA_HEADS, QT, ncp), lambda i: (i, 0, 0, 0)),
        out_shape=jax.ShapeDtypeStruct((n_qt, NSA_HEADS, QT, ncp), F32),
        compiler_params=_cparams(("arbitrary",)),
        name="cmp_bias",
    )(rel_bias)


def _lane_groups(op, x):
    parts = [x[:, c * LANE:(c + 1) * LANE] for c in range(x.shape[1] // LANE)]
    while len(parts) > 1:
        parts = [op(parts[j], parts[j + 1]) for j in range(0, len(parts), 2)]
    return parts[0]


def _far_tiles(i):
    return jnp.maximum((i - (FAR_TILES - TPK)) >> (TPK.bit_length() - 1), 0)


def _strip_offset(i, kt):
    return pl.multiple_of(QT * jnp.maximum(FAR_TILES - (i - TPK * kt), 0), LANE)


def _for_tiles(lo, hi, step):
    n = hi - lo

    def pair(j, c):
        step(lo + 2 * j)
        step(lo + 2 * j + 1)
        return c

    lax.fori_loop(0, n >> 1, pair, 0)

    @pl.when((n & 1) == 1)
    def _():
        step(hi - 1)


def _nsa_kernel(qa_ref, gn_ref, kc_ref, vc_ref, kv_ref, sslc_ref, swin_ref, bcmp_ref, mapt_ref, eall_ref, o_ref,
                q_ref, s_ref, z_ref, p_ref, m_ref, acc_ref, ocmp_ref, sel_ref, st_ref, *, t):
    i = pl.program_id(1)
    q0 = i * QT
    ns = t // SLC_BLOCK
    ncp = t // CMP_STRIDE
    n_top = min(SLC_TOPN, ns)
    grows = NSA_HPG * QT
    arows = NSA_HEADS * QT
    shift = SLC_BLOCK.bit_length() - 1
    halves = QT // RB

    for h in range(NSA_HEADS):
        q_ref[h * QT:(h + 1) * QT, :] = qa_ref[:, h * LANE:(h + 1) * LANE]

    s_ref[:, 0:ncp] = _dot_nt(q_ref[...], kc_ref[...])
    imp = []
    for g in range(NSA_GROUPS):
        psum = None
        for h in range(NSA_HPG * g, NSA_HPG * (g + 1)):
            rows = slice(h * QT, (h + 1) * QT)
            z = s_ref[rows, 0:ncp] + bcmp_ref[h]
            m = jnp.maximum(jnp.max(z, axis=-1, keepdims=True), M_FLOOR)
            e = jnp.exp2(z - m)
            p = e / jnp.maximum(jnp.sum(e, axis=-1, keepdims=True), 1e-30)
            p_ref[rows, 0:ncp] = p.astype(BF16)
            psum = p if psum is None else psum + p
        grp = slice(g * grows, (g + 1) * grows)
        ocmp_ref[grp, :] = _dot(p_ref[grp, 0:ncp], vc_ref[:, g * LANE:(g + 1) * LANE])
        imp.append(_dot_nt(mapt_ref[...], psum.astype(BF16)))
    imp_t = jnp.concatenate(imp, axis=1)

    blk = lax.broadcasted_iota(I32, (ns, NSA_GROUPS * QT), 0)
    tq = q0 + (lax.broadcasted_iota(I32, (ns, NSA_GROUPS * QT), 1) & (QT - 1))
    cur = tq >> shift
    forced = (blk == 0) | (blk == cur) | (blk == cur - 1)
    avail = (blk << shift) <= tq
    score = jnp.where(avail, imp_t + jnp.where(forced, SLC_FORCE, 0.0), NEG)
    st_ref[0:ns, :] = score

    sub8 = 8
    groups = [score[v * sub8:(v + 1) * sub8] for v in range(ns // sub8)]
    blk8 = lax.broadcasted_iota(I32, (sub8, NSA_GROUPS * QT), 0)
    ranks = [jnp.zeros((sub8, NSA_GROUPS * QT), F32) for _ in groups]
    for jp in range(ns):
        rowb = jnp.broadcast_to(st_ref[jp:jp + 1, :], (sub8, NSA_GROUPS * QT))
        for v, sv in enumerate(groups):
            ge = jnp.where(rowb >= sv, 1.0, 0.0)
            gt = jnp.where(rowb > sv, 1.0, 0.0)
            if v * sub8 > jp:
                inc = ge
            elif v * sub8 + sub8 - 1 <= jp:
                inc = gt
            else:
                inc = jnp.where(blk8 + v * sub8 > jp, ge, gt)
            ranks[v] = ranks[v] + inc
    rank = jnp.concatenate(ranks, axis=0)
    sel_t = jnp.where((rank < n_top) & avail, 1.0, 0.0)
    if ns < LANE:
        sel_t = jnp.concatenate([sel_t, jnp.zeros((LANE - ns, NSA_GROUPS * QT), F32)], axis=0)
    for g in range(NSA_GROUPS):
        sel_ref[g] = sel_t[:, g * QT:(g + 1) * QT].T.astype(BF16)

    m_ref[...] = jnp.full((arows, LANE), M_FLOOR, F32)
    acc_ref[...] = jnp.zeros((arows, LANE), F32)

    def max_step(kt, near):
        k0 = pl.multiple_of(kt * KT, KT)
        s = _dot_nt(q_ref[...], kv_ref[pl.ds(k0, KT), KV_KS:KV_KS + LANE])
        for g in range(NSA_GROUPS):
            madd = (_dot(sel_ref[g], eall_ref[:, pl.ds(k0, KT)]) - 1.0) * (-NEG)
            for h in range(NSA_HPG * g, NSA_HPG * (g + 1)):
                for half in range(halves):
                    qrows = slice(half * RB, (half + 1) * RB)
                    rows = slice(h * QT + half * RB, h * QT + (half + 1) * RB)
                    z = s[rows] + madd[qrows]
                    if near:
                        z = z + sslc_ref[h, qrows, pl.ds(_strip_offset(i, kt), KT)]
                    z_ref[rows, pl.ds(k0, KT)] = z
                    m_ref[rows, :] = jnp.maximum(m_ref[rows, :], _lane_groups(jnp.maximum, z))

    n_far = _far_tiles(i)
    n_kt = (i >> (TPK.bit_length() - 1)) + 1
    _for_tiles(0, n_far, lambda kt: max_step(kt, False))
    _for_tiles(n_far, n_kt, lambda kt: max_step(kt, True))
    m_ref[...] = jnp.broadcast_to(jnp.max(m_ref[...], axis=-1, keepdims=True), (arows, LANE))

    def sum_step(kt):
        k0 = pl.multiple_of(kt * KT, KT)
        for g in range(NSA_GROUPS):
            ps = []
            for h in range(NSA_HPG * g, NSA_HPG * (g + 1)):
                for half in range(halves):
                    rows = slice(h * QT + half * RB, h * QT + (half + 1) * RB)
                    m = m_ref[rows, :]
                    z = z_ref[rows, pl.ds(k0, KT)]
                    ps.append(jnp.exp2(z - jnp.concatenate([m] * (KT // LANE), axis=1)).astype(BF16))
            grp = slice(g * grows, (g + 1) * grows)
            acc_ref[grp, :] += _dot(jnp.concatenate(ps, axis=0),
                                    kv_ref[pl.ds(k0, KT), KV_VS + g * LANE:KV_VS + (g + 1) * LANE])

    _for_tiles(0, n_kt, sum_step)

    ks0 = pl.multiple_of(jnp.maximum(i - WINDOW // QT, 0) * QT, QT)
    woff = pl.multiple_of(jnp.maximum(WINDOW // QT - i, 0) * QT, LANE)
    s_ref[...] = _dot_nt(q_ref[...], kv_ref[pl.ds(ks0, WIN_KEYS), KV_KW:KV_KW + LANE])
    for h in range(NSA_HEADS):
        for half in range(halves):
            qrows = slice(half * RB, (half + 1) * RB)
            rows = slice(h * QT + half * RB, h * QT + (half + 1) * RB)
            z = s_ref[rows, :] + swin_ref[h, qrows, pl.ds(woff, WIN_KEYS)]
            p_ref[rows, :] = jnp.exp2(z - jnp.max(z, axis=-1, keepdims=True)).astype(BF16)

    lane_ok = lax.broadcasted_iota(I32, (QT, LANE), 1) < HEAD_DIM
    for g in range(NSA_GROUPS):
        grp = slice(g * grows, (g + 1) * grows)
        o_win = _dot(p_ref[grp, :], kv_ref[pl.ds(ks0, WIN_KEYS), KV_VW + g * LANE:KV_VW + (g + 1) * LANE])
        for hp in range(NSA_HPG):
            h = NSA_HPG * g + hp
            rows = slice(h * QT, (h + 1) * QT)
            wrows = slice(hp * QT, (hp + 1) * QT)
            slc = acc_ref[rows, :]
            o_slc = slc / jnp.maximum(slc[:, ONES_LANE:ONES_LANE + 1], 1e-30)
            win = o_win[wrows]
            o_w = win / jnp.maximum(win[:, ONES_LANE:ONES_LANE + 1], 1e-30)
            o = (gn_ref[:, 3 * h:3 * h + 1] * ocmp_ref[rows, :] + gn_ref[:, 3 * h + 1:3 * h + 2] * o_slc
                 + gn_ref[:, 3 * h + 2:3 * h + 3] * o_w)
            o_ref[:, h * LANE:(h + 1) * LANE] = jnp.where(lane_ok, o, 0.0).astype(BF16)


def _nsa(qa, gn, kcmp, vcmp, kv, sslc, swin, bcmp, mapt, eall, b, t):
    n_qt = t // QT
    ncp = t // CMP_STRIDE
    ns = t // SLC_BLOCK
    arows = NSA_HEADS * QT
    assert ncp <= WIN_KEYS

    def full(a):
        return pl.BlockSpec(a.shape, lambda bi, i: (0,) * a.ndim)

    def once(a):
        return pl.BlockSpec(a.shape, lambda bi, i: (0,) * a.ndim, pipeline_mode=pl.Buffered(1))

    return pl.pallas_call(
        functools.partial(_nsa_kernel, t=t),
        grid=(b, n_qt),
        in_specs=[
            pl.BlockSpec((None, QT, NSA_HEADS * LANE), lambda bi, i: (bi, i, 0)),
            pl.BlockSpec((None, QT, LANE), lambda bi, i: (bi, i, 0)),
            pl.BlockSpec((None, ncp, LANE), lambda bi, i: (bi, 0, 0)),
            pl.BlockSpec((None, ncp, NSA_GROUPS * LANE), lambda bi, i: (bi, 0, 0)),
            pl.BlockSpec((None, t, KV_W), lambda bi, i: (bi, 0, 0), pipeline_mode=pl.Buffered(1)),
            once(sslc), once(swin),
            pl.BlockSpec((None, NSA_HEADS, QT, ncp), lambda bi, i: (i, 0, 0, 0)),
            full(mapt), once(eall),
        ],
        out_specs=pl.BlockSpec((None, QT, NSA_HEADS * LANE), lambda bi, i: (bi, i, 0)),
        out_shape=jax.ShapeDtypeStruct((b, t, NSA_HEADS * LANE), BF16),
        scratch_shapes=[
            pltpu.VMEM((arows, LANE), BF16),
            pltpu.VMEM((arows, WIN_KEYS), F32),
            pltpu.VMEM((arows, t), F32),
            pltpu.VMEM((arows, WIN_KEYS), BF16),
            pltpu.VMEM((arows, LANE), F32),
            pltpu.VMEM((arows, LANE), F32),
            pltpu.VMEM((arows, LANE), F32),
            pltpu.VMEM((NSA_GROUPS, QT, LANE), BF16),
            pltpu.VMEM((max(ns, 8), NSA_GROUPS * QT), F32),
        ],
        compiler_params=_cparams(("arbitrary", "arbitrary"), 56),
        name="nsa",
    )(qa, gn, kcmp, vcmp, kv, sslc, swin, bcmp, mapt, eall)


IDX_CHUNK = 512
FAST_PASSES = 12
LANE_SHIFT = LANE.bit_length() - 1


def _dsa_kernel(ql_ref, qi_ref, wi_ref, ki_ref, ckv_ref, strip_ref, wuv_ref, tri_ref, o_ref,
                q_ref, idx_ref, tmp_ref, s_ref, p_ref, m_ref, l_ref, acc_ref, *, t, k_sel):
    i = pl.program_id(1)
    q0 = i * QT
    nch = (i >> 2) + 1
    hrows = DSA_HEADS * QT
    sub = IDX_CHUNK // LANE
    halves = QT // RB

    for h in range(IDX_HEADS):
        q_ref[h * QT:(h + 1) * QT, :] = qi_ref[:, h * LANE:(h + 1) * LANE]
    tq = q0 + lax.broadcasted_iota(I32, (QT, IDX_CHUNK), 0)
    col = lax.broadcasted_iota(I32, (QT, IDX_CHUNK), 1)

    def idx_body(c, carry):
        c0 = pl.multiple_of(c * IDX_CHUNK, IDX_CHUNK)
        d = jnp.maximum(_dot_nt(q_ref[0:IDX_HEADS * QT, :], ki_ref[pl.ds(c0, IDX_CHUNK), :]), 0.0)
        acc = d[0:QT] * wi_ref[:, 0:1]
        for h in range(1, IDX_HEADS):
            acc = acc + d[h * QT:(h + 1) * QT] * wi_ref[:, h:h + 1]
        idx_ref[:, pl.ds(c0, IDX_CHUNK)] = jnp.where(col + c0 <= tq, acc, NEG)
        return carry

    lax.fori_loop(0, nch, idx_body, 0)

    zeros = jnp.zeros((QT, LANE), F32)

    def scan(fn, init):
        def body(c, carry):
            for s in range(sub):
                off = pl.multiple_of(c * IDX_CHUNK + s * LANE, LANE)
                carry = fn(idx_ref[:, pl.ds(off, LANE)], off, carry)
            return carry
        return lax.fori_loop(0, nch, body, init)

    def search(_):
        kf = float(k_sel)
        big = jnp.full((QT, LANE), BIG, F32)

        def count_gt(pivot):
            pb = jnp.broadcast_to(pivot, (QT, LANE))
            cnt = scan(lambda x, off, c: c + jnp.where(x > pb, 1.0, 0.0), zeros)
            return jnp.sum(cnt, axis=-1, keepdims=True)

        def init_fn(x, off, carry):
            lo, hi = carry
            return jnp.minimum(lo, jnp.where(x > 0.5 * NEG, x, BIG)), jnp.maximum(hi, x)

        row_min, row_max = scan(init_fn, (big, -big))
        row_min = jnp.min(row_min, axis=-1, keepdims=True)
        row_max = jnp.max(row_max, axis=-1, keepdims=True)
        n_valid = (q0 + 1 + lax.broadcasted_iota(I32, (QT, 1), 0)).astype(F32)

        def fast_step(_, c):
            lo, hi, g_lo, g_hi, side = c
            done = lo >= hi
            mid = lo + (hi - lo) * (g_lo / (g_lo - g_hi))
            mid = jnp.where((mid > lo) & (mid < hi), mid, lo + (hi - lo) * 0.5)
            cnt = count_gt(mid)
            up = cnt > kf
            above = cnt >= kf
            return (jnp.where(done | jnp.logical_not(above), lo, mid), jnp.where(done | up, hi, mid),
                    jnp.where(up, cnt - kf, jnp.where(side == 2, g_lo * 0.5, g_lo)),
                    jnp.where(up, jnp.where(side == 1, g_hi * 0.5, g_hi), cnt - kf),
                    jnp.where(up, 1, 2))

        lo0 = row_min - (jnp.abs(row_min) + 1.0)
        init = (lo0, row_max, n_valid - kf, jnp.full((QT, 1), -kf, F32), jnp.zeros((QT, 1), I32))
        lo, hi, _, _, _ = lax.fori_loop(0, FAST_PASSES, fast_step, init)
        done = lo >= hi

        def exact(_):
            lob = jnp.broadcast_to(lo, (QT, LANE))
            hib = jnp.broadcast_to(hi, (QT, LANE))

            def snap_fn(x, off, c):
                a, b_, n_lo, n_hi = c
                above_lo = x > lob
                above_hi = x > hib
                return (jnp.minimum(a, jnp.where(above_lo, x, BIG)), jnp.maximum(b_, jnp.where(above_hi, -BIG, x)),
                        n_lo + jnp.where(above_lo, 1.0, 0.0), n_hi + jnp.where(above_hi, 1.0, 0.0))

            a, b_, n_lo, n_hi = scan(snap_fn, (big, -big, zeros, zeros))
            lo_d = jnp.where(done, lo, jnp.min(a, axis=-1, keepdims=True))
            hi_d = jnp.where(done, hi, jnp.max(b_, axis=-1, keepdims=True))
            f_lo = jnp.sum(n_lo, axis=-1, keepdims=True)
            f_hi = jnp.sum(n_hi, axis=-1, keepdims=True)

            def cond(c):
                return jnp.max(jnp.where(c[0] < c[1], 1, 0)) > 0

            def step(c):
                lo, hi, c_lo, c_hi = c
                mid = lo + (hi - lo) * 0.5
                mid = jnp.where(mid < hi, mid, lo)
                midb = jnp.broadcast_to(mid, (QT, LANE))

                def fn(x, off, cc):
                    cnt, amin, bmax = cc
                    gt = x > midb
                    return (cnt + jnp.where(gt, 1.0, 0.0), jnp.minimum(amin, jnp.where(gt, x, BIG)),
                            jnp.maximum(bmax, jnp.where(gt, -BIG, x)))

                cnt, amin, bmax = scan(fn, (zeros, big, -big))
                cnt = jnp.sum(cnt, axis=-1, keepdims=True)
                amin = jnp.min(amin, axis=-1, keepdims=True)
                bmax = jnp.max(bmax, axis=-1, keepdims=True)
                up = cnt >= kf
                same = lo >= hi
                return (jnp.where(same | jnp.logical_not(up), lo, amin), jnp.where(same | up, hi, bmax),
                        jnp.where(same | jnp.logical_not(up), c_lo, cnt), jnp.where(same | up, c_hi, cnt))

            thr, _, c_ge, c_gt = lax.while_loop(cond, step, (lo_d, hi_d, f_lo, f_hi))
            return thr, c_ge, c_gt

        all_done = jnp.min(jnp.where(done, 1, 0)) > 0
        k_col = jnp.full((QT, 1), kf, F32)
        thr, c_ge, c_gt = lax.cond(all_done, lambda _: (lo, k_col, k_col), exact, 0)
        c_ge = jnp.where(done, kf, c_ge)
        c_gt = jnp.where(done, kf, c_gt)
        thrb = jnp.broadcast_to(thr, (QT, LANE))
        need = k_sel - c_gt

        def tie_search(_):
            lane = lax.broadcasted_iota(I32, (QT, LANE), 1)
            lanef = lane.astype(F32)

            def grp_fn(x, off, g_cnt):
                c = jnp.sum(jnp.where(x == thrb, 1.0, 0.0), axis=-1, keepdims=True)
                return jnp.where(lane == (off >> LANE_SHIFT), c, g_cnt)

            g_cum = _dot(scan(grp_fn, zeros).astype(BF16), tri_ref[...])
            g_star = jnp.sum(jnp.where(g_cum < need, 1.0, 0.0), axis=-1, keepdims=True)
            before = jnp.sum(jnp.where(lanef == g_star - 1.0, g_cum, 0.0), axis=-1, keepdims=True)

            def slab_fn(x, off, slab):
                return jnp.where(g_star == (off >> LANE_SHIFT).astype(F32), x, slab)

            slab = scan(slab_fn, jnp.full((QT, LANE), NEG, F32))
            pre = _dot(jnp.where(slab == thrb, 1.0, 0.0).astype(BF16), tri_ref[...])
            lane_cut = jnp.sum(jnp.where(pre < need - before, 1.0, 0.0), axis=-1, keepdims=True)
            return (g_star * LANE + lane_cut).astype(I32)

        any_tie = jnp.max(jnp.where(c_ge > k_sel, 1, 0)) > 0
        p_cut = lax.cond(any_tie, tie_search, lambda _: jnp.full((QT, 1), t, I32), 0)
        return thr, jnp.where(done, -1, p_cut)

    thr, p_cut = lax.cond(q0 >= k_sel, search,
                          lambda _: (jnp.full((QT, 1), M_FLOOR, F32), jnp.full((QT, 1), t, I32)), 0)
    thrk = jnp.broadcast_to(thr, (QT, KT))
    pcutk = jnp.broadcast_to(p_cut, (QT, KT))
    colk = lax.broadcasted_iota(I32, (QT, KT), 1)

    for h in range(DSA_HEADS):
        q_ref[h * QT:(h + 1) * QT, :] = ql_ref[:, h * LANE:(h + 1) * LANE]
    m_ref[...] = jnp.full((hrows, LANE), M_FLOOR, F32)
    l_ref[...] = jnp.zeros((hrows, LANE), F32)
    acc_ref[...] = jnp.zeros((hrows, LANE), F32)
    blocks = [(h, half) for h in range(DSA_HEADS) for half in range(halves)]

    n_far = _far_tiles(i)
    n_kt = (i >> (TPK.bit_length() - 1)) + 1

    def max_step(kt, near):
        k0 = pl.multiple_of(kt * KT, KT)
        s = _dot_nt(q_ref[...], ckv_ref[pl.ds(k0, KT), :])
        x = idx_ref[:, pl.ds(k0, KT)]
        keep = (x > thrk) | ((x == thrk) & (colk + k0 <= pcutk))
        selm = jnp.where(keep, 0.0, NEG)
        for h, half in blocks:
            qrows = slice(half * RB, (half + 1) * RB)
            rows = slice(h * QT + half * RB, h * QT + (half + 1) * RB)
            z = s[rows] + selm[qrows]
            if near:
                z = z + strip_ref[h, qrows, pl.ds(_strip_offset(i, kt), KT)]
            s_ref[rows, pl.ds(k0, KT)] = z
            m_ref[rows, :] = jnp.maximum(m_ref[rows, :], _lane_groups(jnp.maximum, z))

    _for_tiles(0, n_far, lambda kt: max_step(kt, False))
    _for_tiles(n_far, n_kt, lambda kt: max_step(kt, True))
    m_ref[...] = jnp.broadcast_to(jnp.max(m_ref[...], axis=-1, keepdims=True), (hrows, LANE))

    def sum_step(kt):
        k0 = pl.multiple_of(kt * KT, KT)
        ps = []
        for h, half in blocks:
            rows = slice(h * QT + half * RB, h * QT + (half + 1) * RB)
            m = m_ref[rows, :]
            p = jnp.exp2(s_ref[rows, pl.ds(k0, KT)] - jnp.concatenate([m] * (KT // LANE), axis=1))
            l_ref[rows, :] += _lane_groups(jnp.add, p)
            ps.append(p.astype(BF16))
        acc_ref[...] += _dot(jnp.concatenate(ps, axis=0), ckv_ref[pl.ds(k0, KT), :])

    _for_tiles(0, n_kt, sum_step)
    for h in range(DSA_HEADS):
        rows = slice(h * QT, (h + 1) * QT)
        l = jnp.sum(l_ref[rows, :], axis=-1, keepdims=True)
        o_lat = (acc_ref[rows, :] / jnp.maximum(l, 1e-30)).astype(BF16)
        o_ref[:, h * LANE:(h + 1) * LANE] = _dot(o_lat, wuv_ref[h]).astype(BF16)


def _dsa(qlat, qidx, widx, kidx, ckvn, strip, wuv_pad, b, t):
    n_qt = t // QT
    k_sel = min(IDX_TOPK_MAX, t // 4)
    assert k_sel % QT == 0 and t % IDX_CHUNK == 0 and t // LANE <= LANE
    hrows = DSA_HEADS * QT
    tri = jnp.asarray(np.triu(np.ones((LANE, LANE), np.float32)), BF16)

    def full(a):
        return pl.BlockSpec(a.shape, lambda bi, i: (0,) * a.ndim)

    return pl.pallas_call(
        functools.partial(_dsa_kernel, t=t, k_sel=k_sel),
        grid=(b, n_qt),
        in_specs=[
            pl.BlockSpec((None, QT, DSA_HEADS * LANE), lambda bi, i: (bi, i, 0)),
            pl.BlockSpec((None, QT, IDX_HEADS * LANE), lambda bi, i: (bi, i, 0)),
            pl.BlockSpec((None, QT, LANE), lambda bi, i: (bi, i, 0)),
            pl.BlockSpec((None, t, LANE), lambda bi, i: (bi, 0, 0)),
            pl.BlockSpec((None, t, LANE), lambda bi, i: (bi, 0, 0)),
            pl.BlockSpec(strip.shape, lambda bi, i: (0, 0, 0), pipeline_mode=pl.Buffered(1)),
            full(wuv_pad), full(tri),
        ],
        out_specs=pl.BlockSpec((None, QT, DSA_HEADS * LANE), lambda bi, i: (bi, i, 0)),
        out_shape=jax.ShapeDtypeStruct((b, t, DSA_HEADS * LANE), BF16),
        scratch_shapes=[
            pltpu.VMEM((hrows, LANE), BF16),
            pltpu.VMEM((QT, t), F32),
            pltpu.VMEM((hrows, KT), F32),
            pltpu.VMEM((hrows, t), F32),
            pltpu.VMEM((hrows, KT), BF16),
            pltpu.VMEM((hrows, LANE), F32),
            pltpu.VMEM((hrows, LANE), F32),
            pltpu.VMEM((hrows, LANE), F32),
        ],
        compiler_params=_cparams(("arbitrary", "arbitrary"), 56),
        name="dsa",
    )(qlat, qidx, widx, kidx, ckvn, strip, wuv_pad, tri)


RT_GRP = 0
RT_EXP = 32


def _layer_norm(y, g, b):
    mu = jnp.mean(y, axis=-1, keepdims=True)
    yc = y - mu
    var = jnp.mean(yc * yc, axis=-1, keepdims=True)
    return yc * lax.rsqrt(var + 1e-5) * g + b


def _post_kernel(x_ref, oa_ref, ob_ref, sga_ref, sgb_ref, wa_ref, wb_ref, wo_ref, g_ref, b_ref, wr_ref, br_ref,
                 h_ref, rt_ref, rw_ref, *, alpha):
    rows = x_ref.shape[0]
    merged = (sga_ref[...].astype(F32) * _dot(oa_ref[...], wa_ref[...])
              + sgb_ref[...].astype(F32) * _dot(ob_ref[...], wb_ref[...]))
    y = alpha * x_ref[...] + _dot(merged.astype(BF16), wo_ref[...])
    h = _layer_norm(y, g_ref[...], b_ref[...])
    h_ref[...] = h

    h_hi = h.astype(BF16)
    h_lo = (h - h_hi.astype(F32)).astype(BF16)
    zz = _dot(h_hi, wr_ref[...])
    z = zz[:, 0:LANE] + zz[:, LANE:2 * LANE] + _dot(h_lo, wr_ref[:, 0:LANE]) + br_ref[...]
    lane = lax.broadcasted_iota(I32, (rows, LANE), 1)
    is_g = lane < N_EXPERT_GROUPS
    zg = jnp.where(is_g, z, -BIG)
    gmax = jnp.max(zg, axis=-1, keepdims=True)
    g_sel = jnp.min(jnp.where(is_g & (z == gmax), lane, LANE), axis=-1, keepdims=True)
    p_grp = 1.0 / jnp.sum(jnp.where(is_g, jnp.exp(zg - gmax), 0.0), axis=-1, keepdims=True)
    in_grp = (lane >= RT_EXP) & (lane < RT_EXP + N_EXPERTS) & (((lane - RT_EXP) >> 3) == g_sel)
    ze = jnp.where(in_grp, z, -BIG)
    m1 = jnp.max(ze, axis=-1, keepdims=True)
    i1 = jnp.min(jnp.where(in_grp & (z == m1), lane, LANE), axis=-1, keepdims=True)
    ze2 = jnp.where(lane == i1, -BIG, ze)
    m2 = jnp.max(ze2, axis=-1, keepdims=True)
    i2 = jnp.min(jnp.where(in_grp & (lane != i1) & (z == m2), lane, LANE), axis=-1, keepdims=True)
    e21 = jnp.exp(m2 - m1)
    den = 1.0 + e21
    w1 = p_grp * (1.0 / den)
    w2 = p_grp * (e21 / den)
    rt_ref[...] = jnp.where(lane == 0, i1 - RT_EXP, jnp.where(lane == 1, i2 - RT_EXP, 0))
    rw_ref[...] = jnp.where(lane == 0, w1, jnp.where(lane == 1, w2, 0.0))


def _post(x2, oa, ob, sga, sgb, wa, wb, wo, g, b_, wr, br, alpha, rows=512):
    n, d = x2.shape

    def row(w):
        return pl.BlockSpec((rows, w), lambda i: (i, 0))

    def full(a):
        return pl.BlockSpec(a.shape, lambda i: (0,) * a.ndim)

    return pl.pallas_call(
        functools.partial(_post_kernel, alpha=alpha),
        grid=(n // rows,),
        in_specs=[row(d), row(1024), row(1024), row(1024), row(1024), full(wa), full(wb), full(wo), full(g),
                  full(b_), full(wr), full(br)],
        out_specs=[row(d), row(LANE), row(LANE)],
        out_shape=[jax.ShapeDtypeStruct((n, d), F32), jax.ShapeDtypeStruct((n, LANE), I32),
                   jax.ShapeDtypeStruct((n, LANE), F32)],
        compiler_params=_cparams(("arbitrary",), 48),
        name="post",
    )(x2, oa, ob, sga, sgb, wa, wb, wo, g, b_, wr, br)


def _onehots(rt_ref, rows):
    lane = lax.broadcasted_iota(I32, (rows, LANE), 1)
    oh0 = jnp.where(lane == rt_ref[:, 0:1], 1.0, 0.0)
    oh1 = jnp.where(lane == rt_ref[:, 1:2], 1.0, 0.0)
    return oh0, oh1


def _rank_kernel(rt_ref, tri_ref, rank_ref, cnt_ref, carry_ref):
    rows = rt_ref.shape[0]

    @pl.when(pl.program_id(0) == 0)
    def _():
        carry_ref[...] = jnp.zeros_like(carry_ref)

    oh0, oh1 = _onehots(rt_ref, rows)
    both = oh0 + oh1
    before = _dot(tri_ref[...], both.astype(BF16)) + carry_ref[0:1, :]
    r0 = jnp.sum(oh0 * before, axis=-1, keepdims=True)
    r1 = jnp.sum(oh1 * before, axis=-1, keepdims=True)
    lane = lax.broadcasted_iota(I32, (rows, LANE), 1)
    rank_ref[...] = jnp.where(lane == 0, r0, jnp.where(lane == 1, r1, 0.0))
    carry_ref[...] = carry_ref[...] + jnp.sum(both, axis=0, keepdims=True)
    cnt_ref[...] = carry_ref[...]


def _moe_rank(rt, rows=512):
    n = rt.shape[0]
    tri = jnp.asarray(np.tril(np.ones((rows, rows), np.float32), -1), BF16)
    return pl.pallas_call(
        _rank_kernel,
        grid=(n // rows,),
        in_specs=[pl.BlockSpec((rows, LANE), lambda i: (i, 0)), pl.BlockSpec((rows, rows), lambda i: (0, 0))],
        out_specs=[pl.BlockSpec((rows, LANE), lambda i: (i, 0)), pl.BlockSpec((8, LANE), lambda i: (0, 0))],
        out_shape=[jax.ShapeDtypeStruct((n, LANE), F32), jax.ShapeDtypeStruct((8, LANE), F32)],
        scratch_shapes=[pltpu.VMEM((8, LANE), F32)],
        compiler_params=_cparams(("arbitrary",)),
        name="moe_rank",
    )(rt, tri)


def _lane_cumsum(v):
    lane = lax.broadcasted_iota(I32, v.shape, 1)
    s = 1
    while s < LANE:
        v = v + jnp.where(lane >= s, pltpu.roll(v, s, 1), 0.0)
        s *= 2
    return v


def _dest_kernel(rt_ref, rank_ref, cnt_ref, dest_ref, bexp_ref, *, n_blk_pad):
    rows = rt_ref.shape[0]
    lane8 = lax.broadcasted_iota(I32, (8, LANE), 1)
    cnt = jnp.where(lane8 < N_EXPERTS, cnt_ref[...], 0.0)
    padded = jnp.floor((cnt + (EXPERT_BLOCK - 1)) * (1.0 / EXPERT_BLOCK)) * EXPERT_BLOCK
    pend = _lane_cumsum(padded)
    poff = (pend - padded)[0:1, :]
    oh0, oh1 = _onehots(rt_ref, rows)
    d0 = jnp.sum(oh0 * poff, axis=-1, keepdims=True) + rank_ref[:, 0:1]
    d1 = jnp.sum(oh1 * poff, axis=-1, keepdims=True) + rank_ref[:, 1:2]
    lane = lax.broadcasted_iota(I32, (rows, LANE), 1)
    dest_ref[...] = jnp.where(lane == 0, d0, jnp.where(lane == 1, d1, 0.0)).astype(I32)

    lane_b = lax.broadcasted_iota(I32, (n_blk_pad, LANE), 1)
    start = (lax.broadcasted_iota(I32, (n_blk_pad, LANE), 0) * EXPERT_BLOCK).astype(F32)
    hit = jnp.where((lane_b < N_EXPERTS) & (pend[0:1, :] <= start), 1.0, 0.0)
    e_blk = jnp.minimum(jnp.sum(hit, axis=-1, keepdims=True), N_EXPERTS - 1.0)
    used = jnp.max(pend[0:1, :], axis=-1, keepdims=True) * (1.0 / EXPERT_BLOCK)
    bexp_ref[...] = jnp.where(lane_b == 0, e_blk, jnp.where(lane_b == 1, used, 0.0)).astype(I32)


def _moe_dest(rt, rank, cnt, n_blk, rows=1024):
    n = rt.shape[0]
    n_blk_pad = -(-n_blk // 8) * 8
    return pl.pallas_call(
        functools.partial(_dest_kernel, n_blk_pad=n_blk_pad),
        grid=(n // rows,),
        in_specs=[pl.BlockSpec((rows, LANE), lambda i: (i, 0)), pl.BlockSpec((rows, LANE), lambda i: (i, 0)),
                  pl.BlockSpec((8, LANE), lambda i: (0, 0))],
        out_specs=[pl.BlockSpec((rows, LANE), lambda i: (i, 0)), pl.BlockSpec((n_blk_pad, LANE), lambda i: (0, 0))],
        out_shape=[jax.ShapeDtypeStruct((n, LANE), I32), jax.ShapeDtypeStruct((n_blk_pad, LANE), I32)],
        compiler_params=_cparams(("arbitrary",)),
        name="moe_dest",
    )(rt, rank, cnt)


MOE_ROWS = 512
DMA_UNROLL = 8


def _dispatch_kernel(dest_ref, h_ref, xin_ref, xpad_ref, sem):
    del xin_ref

    def row_copy(r, d):
        return pltpu.make_async_copy(h_ref.at[pl.ds(r, 1)], xpad_ref.at[pl.ds(d, 1)], sem)

    def start(r, carry):
        for j in range(EXPERT_TOPK):
            row_copy(r, dest_ref[EXPERT_TOPK * r + j]).start(priority=j)
        return carry

    lax.fori_loop(0, MOE_ROWS, start, 0, unroll=DMA_UNROLL)

    def wait(r, carry):
        for j in range(EXPERT_TOPK):
            row_copy(r, dest_ref[EXPERT_TOPK * r + j]).wait()
        return carry

    lax.fori_loop(0, MOE_ROWS, wait, 0, unroll=DMA_UNROLL)


def _moe_dispatch(dest_flat, h, n_slots):
    n, d = h.shape
    zeros = jnp.zeros((n_slots, d), h.dtype)
    return pl.pallas_call(
        _dispatch_kernel,
        grid=(n // MOE_ROWS,),
        in_specs=[pl.BlockSpec((EXPERT_TOPK * MOE_ROWS,), lambda i: (i,), memory_space=pltpu.SMEM),
                  pl.BlockSpec((MOE_ROWS, d), lambda i: (i, 0)),
                  pl.BlockSpec(memory_space=pl.ANY)],
        out_specs=pl.BlockSpec(memory_space=pl.ANY),
        out_shape=jax.ShapeDtypeStruct((n_slots, d), h.dtype),
        scratch_shapes=[pltpu.SemaphoreType.DMA(())],
        input_output_aliases={2: 0},
        compiler_params=_cparams(("arbitrary",)),
        name="moe_dispatch",
    )(dest_flat, h, zeros)


def _expert_kernel(bexp_ref, used_ref, x_ref, wg_ref, wu_ref, wd_ref, y_ref):
    blk = pl.program_id(0)

    @pl.when(blk < used_ref[0])
    def _():
        xb = x_ref[...].astype(BF16)
        gate = _dot(xb, wg_ref[...].astype(BF16))
        up = _dot(xb, wu_ref[...].astype(BF16))
        act = (jax.nn.silu(gate) * up).astype(BF16)
        y_ref[...] = _dot(act, wd_ref[...].astype(BF16))

    @pl.when(blk >= used_ref[0])
    def _():
        y_ref[...] = jnp.zeros_like(y_ref)


def _moe_experts(bexp, used, xpad, w_gate, w_up, w_down):
    n_slots, d = xpad.shape
    n_blk = n_slots // EXPERT_BLOCK
    de = w_gate.shape[-1]
    grid_spec = pltpu.PrefetchScalarGridSpec(
        num_scalar_prefetch=2,
        grid=(n_blk,),
        in_specs=[
            pl.BlockSpec((EXPERT_BLOCK, d), lambda i, be, us: (i, 0)),
            pl.BlockSpec((None, d, de), lambda i, be, us: (be[i], 0, 0)),
            pl.BlockSpec((None, d, de), lambda i, be, us: (be[i], 0, 0)),
            pl.BlockSpec((None, de, d), lambda i, be, us: (be[i], 0, 0)),
        ],
        out_specs=pl.BlockSpec((EXPERT_BLOCK, d), lambda i, be, us: (i, 0)),
    )
    return pl.pallas_call(
        _expert_kernel,
        grid_spec=grid_spec,
        out_shape=jax.ShapeDtypeStruct((n_slots, d), F32),
        compiler_params=_cparams(("arbitrary",), 48),
        name="moe_experts",
    )(bexp, used, xpad, w_gate, w_up, w_down)


def _combine_kernel(dest_ref, h_ref, rw_ref, g_ref, b_ref, y_ref, o_ref, buf_ref, sem, *, alpha):
    def row_copy(r, j):
        return pltpu.make_async_copy(y_ref.at[pl.ds(dest_ref[EXPERT_TOPK * r + j], 1)],
                                     buf_ref.at[j, pl.ds(r, 1)], sem)

    def start(r, carry):
        for j in range(EXPERT_TOPK):
            row_copy(r, j).start(priority=j)
        return carry

    lax.fori_loop(0, MOE_ROWS, start, 0, unroll=DMA_UNROLL)

    def wait(r, carry):
        for j in range(EXPERT_TOPK):
            row_copy(r, j).wait()
        return carry

    lax.fori_loop(0, MOE_ROWS, wait, 0, unroll=DMA_UNROLL)
    moe = buf_ref[0] * rw_ref[:, 0:1] + buf_ref[1] * rw_ref[:, 1:2]
    o_ref[...] = _layer_norm(alpha * h_ref[...] + moe, g_ref[...], b_ref[...])


def _moe_combine(dest_flat, h, rw, g, b_, ypad, alpha):
    n, d = h.shape
    return pl.pallas_call(
        functools.partial(_combine_kernel, alpha=alpha),
        grid=(n // MOE_ROWS,),
        in_specs=[pl.BlockSpec((EXPERT_TOPK * MOE_ROWS,), lambda i: (i,), memory_space=pltpu.SMEM),
                  pl.BlockSpec((MOE_ROWS, d), lambda i: (i, 0)),
                  pl.BlockSpec((MOE_ROWS, LANE), lambda i: (i, 0)),
                  pl.BlockSpec((1, d), lambda i: (0, 0)),
                  pl.BlockSpec((1, d), lambda i: (0, 0)),
                  pl.BlockSpec(memory_space=pl.ANY)],
        out_specs=pl.BlockSpec((MOE_ROWS, d), lambda i: (i, 0)),
        out_shape=jax.ShapeDtypeStruct((n, d), F32),
        scratch_shapes=[pltpu.VMEM((EXPERT_TOPK, MOE_ROWS, d), F32), pltpu.SemaphoreType.DMA(())],
        compiler_params=_cparams(("arbitrary",), 48),
        name="moe_combine",
    )(dest_flat, h, rw, g, b_, ypad)


def _cmp_map_t(t):
    nc = (t - CMP_BLOCK) // CMP_STRIDE + 1
    ns = t // SLC_BLOCK
    ncp = t // CMP_STRIDE
    cs = CMP_STRIDE * np.arange(nc)[:, None]
    ss = SLC_BLOCK * np.arange(ns)[None, :]
    ov = np.minimum(cs + CMP_BLOCK, ss + SLC_BLOCK) - np.maximum(cs, ss)
    m = np.clip(ov, 0, None).astype(np.float32) / CMP_STRIDE
    out = np.zeros((ns, ncp), np.float32)
    out[:, :nc] = m.T
    return jnp.asarray(out, BF16)


def _block_expand(t):
    ns = t // SLC_BLOCK
    rows = max(ns, LANE)
    e = np.zeros((rows, t), np.float32)
    e[np.arange(t) // SLC_BLOCK, np.arange(t)] = 1.0
    return jnp.asarray(e, BF16)


def _pad_head_rows(w, n_heads):
    wh = w.reshape(n_heads, HEAD_DIM, w.shape[-1])
    return jnp.concatenate([wh, jnp.zeros_like(wh)], axis=1).reshape(n_heads * LANE, w.shape[-1]).astype(BF16)


def kernel(x, w_in, cmp_pe_k, cmp_pe_v, cmp_w1_k, cmp_w2_k, cmp_w1_v, cmp_w2_v, ckv_norm_g, w_uk, w_uv, rel_bias,
           w_branch_a, w_branch_b, w_out, ln1_g, ln1_b, w_grp, b_grp, w_rtr, b_rtr, w_gate, w_up, w_down, ln2_g,
           ln2_b):
    b, t, d = x.shape
    n = b * t
    depth = w_in.shape[0]
    alpha = (2.0 * depth) ** 0.25
    assert t % 512 == 0 and t >= WIN_KEYS and n % MOE_ROWS == 0

    sslc = _bias_strip(rel_bias, NSA_HEADS, 0, STRIP_A, STRIP_W, None, True)
    sdsa = _bias_strip(rel_bias, DSA_HEADS, NSA_HEADS, STRIP_A, STRIP_W, None, True)
    swin = _bias_strip(rel_bias, NSA_HEADS, 0, WIN_A, WIN_W, WINDOW, False)
    bcmp = _cmp_bias(rel_bias, t)
    mapt = _cmp_map_t(t)
    eall = _block_expand(t)

    n_a = n * EXPERT_TOPK
    n_blk = -(-n_a // EXPERT_BLOCK) + N_EXPERTS
    n_slots = n_blk * EXPERT_BLOCK

    h = x.reshape(n, d)
    for l in range(depth):
        w_pad = _proj_weights(w_in[l])
        wuk = w_uk[l]
        z = jnp.zeros_like(wuk[0])
        wuk_pairs = jnp.stack([
            jnp.concatenate([jnp.concatenate([wuk[2 * k], z], axis=1), jnp.concatenate([z, wuk[2 * k + 1]], axis=1)],
                            axis=0) for k in range(DSA_HEADS // 2)]).astype(BF16)
        wuv_pad = jnp.pad(w_uv[l], ((0, 0), (0, 0), (0, LANE - HEAD_DIM))).astype(BF16)
        wk = _cmp_weights(cmp_w1_k[l], cmp_w2_k[l], cmp_pe_k[l], HEAD_DIM)
        wv = _cmp_weights(cmp_w1_v[l], cmp_w2_v[l], cmp_pe_v[l], LANE)
        wa_pad = _pad_head_rows(w_branch_a[l], NSA_HEADS)
        wb_pad = _pad_head_rows(w_branch_b[l], DSA_HEADS)
        wr = jnp.zeros((d, LANE), F32).at[:, RT_GRP:RT_GRP + N_EXPERT_GROUPS].set(w_grp[l])
        wr = wr.at[:, RT_EXP:RT_EXP + N_EXPERTS].set(w_rtr[l])
        wr_hi = wr.astype(BF16)
        wr = jnp.concatenate([wr_hi, (wr - wr_hi.astype(F32)).astype(BF16)], axis=1)
        br = jnp.zeros((1, LANE), F32).at[0, RT_GRP:RT_GRP + N_EXPERT_GROUPS].set(b_grp[l])
        br = br.at[0, RT_EXP:RT_EXP + N_EXPERTS].set(b_rtr[l])

        (qa, kc, vc, kv, gn, qlat, ckvn, qidx, kidx, widx, sga, sgb) = _proj(
            h, w_pad, wuk_pairs, ckv_norm_g[l].reshape(1, KV_RANK))
        kcmp, vcmp = _compress(kc, vc, wk, wv, b, t)

        def b3(a):
            return a.reshape(b, t, a.shape[-1])

        oa = _nsa(b3(qa), b3(gn), kcmp, vcmp, b3(kv), sslc, swin, bcmp, mapt, eall, b, t)
        ob = _dsa(b3(qlat), b3(qidx), b3(widx), b3(kidx), b3(ckvn), sdsa, wuv_pad, b, t)

        h1, rt, rw = _post(h, oa.reshape(n, -1), ob.reshape(n, -1), sga, sgb, wa_pad, wb_pad, w_out[l].astype(BF16),
                           ln1_g[l].reshape(1, d), ln1_b[l].reshape(1, d), wr, br, alpha)

        rank, cnt = _moe_rank(rt)
        dest, bexp = _moe_dest(rt, rank, cnt, n_blk)
        dest_flat = dest[:, :EXPERT_TOPK].reshape(n_a)
        xpad = _moe_dispatch(dest_flat, h1, n_slots)
        ypad = _moe_experts(bexp[:n_blk, 0], bexp[:1, 1], xpad, w_gate[l].astype(BF16), w_up[l].astype(BF16),
                            w_down[l].astype(BF16))
        h = _moe_combine(dest_flat, h1, rw, ln2_g[l].reshape(1, d), ln2_b[l].reshape(1, d), ypad, alpha)
    return h.reshape(b, t, d)
```
